```python
import jax, jax.numpy as jnp
from jax import lax
import numpy as np

D_MODEL = 2048
BATCH = 4
SEQ = 4096
DEPTH = 1

MIX_WIDTH = D_MODEL
CONV_WIDTH = MIX_WIDTH // 2
CONV_GROUPS = 8
CONV_K = 3
DN_HEADS = 8
DN_HEAD_DIM = 128
DN_WIDTH = DN_HEADS * DN_HEAD_DIM
DN_CONV_K = 4
CHUNK = 64
D_FF = 5632
FFN_CONV_K = 3
PLE_DIM = 256
EPS = 1e-6
IN_COLS = 3 * CONV_WIDTH + 4 * DN_WIDTH + 2 * DN_HEADS

kernel_name = "hybrid_shortconv_gated_deltanet_convffn_ple"


def rmsnorm(x, g):
    xf = x.astype(jnp.float32)
    y = xf * lax.rsqrt(jnp.mean(xf * xf, axis=-1, keepdims=True) + EPS) * g.astype(jnp.float32)
    return y.astype(x.dtype)


def causal_dwconv(x, w):
    K = w.shape[0]
    S = x.shape[1]
    xp = jnp.pad(x, ((0, 0), (K - 1, 0), (0, 0)))
    y = xp[:, 0:S] * w[0]
    for j in range(1, K):
        y = y + xp[:, j:j + S] * w[j]
    return y


def l2norm(x):
    return x * lax.rsqrt(jnp.sum(x * x, axis=-1, keepdims=True) + EPS)


def chunk_gated_delta(q, k, v, g, beta):
    B, H, S, dk = q.shape
    dv = v.shape[-1]
    N = S // CHUNK
    q = q * (dk ** -0.5)
    qc = q.reshape(B, H, N, CHUNK, dk)
    kc = k.reshape(B, H, N, CHUNK, dk)
    vc = v.reshape(B, H, N, CHUNK, dv)
    bc = beta.reshape(B, H, N, CHUNK)
    gcum = jnp.cumsum(g.reshape(B, H, N, CHUNK), axis=-1)
    idx = jnp.arange(CHUNK)
    causal = idx[:, None] >= idx[None, :]
    strict = idx[:, None] > idx[None, :]
    diff = gcum[..., :, None] - gcum[..., None, :]
    decay = jnp.exp(jnp.where(causal, diff, -jnp.inf))
    kk = jnp.einsum('bhncd,bhnmd->bhncm', kc, kc)
    L = jnp.where(strict, kk * decay * bc[..., :, None], 0.0)
    A = L + jnp.eye(CHUNK, dtype=jnp.float32)
    rhs = jnp.concatenate([vc * bc[..., None],
                           kc * (bc * jnp.exp(gcum))[..., None]], axis=-1)
    sol = lax.linalg.triangular_solve(A, rhs, left_side=True, lower=True)
    u = sol[..., :dv]
    w = sol[..., dv:]
    qk = jnp.einsum('bhncd,bhnmd->bhncm', qc, kc) * decay
    q_dec = qc * jnp.exp(gcum)[..., None]
    k_dec = kc * jnp.exp(gcum[..., -1:] - gcum)[..., None]
    g_last = jnp.exp(gcum[..., -1])

    def step(state, inp):
        u_n, w_n, qk_n, qd_n, kd_n, gl_n = inp
        v_new = u_n - jnp.einsum('bhcd,bhde->bhce', w_n, state)
        o = (jnp.einsum('bhcd,bhde->bhce', qd_n, state)
             + jnp.einsum('bhcm,bhme->bhce', qk_n, v_new))
        state = state * gl_n[..., None, None] + jnp.einsum('bhcd,bhce->bhde', kd_n, v_new)
        return state, o

    to_front = lambda t: jnp.moveaxis(t, 2, 0)
    xs = (to_front(u), to_front(w), to_front(qk), to_front(q_dec), to_front(k_dec),
          jnp.moveaxis(g_last, 2, 0))
    s0 = jnp.zeros((B, H, dk, dv), jnp.float32)
    _, o = lax.scan(step, s0, xs)
    return jnp.moveaxis(o, 0, 2).reshape(B, H, S, dv)


def hybrid_layer(x, p_i, norm_mix_g, w_in, conv_a_w, conv_qkv_w, a_log, dt_bias, dn_norm_g,
                 w_out, norm_ffn_g, w_up, conv_ffn_w, w_down, norm_ple_g, w_ple_gate, w_ple_proj):
    Bsz, S, _ = x.shape
    h = rmsnorm(x, norm_mix_g)
    proj = h @ w_in
    s1 = CONV_WIDTH
    s2 = 2 * CONV_WIDTH
    s3 = 3 * CONV_WIDTH
    s4 = s3 + 3 * DN_WIDTH
    s5 = s4 + DN_WIDTH
    s6 = s5 + DN_HEADS
    a_x, a_b, a_c, qkv, z, a_dec, b_beta = jnp.split(proj, [s1, s2, s3, s4, s5, s6], axis=-1)

    y_a = a_b * causal_dwconv(a_c * a_x, conv_a_w)

    qkv = jax.nn.silu(causal_dwconv(qkv, conv_qkv_w)).astype(jnp.float32)
    q, k, v = jnp.split(qkv, 3, axis=-1)
    q = l2norm(q.reshape(Bsz, S, DN_HEADS, DN_HEAD_DIM))
    k = l2norm(k.reshape(Bsz, S, DN_HEADS, DN_HEAD_DIM))
    v = v.reshape(Bsz, S, DN_HEADS, DN_HEAD_DIM)
    g = -jnp.exp(a_log.astype(jnp.float32)) * jax.nn.softplus(
        a_dec.astype(jnp.float32) + dt_bias.astype(jnp.float32))
    beta = jax.nn.sigmoid(b_beta.astype(jnp.float32))
    tr = lambda t: jnp.swapaxes(t, 1, 2)
    o = chunk_gated_delta(tr(q), tr(k), tr(v), tr(g), tr(beta))
    o = tr(o)
    zf = z.astype(jnp.float32).reshape(Bsz, S, DN_HEADS, DN_HEAD_DIM)
    o = (o * lax.rsqrt(jnp.mean(o * o, axis=-1, keepdims=True) + EPS)
         * dn_norm_g.astype(jnp.float32) * jax.nn.silu(zf))
    y_b = o.reshape(Bsz, S, DN_WIDTH).astype(x.dtype)

    x = x + jnp.concatenate([y_a, y_b], axis=-1) @ w_out

    h = rmsnorm(x, norm_ffn_g)
    up = causal_dwconv(h @ w_up, conv_ffn_w)
    gate, val = jnp.split(up, 2, axis=-1)
    x = x + (jax.nn.silu(gate) * val) @ w_down

    ple_gate = jax.nn.sigmoid(rmsnorm(x, norm_ple_g) @ w_ple_gate)
    x = x + ple_gate * (p_i @ w_ple_proj)
    return x


def setup_inputs(seed: int = 0) -> dict:
    key = jax.random.key(seed)
    ks = jax.random.split(key, 20)
    f32 = jnp.float32
    nrm = lambda k, shape, scale: jax.random.normal(k, shape, f32) * scale
    gain = lambda k, shape: 1.0 + 0.02 * jax.random.normal(k, shape, f32)
    return {
        "x": jax.random.normal(ks[0], (BATCH, SEQ, D_MODEL), f32),
        "p": jax.random.normal(ks[1], (DEPTH, BATCH, SEQ, PLE_DIM), f32),
        "norm_mix_g": gain(ks[2], (DEPTH, D_MODEL)),
        "w_in": nrm(ks[3], (DEPTH, D_MODEL, IN_COLS), D_MODEL ** -0.5),
        "conv_a_w": nrm(ks[4], (DEPTH, CONV_K, CONV_WIDTH), CONV_K ** -0.5),
        "conv_qkv_w": nrm(ks[5], (DEPTH, DN_CONV_K, 3 * DN_WIDTH), DN_CONV_K ** -0.5),
        "a_log": jnp.log(jax.random.uniform(ks[6], (DEPTH, DN_HEADS), f32, 1.0, 16.0)),
        "dt_bias": 0.1 * jax.random.normal(ks[7], (DEPTH, DN_HEADS), f32),
        "dn_norm_g": gain(ks[8], (DEPTH, DN_HEAD_DIM)),
        "w_out": nrm(ks[9], (DEPTH, MIX_WIDTH, D_MODEL), MIX_WIDTH ** -0.5),
        "norm_ffn_g": gain(ks[10], (DEPTH, D_MODEL)),
        "w_up": nrm(ks[11], (DEPTH, D_MODEL, 2 * D_FF), D_MODEL ** -0.5),
        "conv_ffn_w": nrm(ks[12], (DEPTH, FFN_CONV_K, 2 * D_FF), FFN_CONV_K ** -0.5),
        "w_down": nrm(ks[13], (DEPTH, D_FF, D_MODEL), D_FF ** -0.5),
        "norm_ple_g": gain(ks[14], (DEPTH, D_MODEL)),
        "w_ple_gate": nrm(ks[15], (DEPTH, D_MODEL, D_MODEL), D_MODEL ** -0.5),
        "w_ple_proj": nrm(ks[16], (DEPTH, PLE_DIM, D_MODEL), PLE_DIM ** -0.5),
        "final_norm_g": gain(ks[17], (D_MODEL,)),
    }


def reference(x, p, norm_mix_g, w_in, conv_a_w, conv_qkv_w, a_log, dt_bias, dn_norm_g,
              w_out, norm_ffn_g, w_up, conv_ffn_w, w_down, norm_ple_g, w_ple_gate,
              w_ple_proj, final_norm_g):
    for i in range(DEPTH):
        x = hybrid_layer(x, p[i], norm_mix_g[i], w_in[i], conv_a_w[i], conv_qkv_w[i],
                         a_log[i], dt_bias[i], dn_norm_g[i], w_out[i], norm_ffn_g[i],
                         w_up[i], conv_ffn_w[i], w_down[i], norm_ple_g[i], w_ple_gate[i],
                         w_ple_proj[i])
    return rmsnorm(x, final_norm_g)
```

```python
import functools

import jax
import jax.numpy as jnp
from jax import lax
from jax.experimental import pallas as pl
from jax.experimental.pallas import tpu as pltpu

D_MODEL = 2048
CONV_WIDTH = 1024
CONV_K = 3
DN_HEADS = 8
DN_HEAD_DIM = 128
DN_WIDTH = DN_HEADS * DN_HEAD_DIM
DN_CONV_K = 4
D_FF = 5632
FFN_CONV_K = 3
PLE_DIM = 256
EPS = 1e-6
PROJ_COLS = 3 * CONV_WIDTH + 4 * DN_WIDTH

LANES = 128
SUBLANES_F32 = 8
SUBLANES_BF16 = 16
VMEM_LIMIT_BYTES = 56 * 1024 * 1024

DN_CHUNK = 128
INV_BASE = 8

F32 = jnp.float32
BF16 = jnp.bfloat16


def _mm(a, b):
    return jnp.dot(a.astype(BF16), b.astype(BF16), preferred_element_type=F32)


def _mm_nt(a, b):
    return lax.dot_general(a.astype(BF16), b.astype(BF16), (((1,), (1,)), ((), ())),
                           preferred_element_type=F32)


def _mm_exact(a, b):
    return jnp.dot(a, b, precision=lax.Precision.HIGHEST, preferred_element_type=F32)


def _rms_scale(x):
    return lax.rsqrt(jnp.mean(x * x, axis=-1, keepdims=True) + EPS)


def _silu(x):
    return x * jax.nn.sigmoid(x)


def _params(semantics):
    return pltpu.CompilerParams(dimension_semantics=semantics, vmem_limit_bytes=VMEM_LIMIT_BYTES)


def _in_proj_kernel(x_ref, g_ref, w_ref, wg_ref, alog_ref, dtb_ref, proj_ref, gates_ref, hs_ref):
    @pl.when(pl.program_id(1) == 0)
    def _():
        x = x_ref[...]
        hb = (x * _rms_scale(x) * g_ref[...]).astype(BF16)
        hs_ref[...] = hb
        raw = jnp.dot(hb, wg_ref[...], preferred_element_type=F32)
        a = raw + dtb_ref[...]
        softplus = jnp.maximum(a, 0.0) + jnp.log(1.0 + jnp.exp(-jnp.abs(a)))
        decay = -jnp.exp(alog_ref[...]) * softplus
        lane = lax.broadcasted_iota(jnp.int32, raw.shape, 1)
        gates_ref[...] = jnp.where(lane < DN_HEADS, decay, jax.nn.sigmoid(raw))

    proj_ref[...] = jnp.dot(hs_ref[...], w_ref[...], preferred_element_type=F32)


def _in_proj(x2, g, w_main, w_gate, alog_pad, dtb_pad, *, tm, tn):
    m = x2.shape[0]
    return pl.pallas_call(
        _in_proj_kernel,
        grid=(m // tm, PROJ_COLS // tn),
        in_specs=[
            pl.BlockSpec((tm, D_MODEL), lambda i, j: (i, 0)),
            pl.BlockSpec((1, D_MODEL), lambda i, j: (0, 0)),
            pl.BlockSpec((D_MODEL, tn), lambda i, j: (0, j)),
            pl.BlockSpec((D_MODEL, LANES), lambda i, j: (0, 0)),
            pl.BlockSpec((1, LANES), lambda i, j: (0, 0)),
            pl.BlockSpec((1, LANES), lambda i, j: (0, 0)),
        ],
        out_specs=[
            pl.BlockSpec((tm, tn), lambda i, j: (i, j)),
            pl.BlockSpec((tm, LANES), lambda i, j: (i, 0)),
        ],
        out_shape=[
            jax.ShapeDtypeStruct((m, PROJ_COLS), F32),
            jax.ShapeDtypeStruct((m, LANES), F32),
        ],
        scratch_shapes=[pltpu.VMEM((tm, D_MODEL), BF16)],
        compiler_params=_params(("parallel", "arbitrary")),
        name="in_proj",
    )(x2, g, w_main, w_gate, alog_pad, dtb_pad)


def _unit_lower_inverse(low, row, col):
    c = low.shape[0]
    eye = (row == col).astype(F32)
    base_shift = INV_BASE.bit_length() - 1
    l0 = jnp.where((row >> base_shift) == (col >> base_shift), low, 0.0)
    l2 = _mm(l0, l0)
    l4 = _mm(l2, l2)
    p1 = eye - l0 + l2 - _mm(l0, l2)
    inv = p1 + _mm(p1, l4)
    shift = base_shift
    while (1 << shift) < c:
        off = jnp.where(((row >> (shift + 1)) == (col >> (shift + 1)))
                        & ((row >> shift) != (col >> shift)), low, 0.0)
        inv = inv - _mm(inv, _mm(off, inv))
        shift += 1
    return inv


def _deltanet_kernel(q_ref, k_ref, v_ref, z_ref, gates_ref, cw_ref, ng_ref, out_ref,
                     qbuf, kbuf, vbuf, state_ref, *, ts, chunk):
    halo = SUBLANES_F32
    first = pl.program_id(1) == 0

    @pl.when(first)
    def _():
        state_ref[...] = jnp.zeros_like(state_ref)
        for buf in (qbuf, kbuf, vbuf):
            buf[pl.ds(0, halo), :] = jnp.zeros((halo, DN_WIDTH), F32)

    @pl.when(jnp.logical_not(first))
    def _():
        for buf in (qbuf, kbuf, vbuf):
            buf[pl.ds(0, halo), :] = buf[pl.ds(ts, halo), :]

    for src, buf in ((q_ref, qbuf), (k_ref, kbuf), (v_ref, vbuf)):
        buf[pl.ds(halo, ts), :] = src[...]

    row = lax.broadcasted_iota(jnp.int32, (chunk, chunk), 0)
    col = lax.broadcasted_iota(jnp.int32, (chunk, chunk), 1)
    lower_incl = (row >= col).astype(F32)
    causal = row >= col
    strict = row > col
    scale = DN_HEAD_DIM ** -0.5

    def conv_silu(buf, r0, which):
        window = buf[pl.ds(r0, chunk + halo), :]
        acc = None
        for j in range(DN_CONV_K):
            w = cw_ref[pl.ds(j, 1), pl.ds(which * DN_WIDTH, DN_WIDTH)]
            back = DN_CONV_K - 1 - j
            shifted = window if back == 0 else pltpu.roll(window, back, axis=0)
            term = shifted[halo:, :] * w
            acc = term if acc is None else acc + term
        return _silu(acc)

    def chunk_body(c, carry):
        r0 = pl.multiple_of(c * chunk, chunk)
        qa = conv_silu(qbuf, r0, 0)
        ka = conv_silu(kbuf, r0, 1)
        va = conv_silu(vbuf, r0, 2)
        gts = gates_ref[pl.ds(r0, chunk), :]
        gcum = _mm_exact(lower_incl, gts)
        gcum_t = gcum.T
        for h in range(DN_HEADS):
            sl = slice(h * DN_HEAD_DIM, (h + 1) * DN_HEAD_DIM)
            q = qa[:, sl]
            k = ka[:, sl]
            v = va[:, sl]
            q = q * (lax.rsqrt(jnp.sum(q * q, axis=-1, keepdims=True) + EPS) * scale)
            k = k * lax.rsqrt(jnp.sum(k * k, axis=-1, keepdims=True) + EPS)
            gc = gcum[:, h:h + 1]
            gr = gcum_t[h:h + 1, :]
            beta = gts[:, DN_HEADS + h:DN_HEADS + h + 1]
            g_last = gc[chunk - 1:chunk, :]
            decay = jnp.exp(jnp.where(causal, gc - gr, -1e30))
            kk = _mm_nt(k, k)
            low = jnp.where(strict, kk * decay * beta, 0.0)
            inv = _unit_lower_inverse(low, row, col)
            eg = jnp.exp(gc)
            rhs = jnp.concatenate([v * beta, k * (beta * eg)], axis=1)
            sol = _mm(inv, rhs)
            u = sol[:, :DN_HEAD_DIM]
            w = sol[:, DN_HEAD_DIM:]
            qk = _mm_nt(q, k) * decay
            q_dec = q * eg
            k_dec = k * jnp.exp(g_last - gc)
            state = state_ref[h]
            ws_qs = _mm(jnp.concatenate([w, q_dec], axis=0), state)
            v_new = u - ws_qs[:chunk]
            o = ws_qs[chunk:] + _mm(qk, v_new)
            state_ref[h] = state * jnp.exp(g_last) + _mm(k_dec.T, v_new)
            z = z_ref[pl.ds(r0, chunk), sl]
            y = o * _rms_scale(o) * ng_ref[...] * _silu(z)
            out_ref[pl.ds(r0, chunk), sl] = y.astype(out_ref.dtype)
        return carry

    lax.fori_loop(0, ts // chunk, chunk_body, 0)


def _deltanet(proj, gates, conv_w, norm_g, *, seq, ts, chunk):
    m = proj.shape[0]
    tiles = seq // ts
    col0 = 3 * CONV_WIDTH // DN_WIDTH
    row_map = lambda b, t: b * tiles + t
    qkvz = [pl.BlockSpec((ts, DN_WIDTH), functools.partial(lambda b, t, n: (b * tiles + t, col0 + n), n=n))
            for n in range(4)]
    return pl.pallas_call(
        functools.partial(_deltanet_kernel, ts=ts, chunk=chunk),
        grid=(m // seq, tiles),
        in_specs=qkvz + [
            pl.BlockSpec((ts, LANES), lambda b, t: (row_map(b, t), 0)),
            pl.BlockSpec((DN_CONV_K, 3 * DN_WIDTH), lambda b, t: (0, 0)),
            pl.BlockSpec((1, DN_HEAD_DIM), lambda b, t: (0, 0)),
        ],
        out_specs=pl.BlockSpec((ts, DN_WIDTH), lambda b, t: (row_map(b, t), 0)),
        out_shape=jax.ShapeDtypeStruct((m, DN_WIDTH), BF16),
        scratch_shapes=[pltpu.VMEM((ts + SUBLANES_F32, DN_WIDTH), F32)] * 3
        + [pltpu.VMEM((DN_HEADS, DN_HEAD_DIM, DN_HEAD_DIM), F32)],
        compiler_params=_params(("parallel", "arbitrary")),
        name="deltanet",
    )(proj, proj, proj, proj, gates, conv_w, norm_g)


def _out_proj_kernel(ax_ref, ab_ref, ac_ref, hax_ref, hac_ref, yb_ref, x_ref, cw_ref, w_ref,
                     out_ref, pbuf, *, tm, seq):
    halo = SUBLANES_F32
    seq_start = (pl.program_id(0) * tm) % seq == 0
    pbuf[pl.ds(0, halo), :] = jnp.where(seq_start, 0.0, hac_ref[...] * hax_ref[...])
    pbuf[pl.ds(halo, tm), :] = ac_ref[...] * ax_ref[...]
    acc = None
    for j in range(CONV_K):
        term = pbuf[pl.ds(halo - (CONV_K - 1) + j, tm), :] * cw_ref[pl.ds(j, 1), :]
        acc = term if acc is None else acc + term
    ya = (ab_ref[...] * acc).astype(BF16)
    out_ref[...] = (x_ref[...]
                    + jnp.dot(ya, w_ref[pl.ds(0, CONV_WIDTH), :], preferred_element_type=F32)
                    + jnp.dot(yb_ref[...], w_ref[pl.ds(CONV_WIDTH, DN_WIDTH), :],
                              preferred_element_type=F32))


def _out_proj(proj, yb, x2, conv_w, w_out, *, tm, seq):
    m = x2.shape[0]
    halo = SUBLANES_F32
    halo_row = lambda i: jnp.maximum(i * (tm // halo) - 1, 0)
    return pl.pallas_call(
        functools.partial(_out_proj_kernel, tm=tm, seq=seq),
        grid=(m // tm,),
        in_specs=[
            pl.BlockSpec((tm, CONV_WIDTH), lambda i: (i, 0)),
            pl.BlockSpec((tm, CONV_WIDTH), lambda i: (i, 1)),
            pl.BlockSpec((tm, CONV_WIDTH), lambda i: (i, 2)),
            pl.BlockSpec((halo, CONV_WIDTH), lambda i: (halo_row(i), 0)),
            pl.BlockSpec((halo, CONV_WIDTH), lambda i: (halo_row(i), 2)),
            pl.BlockSpec((tm, DN_WIDTH), lambda i: (i, 0)),
            pl.BlockSpec((tm, D_MODEL), lambda i: (i, 0)),
            pl.BlockSpec((CONV_K, CONV_WIDTH), lambda i: (0, 0)),
            pl.BlockSpec((D_MODEL, D_MODEL), lambda i: (0, 0)),
        ],
        out_specs=pl.BlockSpec((tm, D_MODEL), lambda i: (i, 0)),
        out_shape=jax.ShapeDtypeStruct((m, D_MODEL), F32),
        scratch_shapes=[pltpu.VMEM((tm + halo, CONV_WIDTH), F32)],
        compiler_params=_params(("parallel",)),
        name="out_proj",
    )(proj, proj, proj, proj, proj, yb, x2, conv_w, w_out)


def _ffn_kernel(x_ref, hx_ref, g_ref, wg_ref, wv_ref, cg_ref, cv_ref, wd_ref, out_ref,
                hs_ref, ubuf, *, tm, seq):
    halo = SUBLANES_BF16
    j = pl.program_id(1)

    @pl.when(j == 0)
    def _():
        x = x_ref[...]
        hs_ref[pl.ds(0, tm), :] = (x * _rms_scale(x) * g_ref[...]).astype(BF16)
        hx = hx_ref[...]
        seq_start = (pl.program_id(0) * tm) % seq == 0
        hh = jnp.where(seq_start, 0.0, hx * _rms_scale(hx) * g_ref[...])
        hs_ref[pl.ds(tm, halo), :] = hh.astype(BF16)
        out_ref[...] = x

    def up_conv(w_ref, cw_ref):
        up = jnp.dot(hs_ref[...], w_ref[...], preferred_element_type=F32)
        ubuf[pl.ds(0, halo), :] = up[tm:, :]
        ubuf[pl.ds(halo, tm), :] = up[:tm, :]
        acc = None
        for t in range(FFN_CONV_K):
            term = ubuf[pl.ds(halo - (FFN_CONV_K - 1) + t, tm), :] * cw_ref[pl.ds(t, 1), :]
            acc = term if acc is None else acc + term
        return acc

    gate = up_conv(wg_ref, cg_ref)
    act = _silu(gate)
    val = up_conv(wv_ref, cv_ref)
    act = (act * val).astype(BF16)
    out_ref[...] += jnp.dot(act, wd_ref[...], preferred_element_type=F32)


def _ffn(x1, g, w_up, conv_w, w_down, *, tm, bn, seq):
    m = x1.shape[0]
    halo = SUBLANES_BF16
    nj = D_FF // bn
    return pl.pallas_call(
        functools.partial(_ffn_kernel, tm=tm, seq=seq),
        grid=(m // tm, nj),
        in_specs=[
            pl.BlockSpec((tm, D_MODEL), lambda i, j: (i, 0)),
            pl.BlockSpec((halo, D_MODEL), lambda i, j: (jnp.maximum(i * (tm // halo) - 1, 0), 0)),
            pl.BlockSpec((1, D_MODEL), lambda i, j: (0, 0)),
            pl.BlockSpec((D_MODEL, bn), lambda i, j: (0, j)),
            pl.BlockSpec((D_MODEL, bn), lambda i, j: (0, j + nj)),
            pl.BlockSpec((FFN_CONV_K, bn), lambda i, j: (0, j)),
            pl.BlockSpec((FFN_CONV_K, bn), lambda i, j: (0, j + nj)),
            pl.BlockSpec((bn, D_MODEL), lambda i, j: (j, 0)),
        ],
        out_specs=pl.BlockSpec((tm, D_MODEL), lambda i, j: (i, 0)),
        out_shape=jax.ShapeDtypeStruct((m, D_MODEL), F32),
        scratch_shapes=[pltpu.VMEM((tm + halo, D_MODEL), BF16),
                        pltpu.VMEM((tm + halo, bn), F32)],
        compiler_params=_params(("parallel", "arbitrary")),
        name="conv_ffn",
    )(x1, x1, g, w_up, w_up, conv_w, conv_w, w_down)


def _ple_kernel(x_ref, p_ref, g_ref, wpg_ref, wpp_ref, fg_ref, out_ref):
    x = x_ref[...]
    hb = (x * _rms_scale(x) * g_ref[...]).astype(BF16)
    gate = jax.nn.sigmoid(jnp.dot(hb, wpg_ref[...], preferred_element_type=F32))
    emb = jnp.dot(p_ref[...].astype(BF16), wpp_ref[...], preferred_element_type=F32)
    y = x + gate * emb
    out_ref[...] = y * _rms_scale(y) * fg_ref[...]


def _ple(x2, p2, g, w_pg, w_pp, final_g, *, tm):
    m = x2.shape[0]
    return pl.pallas_call(
        _ple_kernel,
        grid=(m // tm,),
        in_specs=[
            pl.BlockSpec((tm, D_MODEL), lambda i: (i, 0)),
            pl.BlockSpec((tm, PLE_DIM), lambda i: (i, 0)),
            pl.BlockSpec((1, D_MODEL), lambda i: (0, 0)),
            pl.BlockSpec((D_MODEL, D_MODEL), lambda i: (0, 0)),
            pl.BlockSpec((PLE_DIM, D_MODEL), lambda i: (0, 0)),
            pl.BlockSpec((1, D_MODEL), lambda i: (0, 0)),
        ],
        out_specs=pl.BlockSpec((tm, D_MODEL), lambda i: (i, 0)),
        out_shape=jax.ShapeDtypeStruct((m, D_MODEL), F32),
        compiler_params=_params(("parallel",)),
        name="ple",
    )(x2, p2, g, w_pg, w_pp, final_g)


def _layer(x2, p2, seq, norm_mix_g, w_in, conv_a_w, conv_qkv_w, a_log, dt_bias, dn_norm_g,
           w_out, norm_ffn_g, w_up, conv_ffn_w, w_down, norm_ple_g, w_ple_gate, w_ple_proj,
           out_norm_g):
    row = lambda v: v.reshape(1, -1).astype(F32)
    lane_pad = lambda v: jnp.pad(row(v), ((0, 0), (0, LANES - v.shape[-1])))
    w_main = w_in[:, :PROJ_COLS].astype(BF16)
    w_gate = jnp.pad(w_in[:, PROJ_COLS:], ((0, 0), (0, LANES - 2 * DN_HEADS))).astype(BF16)

    proj, gates = _in_proj(x2, row(norm_mix_g), w_main, w_gate, lane_pad(a_log), lane_pad(dt_bias),
                           tm=1024, tn=512)
    yb = _deltanet(proj, gates, conv_qkv_w.astype(F32), row(dn_norm_g), seq=seq, ts=512,
                   chunk=DN_CHUNK)
    x2 = _out_proj(proj, yb, x2, conv_a_w.astype(F32), w_out.astype(BF16), tm=512, seq=seq)
    x2 = _ffn(x2, row(norm_ffn_g), w_up.astype(BF16), conv_ffn_w.astype(F32), w_down.astype(BF16),
              tm=512, bn=512, seq=seq)
    return _ple(x2, p2, row(norm_ple_g), w_ple_gate.astype(BF16), w_ple_proj.astype(BF16),
                row(out_norm_g), tm=512)


def kernel(x, p, norm_mix_g, w_in, conv_a_w, conv_qkv_w, a_log, dt_bias, dn_norm_g, w_out,
           norm_ffn_g, w_up, conv_ffn_w, w_down, norm_ple_g, w_ple_gate, w_ple_proj, final_norm_g):
    batch, seq, d_model = x.shape
    depth = p.shape[0]
    assert depth == 1 and d_model == D_MODEL
    x2 = x.reshape(batch * seq, d_model)
    p2 = p[0].reshape(batch * seq, PLE_DIM)
    out = _layer(x2, p2, seq, norm_mix_g[0], w_in[0], conv_a_w[0], conv_qkv_w[0], a_log[0],
                 dt_bias[0], dn_norm_g[0], w_out[0], norm_ffn_g[0], w_up[0], conv_ffn_w[0],
                 w_down[0], norm_ple_g[0], w_ple_gate[0], w_ple_proj[0], final_norm_g)
    return out.reshape(batch, seq, d_model)
```

```python
import functools

import jax
import jax.numpy as jnp
from jax import lax
from jax.experimental import pallas as pl
from jax.experimental.pallas import tpu as pltpu

D_MODEL = 2048
CONV_WIDTH = 1024
CONV_K = 3
DN_HEADS = 8
DN_HEAD_DIM = 128
DN_WIDTH = DN_HEADS * DN_HEAD_DIM
DN_CONV_K = 4
D_FF = 5632
FFN_CONV_K = 3
PLE_DIM = 256
EPS = 1e-6
PROJ_COLS = 3 * CONV_WIDTH + 4 * DN_WIDTH

LANES = 128
SUBLANES_F32 = 8
SUBLANES_BF16 = 16
VMEM_LIMIT_BYTES = 56 * 1024 * 1024

DN_CHUNK = 128
INV_BASE = 8

F32 = jnp.float32
BF16 = jnp.bfloat16


def _mm(a, b):
    return jnp.dot(a.astype(BF16), b.astype(BF16), preferred_element_type=F32)


def _mm_nt(a, b):
    return lax.dot_general(a.astype(BF16), b.astype(BF16), (((1,), (1,)), ((), ())),
                           preferred_element_type=F32)


def _mm_exact(a, b):
    return jnp.dot(a, b, precision=lax.Precision.HIGHEST, preferred_element_type=F32)


def _rms_scale(x):
    return lax.rsqrt(jnp.mean(x * x, axis=-1, keepdims=True) + EPS)


def _silu(x):
    return x * jax.nn.sigmoid(x)


def _params(semantics):
    return pltpu.CompilerParams(dimension_semantics=semantics, vmem_limit_bytes=VMEM_LIMIT_BYTES)


def _in_proj_kernel(x_ref, g_ref, w_ref, wg_ref, alog_ref, dtb_ref, proj_ref, gates_ref, hs_ref):
    @pl.when(pl.program_id(1) == 0)
    def _():
        x = x_ref[...]
        hb = (x * _rms_scale(x) * g_ref[...]).astype(BF16)
        hs_ref[...] = hb
        raw = jnp.dot(hb, wg_ref[...], preferred_element_type=F32)
        a = raw + dtb_ref[...]
        softplus = jnp.maximum(a, 0.0) + jnp.log(1.0 + jnp.exp(-jnp.abs(a)))
        decay = -jnp.exp(alog_ref[...]) * softplus
        lane = lax.broadcasted_iota(jnp.int32, raw.shape, 1)
        gates_ref[...] = jnp.where(lane < DN_HEADS, decay, jax.nn.sigmoid(raw))

    proj_ref[...] = jnp.dot(hs_ref[...], w_ref[...], preferred_element_type=F32)


def _in_proj(x2, g, w_main, w_gate, alog_pad, dtb_pad, *, tm, tn):
    m = x2.shape[0]
    return pl.pallas_call(
        _in_proj_kernel,
        grid=(m // tm, PROJ_COLS // tn),
        in_specs=[
            pl.BlockSpec((tm, D_MODEL), lambda i, j: (i, 0)),
            pl.BlockSpec((1, D_MODEL), lambda i, j: (0, 0)),
            pl.BlockSpec((D_MODEL, tn), lambda i, j: (0, j)),
            pl.BlockSpec((D_MODEL, LANES), lambda i, j: (0, 0)),
            pl.BlockSpec((1, LANES), lambda i, j: (0, 0)),
            pl.BlockSpec((1, LANES), lambda i, j: (0, 0)),
        ],
        out_specs=[
            pl.BlockSpec((tm, tn), lambda i, j: (i, j)),
            pl.BlockSpec((tm, LANES), lambda i, j: (i, 0)),
        ],
        out_shape=[
            jax.ShapeDtypeStruct((m, PROJ_COLS), F32),
            jax.ShapeDtypeStruct((m, LANES), F32),
        ],
        scratch_shapes=[pltpu.VMEM((tm, D_MODEL), BF16)],
        compiler_params=_params(("parallel", "arbitrary")),
        name="in_proj",
    )(x2, g, w_main, w_gate, alog_pad, dtb_pad)


def _each(fn, *lists):
    return [fn(*args) for args in zip(*lists)]


def _unit_lower_inverse(lows, row, col):
    c = lows[0].shape[0]
    eye = (row == col).astype(F32)
    base_shift = INV_BASE.bit_length() - 1
    base_mask = (row >> base_shift) == (col >> base_shift)
    l0 = [jnp.where(base_mask, low, 0.0) for low in lows]
    l2 = _each(_mm, l0, l0)
    l4 = _each(_mm, l2, l2)
    l3 = _each(_mm, l0, l2)
    p1 = _each(lambda a, b, d: eye - a + b - d, l0, l2, l3)
    inv = _each(lambda p, pl4: p + pl4, p1, _each(_mm, p1, l4))
    shift = base_shift
    while (1 << shift) < c:
        off_mask = (((row >> (shift + 1)) == (col >> (shift + 1)))
                    & ((row >> shift) != (col >> shift)))
        off_inv = _each(lambda low, t: _mm(jnp.where(off_mask, low, 0.0), t), lows, inv)
        inv = _each(lambda t, x: t - _mm(t, x), inv, off_inv)
        shift += 1
    return inv


def _deltanet_kernel(q_ref, k_ref, v_ref, z_ref, gates_ref, cw_ref, ng_ref, out_ref,
                     qbuf, kbuf, vbuf, state_ref, *, ts, chunk):
    halo = SUBLANES_F32
    first = pl.program_id(1) == 0

    @pl.when(first)
    def _():
        state_ref[...] = jnp.zeros_like(state_ref)
        for buf in (qbuf, kbuf, vbuf):
            buf[pl.ds(0, halo), :] = jnp.zeros((halo, DN_WIDTH), F32)

    @pl.when(jnp.logical_not(first))
    def _():
        for buf in (qbuf, kbuf, vbuf):
            buf[pl.ds(0, halo), :] = buf[pl.ds(ts, halo), :]

    for src, buf in ((q_ref, qbuf), (k_ref, kbuf), (v_ref, vbuf)):
        buf[pl.ds(halo, ts), :] = src[...]

    row = lax.broadcasted_iota(jnp.int32, (chunk, chunk), 0)
    col = lax.broadcasted_iota(jnp.int32, (chunk, chunk), 1)
    lower_incl = (row >= col).astype(F32)
    causal = row >= col
    strict = row > col
    scale = DN_HEAD_DIM ** -0.5
    heads = range(DN_HEADS)
    head_cols = [slice(h * DN_HEAD_DIM, (h + 1) * DN_HEAD_DIM) for h in heads]

    def l2_normalized(x, extra_scale):
        return x * (lax.rsqrt(jnp.sum(x * x, axis=-1, keepdims=True) + EPS) * extra_scale)

    def conv_silu(buf, r0, which):
        window = buf[pl.ds(r0, chunk + halo), :]
        acc = None
        for j in range(DN_CONV_K):
            w = cw_ref[pl.ds(j, 1), pl.ds(which * DN_WIDTH, DN_WIDTH)]
            back = DN_CONV_K - 1 - j
            shifted = window if back == 0 else pltpu.roll(window, back, axis=0)
            term = shifted[halo:, :] * w
            acc = term if acc is None else acc + term
        return _silu(acc)

    def chunk_body(c, carry):
        r0 = pl.multiple_of(c * chunk, chunk)
        qa = conv_silu(qbuf, r0, 0)
        ka = conv_silu(kbuf, r0, 1)
        va = conv_silu(vbuf, r0, 2)
        gts = gates_ref[pl.ds(r0, chunk), :]
        gcum = _mm_exact(lower_incl, gts)
        gcum_t = gcum.T
        q = [l2_normalized(qa[:, sl], scale) for sl in head_cols]
        k = [l2_normalized(ka[:, sl], 1.0) for sl in head_cols]
        v = [va[:, sl] for sl in head_cols]
        gc = [gcum[:, h:h + 1] for h in heads]
        gr = [gcum_t[h:h + 1, :] for h in heads]
        beta = [gts[:, DN_HEADS + h:DN_HEADS + h + 1] for h in heads]
        g_last = [g[chunk - 1:chunk, :] for g in gc]
        decay = _each(lambda a, b: jnp.exp(jnp.where(causal, a - b, -1e30)), gc, gr)
        kk = _each(_mm_nt, k, k)
        low = _each(lambda m, d, b: jnp.where(strict, m * d * b, 0.0), kk, decay, beta)
        inv = _unit_lower_inverse(low, row, col)
        eg = _each(jnp.exp, gc)
        rhs = _each(lambda vv, kx, b, e: jnp.concatenate([vv * b, kx * (b * e)], axis=1),
                    v, k, beta, eg)
        sol = _each(_mm, inv, rhs)
        qk = _each(lambda a, b, d: _mm_nt(a, b) * d, q, k, decay)
        q_dec = _each(lambda a, e: a * e, q, eg)
        k_dec_t = _each(lambda kx, gl, g: (kx * jnp.exp(gl - g)).T, k, g_last, gc)
        state = [state_ref[h] for h in heads]
        ws_qs = _each(lambda s, qd, st: _mm(jnp.concatenate([s[:, DN_HEAD_DIM:], qd], axis=0), st),
                      sol, q_dec, state)
        v_new = _each(lambda s, x: s[:, :DN_HEAD_DIM] - x[:chunk], sol, ws_qs)
        o = _each(lambda x, a, vn: x[chunk:] + _mm(a, vn), ws_qs, qk, v_new)
        new_state = _each(lambda st, gl, kt, vn: st * jnp.exp(gl) + _mm(kt, vn),
                          state, g_last, k_dec_t, v_new)
        for h in heads:
            state_ref[h] = new_state[h]
        for h, sl in enumerate(head_cols):
            z = z_ref[pl.ds(r0, chunk), sl]
            y = o[h] * _rms_scale(o[h]) * ng_ref[...] * _silu(z)
            out_ref[pl.ds(r0, chunk), sl] = y.astype(out_ref.dtype)
        return carry

    lax.fori_loop(0, ts // chunk, chunk_body, 0)


def _deltanet(proj, gates, conv_w, norm_g, *, seq, ts, chunk):
    m = proj.shape[0]
    tiles = seq // ts
    col0 = 3 * CONV_WIDTH // DN_WIDTH
    row_map = lambda b, t: b * tiles + t
    qkvz = [pl.BlockSpec((ts, DN_WIDTH), functools.partial(lambda b, t, n: (b * tiles + t, col0 + n), n=n))
            for n in range(4)]
    return pl.pallas_call(
        functools.partial(_deltanet_kernel, ts=ts, chunk=chunk),
        grid=(m // seq, tiles),
        in_specs=qkvz + [
            pl.BlockSpec((ts, LANES), lambda b, t: (row_map(b, t), 0)),
            pl.BlockSpec((DN_CONV_K, 3 * DN_WIDTH), lambda b, t: (0, 0)),
            pl.BlockSpec((1, DN_HEAD_DIM), lambda b, t: (0, 0)),
        ],
        out_specs=pl.BlockSpec((ts, DN_WIDTH), lambda b, t: (row_map(b, t), 0)),
        out_shape=jax.ShapeDtypeStruct((m, DN_WIDTH), BF16),
        scratch_shapes=[pltpu.VMEM((ts + SUBLANES_F32, DN_WIDTH), F32)] * 3
        + [pltpu.VMEM((DN_HEADS, DN_HEAD_DIM, DN_HEAD_DIM), F32)],
        compiler_params=_params(("parallel", "arbitrary")),
        name="deltanet",
    )(proj, proj, proj, proj, gates, conv_w, norm_g)


def _out_proj_kernel(ax_ref, ab_ref, ac_ref, hax_ref, hac_ref, yb_ref, x_ref, cw_ref, w_ref,
                     out_ref, pbuf, *, tm, seq):
    halo = SUBLANES_F32
    seq_start = (pl.program_id(0) * tm) % seq == 0
    pbuf[pl.ds(0, halo), :] = jnp.where(seq_start, 0.0, hac_ref[...] * hax_ref[...])
    pbuf[pl.ds(halo, tm), :] = ac_ref[...] * ax_ref[...]
    acc = None
    for j in range(CONV_K):
        term = pbuf[pl.ds(halo - (CONV_K - 1) + j, tm), :] * cw_ref[pl.ds(j, 1), :]
        acc = term if acc is None else acc + term
    ya = (ab_ref[...] * acc).astype(BF16)
    out_ref[...] = (x_ref[...]
                    + jnp.dot(ya, w_ref[pl.ds(0, CONV_WIDTH), :], preferred_element_type=F32)
                    + jnp.dot(yb_ref[...], w_ref[pl.ds(CONV_WIDTH, DN_WIDTH), :],
                              preferred_element_type=F32))


def _out_proj(proj, yb, x2, conv_w, w_out, *, tm, seq):
    m = x2.shape[0]
    halo = SUBLANES_F32
    halo_row = lambda i: jnp.maximum(i * (tm // halo) - 1, 0)
    return pl.pallas_call(
        functools.partial(_out_proj_kernel, tm=tm, seq=seq),
        grid=(m // tm,),
        in_specs=[
            pl.BlockSpec((tm, CONV_WIDTH), lambda i: (i, 0)),
            pl.BlockSpec((tm, CONV_WIDTH), lambda i: (i, 1)),
            pl.BlockSpec((tm, CONV_WIDTH), lambda i: (i, 2)),
            pl.BlockSpec((halo, CONV_WIDTH), lambda i: (halo_row(i), 0)),
            pl.BlockSpec((halo, CONV_WIDTH), lambda i: (halo_row(i), 2)),
            pl.BlockSpec((tm, DN_WIDTH), lambda i: (i, 0)),
            pl.BlockSpec((tm, D_MODEL), lambda i: (i, 0)),
            pl.BlockSpec((CONV_K, CONV_WIDTH), lambda i: (0, 0)),
            pl.BlockSpec((D_MODEL, D_MODEL), lambda i: (0, 0)),
        ],
        out_specs=pl.BlockSpec((tm, D_MODEL), lambda i: (i, 0)),
        out_shape=jax.ShapeDtypeStruct((m, D_MODEL), F32),
        scratch_shapes=[pltpu.VMEM((tm + halo, CONV_WIDTH), F32)],
        compiler_params=_params(("parallel",)),
        name="out_proj",
    )(proj, proj, proj, proj, proj, yb, x2, conv_w, w_out)


def _ffn_kernel(x_ref, hx_ref, g_ref, wg_ref, wv_ref, cg_ref, cv_ref, wd_ref, out_ref,
                hs_ref, ubuf, *, tm, seq):
    halo = SUBLANES_BF16
    j = pl.program_id(1)

    @pl.when(j == 0)
    def _():
        x = x_ref[...]
        hs_ref[pl.ds(0, tm), :] = (x * _rms_scale(x) * g_ref[...]).astype(BF16)
        hx = hx_ref[...]
        seq_start = (pl.program_id(0) * tm) % seq == 0
        hh = jnp.where(seq_start, 0.0, hx * _rms_scale(hx) * g_ref[...])
        hs_ref[pl.ds(tm, halo), :] = hh.astype(BF16)
        out_ref[...] = x

    def up_conv(w_ref, cw_ref):
        up = jnp.dot(hs_ref[...], w_ref[...], preferred_element_type=F32)
        ubuf[pl.ds(0, halo), :] = up[tm:, :]
        ubuf[pl.ds(halo, tm), :] = up[:tm, :]
        acc = None
        for t in range(FFN_CONV_K):
            term = ubuf[pl.ds(halo - (FFN_CONV_K - 1) + t, tm), :] * cw_ref[pl.ds(t, 1), :]
            acc = term if acc is None else acc + term
        return acc

    gate = up_conv(wg_ref, cg_ref)
    act = _silu(gate)
    val = up_conv(wv_ref, cv_ref)
    act = (act * val).astype(BF16)
    out_ref[...] += jnp.dot(act, wd_ref[...], preferred_element_type=F32)


def _ffn(x1, g, w_up, conv_w, w_down, *, tm, bn, seq):
    m = x1.shape[0]
    halo = SUBLANES_BF16
    nj = D_FF // bn
    return pl.pallas_call(
        functools.partial(_ffn_kernel, tm=tm, seq=seq),
        grid=(m // tm, nj),
        in_specs=[
            pl.BlockSpec((tm, D_MODEL), lambda i, j: (i, 0)),
            pl.BlockSpec((halo, D_MODEL), lambda i, j: (jnp.maximum(i * (tm // halo) - 1, 0), 0)),
            pl.BlockSpec((1, D_MODEL), lambda i, j: (0, 0)),
            pl.BlockSpec((D_MODEL, bn), lambda i, j: (0, j)),
            pl.BlockSpec((D_MODEL, bn), lambda i, j: (0, j + nj)),
            pl.BlockSpec((FFN_CONV_K, bn), lambda i, j: (0, j)),
            pl.BlockSpec((FFN_CONV_K, bn), lambda i, j: (0, j + nj)),
            pl.BlockSpec((bn, D_MODEL), lambda i, j: (j, 0)),
        ],
        out_specs=pl.BlockSpec((tm, D_MODEL), lambda i, j: (i, 0)),
        out_shape=jax.ShapeDtypeStruct((m, D_MODEL), F32),
        scratch_shapes=[pltpu.VMEM((tm + halo, D_MODEL), BF16),
                        pltpu.VMEM((tm + halo, bn), F32)],
        compiler_params=_params(("parallel", "arbitrary")),
        name="conv_ffn",
    )(x1, x1, g, w_up, w_up, conv_w, conv_w, w_down)


def _ple_kernel(x_ref, p_ref, g_ref, wpg_ref, wpp_ref, fg_ref, out_ref):
    x = x_ref[...]
    hb = (x * _rms_scale(x) * g_ref[...]).astype(BF16)
    gate = jax.nn.sigmoid(jnp.dot(hb, wpg_ref[...], preferred_element_type=F32))
    emb = jnp.dot(p_ref[...].astype(BF16), wpp_ref[...], preferred_element_type=F32)
    y = x + gate * emb
    out_ref[...] = y * _rms_scale(y) * fg_ref[...]


def _ple(x2, p2, g, w_pg, w_pp, final_g, *, tm):
    m = x2.shape[0]
    return pl.pallas_call(
        _ple_kernel,
        grid=(m // tm,),
        in_specs=[
            pl.BlockSpec((tm, D_MODEL), lambda i: (i, 0)),
            pl.BlockSpec((tm, PLE_DIM), lambda i: (i, 0)),
            pl.BlockSpec((1, D_MODEL), lambda i: (0, 0)),
            pl.BlockSpec((D_MODEL, D_MODEL), lambda i: (0, 0)),
            pl.BlockSpec((PLE_DIM, D_MODEL), lambda i: (0, 0)),
            pl.BlockSpec((1, D_MODEL), lambda i: (0, 0)),
        ],
        out_specs=pl.BlockSpec((tm, D_MODEL), lambda i: (i, 0)),
        out_shape=jax.ShapeDtypeStruct((m, D_MODEL), F32),
        compiler_params=_params(("parallel",)),
        name="ple",
    )(x2, p2, g, w_pg, w_pp, final_g)


def _layer(x2, p2, seq, norm_mix_g, w_in, conv_a_w, conv_qkv_w, a_log, dt_bias, dn_norm_g,
           w_out, norm_ffn_g, w_up, conv_ffn_w, w_down, norm_ple_g, w_ple_gate, w_ple_proj,
           out_norm_g):
    row = lambda v: v.reshape(1, -1).astype(F32)
    lane_pad = lambda v: jnp.pad(row(v), ((0, 0), (0, LANES - v.shape[-1])))
    w_main = w_in[:, :PROJ_COLS].astype(BF16)
    w_gate = jnp.pad(w_in[:, PROJ_COLS:], ((0, 0), (0, LANES - 2 * DN_HEADS))).astype(BF16)

    proj, gates = _in_proj(x2, row(norm_mix_g), w_main, w_gate, lane_pad(a_log), lane_pad(dt_bias),
                           tm=1024, tn=512)
    yb = _deltanet(proj, gates, conv_qkv_w.astype(F32), row(dn_norm_g), seq=seq, ts=512,
                   chunk=DN_CHUNK)
    x2 = _out_proj(proj, yb, x2, conv_a_w.astype(F32), w_out.astype(BF16), tm=512, seq=seq)
    x2 = _ffn(x2, row(norm_ffn_g), w_up.astype(BF16), conv_ffn_w.astype(F32), w_down.astype(BF16),
              tm=512, bn=512, seq=seq)
    return _ple(x2, p2, row(norm_ple_g), w_ple_gate.astype(BF16), w_ple_proj.astype(BF16),
                row(out_norm_g), tm=512)


def kernel(x, p, norm_mix_g, w_in, conv_a_w, conv_qkv_w, a_log, dt_bias, dn_norm_g, w_out,
           norm_ffn_g, w_up, conv_ffn_w, w_down, norm_ple_g, w_ple_gate, w_ple_proj, final_norm_g):
    batch, seq, d_model = x.shape
    depth = p.shape[0]
    assert depth == 1 and d_model == D_MODEL
    x2 = x.reshape(batch * seq, d_model)
    p2 = p[0].reshape(batch * seq, PLE_DIM)
    out = _layer(x2, p2, seq, norm_mix_g[0], w_in[0], conv_a_w[0], conv_qkv_w[0], a_log[0],
                 dt_bias[0], dn_norm_g[0], w_out[0], norm_ffn_g[0], w_up[0], conv_ffn_w[0],
                 w_down[0], norm_ple_g[0], w_ple_gate[0], w_ple_proj[0], final_norm_g)
    return out.reshape(batch, seq, d_model)
```

```python
import functools

import jax
import jax.numpy as jnp
from jax import lax
from jax.experimental import pallas as pl
from jax.experimental.pallas import tpu as pltpu

D_MODEL = 2048
CONV_WIDTH = 1024
CONV_K = 3
DN_HEADS = 8
DN_HEAD_DIM = 128
DN_WIDTH = DN_HEADS * DN_HEAD_DIM
DN_CONV_K = 4
D_FF = 5632
FFN_CONV_K = 3
PLE_DIM = 256
EPS = 1e-6
PROJ_COLS = 3 * CONV_WIDTH + 4 * DN_WIDTH

LANES = 128
SUBLANES_F32 = 8
SUBLANES_BF16 = 16
VMEM_LIMIT_BYTES = 60 * 1024 * 1024

DN_CHUNK = 128
INV_BASE = 8

F32 = jnp.float32
BF16 = jnp.bfloat16


def _mm(a, b):
    return jnp.dot(a.astype(BF16), b.astype(BF16), preferred_element_type=F32)


def _mm_nt(a, b):
    return lax.dot_general(a.astype(BF16), b.astype(BF16), (((1,), (1,)), ((), ())),
                           preferred_element_type=F32)


def _mm_exact(a, b):
    return jnp.dot(a, b, precision=lax.Precision.HIGHEST, preferred_element_type=F32)


def _rms_scale(x):
    return lax.rsqrt(jnp.mean(x * x, axis=-1, keepdims=True) + EPS)


def _silu(x):
    return x * jax.nn.sigmoid(x)


def _params(semantics):
    return pltpu.CompilerParams(dimension_semantics=semantics, vmem_limit_bytes=VMEM_LIMIT_BYTES)


def _in_proj_kernel(x_ref, g_ref, w_ref, wg_ref, alog_ref, dtb_ref, proj_ref, gates_ref, hs_ref):
    @pl.when(pl.program_id(1) == 0)
    def _():
        x = x_ref[...]
        hb = (x * _rms_scale(x) * g_ref[...]).astype(BF16)
        hs_ref[...] = hb
        raw = jnp.dot(hb, wg_ref[...], preferred_element_type=F32)
        a = raw + dtb_ref[...]
        softplus = jnp.maximum(a, 0.0) + jnp.log(1.0 + jnp.exp(-jnp.abs(a)))
        decay = -jnp.exp(alog_ref[...]) * softplus
        lane = lax.broadcasted_iota(jnp.int32, raw.shape, 1)
        gates_ref[...] = jnp.where(lane < DN_HEADS, decay, jax.nn.sigmoid(raw))

    proj_ref[...] = jnp.dot(hs_ref[...], w_ref[...], preferred_element_type=F32)


def _in_proj(x2, g, w_main, w_gate, alog_pad, dtb_pad, *, tm, tn):
    m = x2.shape[0]
    return pl.pallas_call(
        _in_proj_kernel,
        grid=(m // tm, PROJ_COLS // tn),
        in_specs=[
            pl.BlockSpec((tm, D_MODEL), lambda i, j: (i, 0)),
            pl.BlockSpec((1, D_MODEL), lambda i, j: (0, 0)),
            pl.BlockSpec((D_MODEL, tn), lambda i, j: (0, j)),
            pl.BlockSpec((D_MODEL, LANES), lambda i, j: (0, 0)),
            pl.BlockSpec((1, LANES), lambda i, j: (0, 0)),
            pl.BlockSpec((1, LANES), lambda i, j: (0, 0)),
        ],
        out_specs=[
            pl.BlockSpec((tm, tn), lambda i, j: (i, j)),
            pl.BlockSpec((tm, LANES), lambda i, j: (i, 0)),
        ],
        out_shape=[
            jax.ShapeDtypeStruct((m, PROJ_COLS), F32),
            jax.ShapeDtypeStruct((m, LANES), F32),
        ],
        scratch_shapes=[pltpu.VMEM((tm, D_MODEL), BF16)],
        compiler_params=_params(("parallel", "arbitrary")),
        name="in_proj",
    )(x2, g, w_main, w_gate, alog_pad, dtb_pad)


def _each(fn, *lists):
    return [fn(*args) for args in zip(*lists)]


def _unit_lower_inverse(lows, row, col):
    c = lows[0].shape[0]
    eye = (row == col).astype(F32)
    base_shift = INV_BASE.bit_length() - 1
    base_mask = (row >> base_shift) == (col >> base_shift)
    l0 = [jnp.where(base_mask, low, 0.0) for low in lows]
    l2 = _each(_mm, l0, l0)
    l4 = _each(_mm, l2, l2)
    l3 = _each(_mm, l0, l2)
    p1 = _each(lambda a, b, d: eye - a + b - d, l0, l2, l3)
    inv = _each(lambda p, pl4: p + pl4, p1, _each(_mm, p1, l4))
    shift = base_shift
    while (1 << shift) < c:
        off_mask = (((row >> (shift + 1)) == (col >> (shift + 1)))
                    & ((row >> shift) != (col >> shift)))
        off_inv = _each(lambda low, t: _mm(jnp.where(off_mask, low, 0.0), t), lows, inv)
        inv = _each(lambda t, x: t - _mm(t, x), inv, off_inv)
        shift += 1
    return inv


def _deltanet_kernel(q_ref, k_ref, v_ref, z_ref, gates_ref, cw_ref, ng_ref, out_ref,
                     qbuf, kbuf, vbuf, state_ref, *, ts, chunk):
    halo = SUBLANES_F32
    first = pl.program_id(1) == 0

    @pl.when(first)
    def _():
        state_ref[...] = jnp.zeros_like(state_ref)
        for buf in (qbuf, kbuf, vbuf):
            buf[pl.ds(0, halo), :] = jnp.zeros((halo, DN_WIDTH), F32)

    @pl.when(jnp.logical_not(first))
    def _():
        for buf in (qbuf, kbuf, vbuf):
            buf[pl.ds(0, halo), :] = buf[pl.ds(ts, halo), :]

    for src, buf in ((q_ref, qbuf), (k_ref, kbuf), (v_ref, vbuf)):
        buf[pl.ds(halo, ts), :] = src[...]

    row = lax.broadcasted_iota(jnp.int32, (chunk, chunk), 0)
    col = lax.broadcasted_iota(jnp.int32, (chunk, chunk), 1)
    lower_incl = (row >= col).astype(F32)
    causal = row >= col
    strict = row > col
    scale = DN_HEAD_DIM ** -0.5
    heads = range(DN_HEADS)
    head_cols = [slice(h * DN_HEAD_DIM, (h + 1) * DN_HEAD_DIM) for h in heads]

    def l2_normalized(x, extra_scale):
        return x * (lax.rsqrt(jnp.sum(x * x, axis=-1, keepdims=True) + EPS) * extra_scale)

    def conv_silu(buf, r0, which):
        window = buf[pl.ds(r0, chunk + halo), :]
        acc = None
        for j in range(DN_CONV_K):
            w = cw_ref[pl.ds(j, 1), pl.ds(which * DN_WIDTH, DN_WIDTH)]
            back = DN_CONV_K - 1 - j
            shifted = window if back == 0 else pltpu.roll(window, back, axis=0)
            term = shifted[halo:, :] * w
            acc = term if acc is None else acc + term
        return _silu(acc)

    def chunk_body(c, carry):
        r0 = pl.multiple_of(c * chunk, chunk)
        qa = conv_silu(qbuf, r0, 0)
        ka = conv_silu(kbuf, r0, 1)
        va = conv_silu(vbuf, r0, 2)
        gts = gates_ref[pl.ds(r0, chunk), :]
        gcum = _mm_exact(lower_incl, gts)
        gcum_t = gcum.T
        q = [l2_normalized(qa[:, sl], scale) for sl in head_cols]
        k = [l2_normalized(ka[:, sl], 1.0) for sl in head_cols]
        v = [va[:, sl] for sl in head_cols]
        gc = [gcum[:, h:h + 1] for h in heads]
        gr = [gcum_t[h:h + 1, :] for h in heads]
        beta = [gts[:, DN_HEADS + h:DN_HEADS + h + 1] for h in heads]
        g_last = [g[chunk - 1:chunk, :] for g in gc]
        decay = _each(lambda a, b: jnp.exp(jnp.where(causal, a - b, -1e30)), gc, gr)
        kk = _each(_mm_nt, k, k)
        low = _each(lambda m, d, b: jnp.where(strict, m * d * b, 0.0), kk, decay, beta)
        inv = _unit_lower_inverse(low, row, col)
        eg = _each(jnp.exp, gc)
        rhs = _each(lambda vv, kx, b, e: jnp.concatenate([vv * b, kx * (b * e)], axis=1),
                    v, k, beta, eg)
        sol = _each(_mm, inv, rhs)
        qk = _each(lambda a, b, d: _mm_nt(a, b) * d, q, k, decay)
        q_dec = _each(lambda a, e: a * e, q, eg)
        k_dec_t = _each(lambda kx, gl, g: (kx * jnp.exp(gl - g)).T, k, g_last, gc)
        state = [state_ref[h] for h in heads]
        ws_qs = _each(lambda s, qd, st: _mm(jnp.concatenate([s[:, DN_HEAD_DIM:], qd], axis=0), st),
                      sol, q_dec, state)
        v_new = _each(lambda s, x: s[:, :DN_HEAD_DIM] - x[:chunk], sol, ws_qs)
        o = _each(lambda x, a, vn: x[chunk:] + _mm(a, vn), ws_qs, qk, v_new)
        new_state = _each(lambda st, gl, kt, vn: st * jnp.exp(gl) + _mm(kt, vn),
                          state, g_last, k_dec_t, v_new)
        for h in heads:
            state_ref[h] = new_state[h]
        for h, sl in enumerate(head_cols):
            z = z_ref[pl.ds(r0, chunk), sl]
            y = o[h] * _rms_scale(o[h]) * ng_ref[...] * _silu(z)
            out_ref[pl.ds(r0, chunk), sl] = y.astype(out_ref.dtype)
        return carry

    lax.fori_loop(0, ts // chunk, chunk_body, 0)


def _deltanet(proj, gates, conv_w, norm_g, *, seq, ts, chunk):
    m = proj.shape[0]
    tiles = seq // ts
    col0 = 3 * CONV_WIDTH // DN_WIDTH
    row_map = lambda b, t: b * tiles + t
    qkvz = [pl.BlockSpec((ts, DN_WIDTH), functools.partial(lambda b, t, n: (b * tiles + t, col0 + n), n=n))
            for n in range(4)]
    return pl.pallas_call(
        functools.partial(_deltanet_kernel, ts=ts, chunk=chunk),
        grid=(m // seq, tiles),
        in_specs=qkvz + [
            pl.BlockSpec((ts, LANES), lambda b, t: (row_map(b, t), 0)),
            pl.BlockSpec((DN_CONV_K, 3 * DN_WIDTH), lambda b, t: (0, 0)),
            pl.BlockSpec((1, DN_HEAD_DIM), lambda b, t: (0, 0)),
        ],
        out_specs=pl.BlockSpec((ts, DN_WIDTH), lambda b, t: (row_map(b, t), 0)),
        out_shape=jax.ShapeDtypeStruct((m, DN_WIDTH), BF16),
        scratch_shapes=[pltpu.VMEM((ts + SUBLANES_F32, DN_WIDTH), F32)] * 3
        + [pltpu.VMEM((DN_HEADS, DN_HEAD_DIM, DN_HEAD_DIM), F32)],
        compiler_params=_params(("parallel", "arbitrary")),
        name="deltanet",
    )(proj, proj, proj, proj, gates, conv_w, norm_g)


def _out_proj_kernel(ax_ref, ab_ref, ac_ref, hax_ref, hac_ref, yb_ref, x_ref, cw_ref, w_ref,
                     out_ref, pbuf, *, tm, seq):
    halo = SUBLANES_F32
    acc_b = jnp.dot(yb_ref[...], w_ref[pl.ds(CONV_WIDTH, DN_WIDTH), :], preferred_element_type=F32)
    seq_start = (pl.program_id(0) * tm) % seq == 0
    pbuf[pl.ds(0, halo), :] = jnp.where(seq_start, 0.0, hac_ref[...] * hax_ref[...])
    pbuf[pl.ds(halo, tm), :] = ac_ref[...] * ax_ref[...]
    acc = None
    for j in range(CONV_K):
        term = pbuf[pl.ds(halo - (CONV_K - 1) + j, tm), :] * cw_ref[pl.ds(j, 1), :]
        acc = term if acc is None else acc + term
    ya = (ab_ref[...] * acc).astype(BF16)
    out_ref[...] = (x_ref[...] + acc_b
                    + jnp.dot(ya, w_ref[pl.ds(0, CONV_WIDTH), :], preferred_element_type=F32))


def _out_proj(proj, yb, x2, conv_w, w_out, *, tm, seq):
    m = x2.shape[0]
    halo = SUBLANES_F32
    halo_row = lambda i: jnp.maximum(i * (tm // halo) - 1, 0)
    return pl.pallas_call(
        functools.partial(_out_proj_kernel, tm=tm, seq=seq),
        grid=(m // tm,),
        in_specs=[
            pl.BlockSpec((tm, CONV_WIDTH), lambda i: (i, 0)),
            pl.BlockSpec((tm, CONV_WIDTH), lambda i: (i, 1)),
            pl.BlockSpec((tm, CONV_WIDTH), lambda i: (i, 2)),
            pl.BlockSpec((halo, CONV_WIDTH), lambda i: (halo_row(i), 0)),
            pl.BlockSpec((halo, CONV_WIDTH), lambda i: (halo_row(i), 2)),
            pl.BlockSpec((tm, DN_WIDTH), lambda i: (i, 0)),
            pl.BlockSpec((tm, D_MODEL), lambda i: (i, 0)),
            pl.BlockSpec((CONV_K, CONV_WIDTH), lambda i: (0, 0)),
            pl.BlockSpec((D_MODEL, D_MODEL), lambda i: (0, 0)),
        ],
        out_specs=pl.BlockSpec((tm, D_MODEL), lambda i: (i, 0)),
        out_shape=jax.ShapeDtypeStruct((m, D_MODEL), F32),
        scratch_shapes=[pltpu.VMEM((tm + halo, CONV_WIDTH), F32)],
        compiler_params=_params(("parallel",)),
        name="out_proj",
    )(proj, proj, proj, proj, proj, yb, x2, conv_w, w_out)


def _ffn_kernel(x_ref, hx_ref, g_ref, wg_ref, wv_ref, cg_ref, cv_ref, wd_ref, out_ref,
                hs_ref, ubuf, *, tm, seq):
    halo = SUBLANES_BF16
    j = pl.program_id(1)

    @pl.when(j == 0)
    def _():
        x = x_ref[...]
        hs_ref[pl.ds(0, tm), :] = (x * _rms_scale(x) * g_ref[...]).astype(BF16)
        hx = hx_ref[...]
        seq_start = (pl.program_id(0) * tm) % seq == 0
        hh = jnp.where(seq_start, 0.0, hx * _rms_scale(hx) * g_ref[...])
        hs_ref[pl.ds(tm, halo), :] = hh.astype(BF16)
        out_ref[...] = x

    def up_conv(w_ref, cw_ref):
        up = jnp.dot(hs_ref[...], w_ref[...], preferred_element_type=F32)
        ubuf[pl.ds(0, halo), :] = up[tm:, :]
        ubuf[pl.ds(halo, tm), :] = up[:tm, :]
        acc = None
        for t in range(FFN_CONV_K):
            term = ubuf[pl.ds(halo - (FFN_CONV_K - 1) + t, tm), :] * cw_ref[pl.ds(t, 1), :]
            acc = term if acc is None else acc + term
        return acc

    gate = up_conv(wg_ref, cg_ref)
    act = _silu(gate)
    val = up_conv(wv_ref, cv_ref)
    act = (act * val).astype(BF16)
    out_ref[...] += jnp.dot(act, wd_ref[...], preferred_element_type=F32)


def _ffn(x1, g, w_up, conv_w, w_down, *, tm, bn, seq):
    m = x1.shape[0]
    halo = SUBLANES_BF16
    nj = D_FF // bn
    return pl.pallas_call(
        functools.partial(_ffn_kernel, tm=tm, seq=seq),
        grid=(m // tm, nj),
        in_specs=[
            pl.BlockSpec((tm, D_MODEL), lambda i, j: (i, 0)),
            pl.BlockSpec((halo, D_MODEL), lambda i, j: (jnp.maximum(i * (tm // halo) - 1, 0), 0)),
            pl.BlockSpec((1, D_MODEL), lambda i, j: (0, 0)),
            pl.BlockSpec((D_MODEL, bn), lambda i, j: (0, j)),
            pl.BlockSpec((D_MODEL, bn), lambda i, j: (0, j + nj)),
            pl.BlockSpec((FFN_CONV_K, bn), lambda i, j: (0, j)),
            pl.BlockSpec((FFN_CONV_K, bn), lambda i, j: (0, j + nj)),
            pl.BlockSpec((bn, D_MODEL), lambda i, j: (j, 0)),
        ],
        out_specs=pl.BlockSpec((tm, D_MODEL), lambda i, j: (i, 0)),
        out_shape=jax.ShapeDtypeStruct((m, D_MODEL), F32),
        scratch_shapes=[pltpu.VMEM((tm + halo, D_MODEL), BF16),
                        pltpu.VMEM((tm + halo, bn), F32)],
        compiler_params=_params(("parallel", "arbitrary")),
        name="conv_ffn",
    )(x1, x1, g, w_up, w_up, conv_w, conv_w, w_down)


def _ple_kernel(x_ref, p_ref, g_ref, wpg_ref, wpp_ref, fg_ref, out_ref):
    x = x_ref[...]
    hb = (x * _rms_scale(x) * g_ref[...]).astype(BF16)
    gate = jax.nn.sigmoid(jnp.dot(hb, wpg_ref[...], preferred_element_type=F32))
    emb = jnp.dot(p_ref[...].astype(BF16), wpp_ref[...], preferred_element_type=F32)
    y = x + gate * emb
    out_ref[...] = y * _rms_scale(y) * fg_ref[...]


def _ple(x2, p2, g, w_pg, w_pp, final_g, *, tm):
    m = x2.shape[0]
    return pl.pallas_call(
        _ple_kernel,
        grid=(m // tm,),
        in_specs=[
            pl.BlockSpec((tm, D_MODEL), lambda i: (i, 0)),
            pl.BlockSpec((tm, PLE_DIM), lambda i: (i, 0)),
            pl.BlockSpec((1, D_MODEL), lambda i: (0, 0)),
            pl.BlockSpec((D_MODEL, D_MODEL), lambda i: (0, 0)),
            pl.BlockSpec((PLE_DIM, D_MODEL), lambda i: (0, 0)),
            pl.BlockSpec((1, D_MODEL), lambda i: (0, 0)),
        ],
        out_specs=pl.BlockSpec((tm, D_MODEL), lambda i: (i, 0)),
        out_shape=jax.ShapeDtypeStruct((m, D_MODEL), F32),
        compiler_params=_params(("parallel",)),
        name="ple",
    )(x2, p2, g, w_pg, w_pp, final_g)


def _layer(x2, p2, seq, norm_mix_g, w_in, conv_a_w, conv_qkv_w, a_log, dt_bias, dn_norm_g,
           w_out, norm_ffn_g, w_up, conv_ffn_w, w_down, norm_ple_g, w_ple_gate, w_ple_proj,
           out_norm_g):
    row = lambda v: v.reshape(1, -1).astype(F32)
    lane_pad = lambda v: jnp.pad(row(v), ((0, 0), (0, LANES - v.shape[-1])))
    w_main = w_in.astype(BF16)
    w_gate = jnp.pad(w_in[:, PROJ_COLS:], ((0, 0), (0, LANES - 2 * DN_HEADS))).astype(BF16)

    proj, gates = _in_proj(x2, row(norm_mix_g), w_main, w_gate, lane_pad(a_log), lane_pad(dt_bias),
                           tm=1024, tn=1024)
    yb = _deltanet(proj, gates, conv_qkv_w.astype(F32), row(dn_norm_g), seq=seq, ts=512,
                   chunk=DN_CHUNK)
    x2 = _out_proj(proj, yb, x2, conv_a_w.astype(F32), w_out.astype(BF16), tm=512, seq=seq)
    x2 = _ffn(x2, row(norm_ffn_g), w_up.astype(BF16), conv_ffn_w.astype(F32), w_down.astype(BF16),
              tm=1024, bn=512, seq=seq)
    return _ple(x2, p2, row(norm_ple_g), w_ple_gate.astype(BF16), w_ple_proj.astype(BF16),
                row(out_norm_g), tm=512)


def kernel(x, p, norm_mix_g, w_in, conv_a_w, conv_qkv_w, a_log, dt_bias, dn_norm_g, w_out,
           norm_ffn_g, w_up, conv_ffn_w, w_down, norm_ple_g, w_ple_gate, w_ple_proj, final_norm_g):
    batch, seq, d_model = x.shape
    depth = p.shape[0]
    assert depth == 1 and d_model == D_MODEL
    x2 = x.reshape(batch * seq, d_model)
    p2 = p[0].reshape(batch * seq, PLE_DIM)
    out = _layer(x2, p2, seq, norm_mix_g[0], w_in[0], conv_a_w[0], conv_qkv_w[0], a_log[0],
                 dt_bias[0], dn_norm_g[0], w_out[0], norm_ffn_g[0], w_up[0], conv_ffn_w[0],
                 w_down[0], norm_ple_g[0], w_ple_gate[0], w_ple_proj[0], final_norm_g)
    return out.reshape(batch, seq, d_model)
```

```python
import functools

import jax
import jax.numpy as jnp
from jax import lax
from jax.experimental import pallas as pl
from jax.experimental.pallas import tpu as pltpu

D_MODEL = 2048
CONV_WIDTH = 1024
CONV_K = 3
DN_HEADS = 8
DN_HEAD_DIM = 128
DN_WIDTH = DN_HEADS * DN_HEAD_DIM
DN_CONV_K = 4
D_FF = 5632
FFN_CONV_K = 3
PLE_DIM = 256
EPS = 1e-6
PROJ_COLS = 3 * CONV_WIDTH + 4 * DN_WIDTH

LANES = 128
SUBLANES_F32 = 8
SUBLANES_BF16 = 16
VMEM_LIMIT_BYTES = 60 * 1024 * 1024

DN_CHUNK = 128
INV_BASE = 8

F32 = jnp.float32
BF16 = jnp.bfloat16


def _mm(a, b):
    return jnp.dot(a.astype(BF16), b.astype(BF16), preferred_element_type=F32)


def _mm_nt(a, b):
    return lax.dot_general(a.astype(BF16), b.astype(BF16), (((1,), (1,)), ((), ())),
                           preferred_element_type=F32)


def _mm_exact(a, b):
    return jnp.dot(a, b, precision=lax.Precision.HIGHEST, preferred_element_type=F32)


def _rms_scale(x):
    return lax.rsqrt(jnp.mean(x * x, axis=-1, keepdims=True) + EPS)


def _silu(x):
    return x * jax.nn.sigmoid(x)


def _params(semantics):
    return pltpu.CompilerParams(dimension_semantics=semantics, vmem_limit_bytes=VMEM_LIMIT_BYTES)


def _in_proj_kernel(x_ref, g_ref, w_ref, wg_ref, alog_ref, dtb_ref, proj_ref, gates_ref, hs_ref):
    @pl.when(pl.program_id(1) == 0)
    def _():
        x = x_ref[...]
        hb = (x * _rms_scale(x) * g_ref[...]).astype(BF16)
        hs_ref[...] = hb
        raw = jnp.dot(hb, wg_ref[...], preferred_element_type=F32)
        a = raw + dtb_ref[...]
        softplus = jnp.maximum(a, 0.0) + jnp.log(1.0 + jnp.exp(-jnp.abs(a)))
        decay = -jnp.exp(alog_ref[...]) * softplus
        lane = lax.broadcasted_iota(jnp.int32, raw.shape, 1)
        gates_ref[...] = jnp.where(lane < DN_HEADS, decay, jax.nn.sigmoid(raw))

    proj_ref[...] = jnp.dot(hs_ref[...], w_ref[...], preferred_element_type=F32)


def _in_proj(x2, g, w_main, w_gate, alog_pad, dtb_pad, *, tm, tn):
    m = x2.shape[0]
    return pl.pallas_call(
        _in_proj_kernel,
        grid=(m // tm, PROJ_COLS // tn),
        in_specs=[
            pl.BlockSpec((tm, D_MODEL), lambda i, j: (i, 0)),
            pl.BlockSpec((1, D_MODEL), lambda i, j: (0, 0)),
            pl.BlockSpec((D_MODEL, tn), lambda i, j: (0, j)),
            pl.BlockSpec((D_MODEL, LANES), lambda i, j: (0, 0)),
            pl.BlockSpec((1, LANES), lambda i, j: (0, 0)),
            pl.BlockSpec((1, LANES), lambda i, j: (0, 0)),
        ],
        out_specs=[
            pl.BlockSpec((tm, tn), lambda i, j: (i, j)),
            pl.BlockSpec((tm, LANES), lambda i, j: (i, 0)),
        ],
        out_shape=[
            jax.ShapeDtypeStruct((m, PROJ_COLS), F32),
            jax.ShapeDtypeStruct((m, LANES), F32),
        ],
        scratch_shapes=[pltpu.VMEM((tm, D_MODEL), BF16)],
        compiler_params=_params(("parallel", "arbitrary")),
        name="in_proj",
    )(x2, g, w_main, w_gate, alog_pad, dtb_pad)


def _each(fn, *lists):
    return [fn(*args) for args in zip(*lists)]


def _unit_lower_inverse(lows, row, col):
    c = lows[0].shape[0]
    eye = (row == col).astype(F32)
    base_shift = INV_BASE.bit_length() - 1
    base_mask = (row >> base_shift) == (col >> base_shift)
    l0 = [jnp.where(base_mask, low, 0.0) for low in lows]
    l2 = _each(_mm, l0, l0)
    l4 = _each(_mm, l2, l2)
    l3 = _each(_mm, l0, l2)
    p1 = _each(lambda a, b, d: eye - a + b - d, l0, l2, l3)
    inv = _each(lambda p, pl4: p + pl4, p1, _each(_mm, p1, l4))
    shift = base_shift
    while (1 << shift) < c:
        off_mask = (((row >> (shift + 1)) == (col >> (shift + 1)))
                    & ((row >> shift) != (col >> shift)))
        off_inv = _each(lambda low, t: _mm(jnp.where(off_mask, low, 0.0), t), lows, inv)
        inv = _each(lambda t, x: t - _mm(t, x), inv, off_inv)
        shift += 1
    return inv


def _deltanet_kernel(q_ref, k_ref, v_ref, z_ref, gates_ref, cw_ref, ng_ref, out_ref,
                     qbuf, kbuf, vbuf, state_ref, u_s, wq_s, qk_s, kdt_s, *, ts, chunk):
    halo = SUBLANES_F32
    first = pl.program_id(1) == 0

    @pl.when(first)
    def _():
        state_ref[...] = jnp.zeros_like(state_ref)
        for buf in (qbuf, kbuf, vbuf):
            buf[pl.ds(0, halo), :] = jnp.zeros((halo, DN_WIDTH), F32)

    @pl.when(jnp.logical_not(first))
    def _():
        for buf in (qbuf, kbuf, vbuf):
            buf[pl.ds(0, halo), :] = buf[pl.ds(ts, halo), :]

    for src, buf in ((q_ref, qbuf), (k_ref, kbuf), (v_ref, vbuf)):
        buf[pl.ds(halo, ts), :] = src[...]

    row = lax.broadcasted_iota(jnp.int32, (chunk, chunk), 0)
    col = lax.broadcasted_iota(jnp.int32, (chunk, chunk), 1)
    lower_incl = (row >= col).astype(F32)
    causal = row >= col
    strict = row > col
    scale = DN_HEAD_DIM ** -0.5
    heads = range(DN_HEADS)
    chunks = range(ts // chunk)
    head_cols = [slice(h * DN_HEAD_DIM, (h + 1) * DN_HEAD_DIM) for h in heads]
    pairs = [(c, h) for c in chunks for h in heads]
    slot = lambda c, h: c * DN_HEADS + h

    def l2_normalized(x, extra_scale):
        return x * (lax.rsqrt(jnp.sum(x * x, axis=-1, keepdims=True) + EPS) * extra_scale)

    def conv_silu(buf, c, which):
        w = [cw_ref[pl.ds(j, 1), pl.ds(which * DN_WIDTH, DN_WIDTH)] for j in range(DN_CONV_K)]
        cur = buf[pl.ds(c * chunk, chunk + halo), :]
        prev = pltpu.roll(cur, 1, axis=0)
        tail = pltpu.roll(w[1] * cur + w[0] * prev, 2, axis=0)
        return _silu((w[3] * cur + w[2] * prev + tail)[halo:, :])

    qa = [conv_silu(qbuf, c, 0) for c in chunks]
    ka = [conv_silu(kbuf, c, 1) for c in chunks]
    va = [conv_silu(vbuf, c, 2) for c in chunks]
    gts = [gates_ref[pl.ds(c * chunk, chunk), :] for c in chunks]
    gcum = [_mm_exact(lower_incl, g) for g in gts]
    gcum_t = [g.T for g in gcum]
    q = [l2_normalized(qa[c][:, head_cols[h]], scale) for c, h in pairs]
    k = [l2_normalized(ka[c][:, head_cols[h]], 1.0) for c, h in pairs]
    v = [va[c][:, head_cols[h]] for c, h in pairs]
    gc = [gcum[c][:, h:h + 1] for c, h in pairs]
    gr = [gcum_t[c][h:h + 1, :] for c, h in pairs]
    beta = [gts[c][:, DN_HEADS + h:DN_HEADS + h + 1] for c, h in pairs]
    g_last = [g[chunk - 1:chunk, :] for g in gc]
    decay = _each(lambda a, b: jnp.exp(jnp.where(causal, a - b, -1e30)), gc, gr)
    kk = _each(_mm_nt, k, k)
    low = _each(lambda m, d, b: jnp.where(strict, m * d * b, 0.0), kk, decay, beta)
    inv = _unit_lower_inverse(low, row, col)
    eg = _each(jnp.exp, gc)
    rhs = _each(lambda vv, kx, b, e: jnp.concatenate([vv * b, kx * (b * e)], axis=1),
                v, k, beta, eg)
    sol = _each(_mm, inv, rhs)
    qk = _each(lambda a, b, d: _mm_nt(a, b) * d, q, k, decay)
    for (c, h), s, qx, e, a, kx, gl, g in zip(pairs, sol, q, eg, qk, k, g_last, gc):
        i = slot(c, h)
        u_s[i] = s[:, :DN_HEAD_DIM]
        wq_s[i] = jnp.concatenate([s[:, DN_HEAD_DIM:], qx * e], axis=0).astype(BF16)
        qk_s[i] = a.astype(BF16)
        kdt_s[i] = (kx * jnp.exp(gl - g)).T.astype(BF16)
    carry_decay = _each(jnp.exp, g_last)

    state = [state_ref[h] for h in heads]
    for c in chunks:
        ids = [slot(c, h) for h in heads]
        ws_qs = [jnp.dot(wq_s[i], st.astype(BF16), preferred_element_type=F32)
                 for i, st in zip(ids, state)]
        v_new = [(u_s[i] - x[:chunk]).astype(BF16) for i, x in zip(ids, ws_qs)]
        o = [x[chunk:] + jnp.dot(qk_s[i], vn, preferred_element_type=F32)
             for i, x, vn in zip(ids, ws_qs, v_new)]
        state = [st * carry_decay[i] + jnp.dot(kdt_s[i], vn, preferred_element_type=F32)
                 for i, st, vn in zip(ids, state, v_new)]
        rows = pl.ds(c * chunk, chunk)
        for h, sl in enumerate(head_cols):
            y = o[h] * _rms_scale(o[h]) * ng_ref[...] * _silu(z_ref[rows, sl])
            out_ref[rows, sl] = y.astype(out_ref.dtype)
    for h in heads:
        state_ref[h] = state[h]


def _deltanet(proj, gates, conv_w, norm_g, *, seq, ts, chunk):
    m = proj.shape[0]
    tiles = seq // ts
    pairs = (ts // chunk) * DN_HEADS
    col0 = 3 * CONV_WIDTH // DN_WIDTH
    row_map = lambda b, t: b * tiles + t
    qkvz = [pl.BlockSpec((ts, DN_WIDTH), functools.partial(lambda b, t, n: (b * tiles + t, col0 + n), n=n))
            for n in range(4)]
    return pl.pallas_call(
        functools.partial(_deltanet_kernel, ts=ts, chunk=chunk),
        grid=(m // seq, tiles),
        in_specs=qkvz + [
            pl.BlockSpec((ts, LANES), lambda b, t: (row_map(b, t), 0)),
            pl.BlockSpec((DN_CONV_K, 3 * DN_WIDTH), lambda b, t: (0, 0)),
            pl.BlockSpec((1, DN_HEAD_DIM), lambda b, t: (0, 0)),
        ],
        out_specs=pl.BlockSpec((ts, DN_WIDTH), lambda b, t: (row_map(b, t), 0)),
        out_shape=jax.ShapeDtypeStruct((m, DN_WIDTH), BF16),
        scratch_shapes=[pltpu.VMEM((ts + SUBLANES_F32, DN_WIDTH), F32)] * 3 + [
            pltpu.VMEM((DN_HEADS, DN_HEAD_DIM, DN_HEAD_DIM), F32),
            pltpu.VMEM((pairs, chunk, DN_HEAD_DIM), F32),
            pltpu.VMEM((pairs, 2 * chunk, DN_HEAD_DIM), BF16),
            pltpu.VMEM((pairs, chunk, chunk), BF16),
            pltpu.VMEM((pairs, DN_HEAD_DIM, chunk), BF16),
        ],
        compiler_params=_params(("parallel", "arbitrary")),
        name="deltanet",
    )(proj, proj, proj, proj, gates, conv_w, norm_g)


def _out_proj_kernel(ax_ref, ab_ref, ac_ref, hax_ref, hac_ref, yb_ref, x_ref, cw_ref, w_ref,
                     out_ref, pbuf, *, tm, seq):
    halo = SUBLANES_F32
    acc_b = jnp.dot(yb_ref[...], w_ref[pl.ds(CONV_WIDTH, DN_WIDTH), :], preferred_element_type=F32)
    seq_start = (pl.program_id(0) * tm) % seq == 0
    pbuf[pl.ds(0, halo), :] = jnp.where(seq_start, 0.0, hac_ref[...] * hax_ref[...])
    pbuf[pl.ds(halo, tm), :] = ac_ref[...] * ax_ref[...]
    acc = None
    for j in range(CONV_K):
        term = pbuf[pl.ds(halo - (CONV_K - 1) + j, tm), :] * cw_ref[pl.ds(j, 1), :]
        acc = term if acc is None else acc + term
    ya = (ab_ref[...] * acc).astype(BF16)
    out_ref[...] = (x_ref[...] + acc_b
                    + jnp.dot(ya, w_ref[pl.ds(0, CONV_WIDTH), :], preferred_element_type=F32))


def _out_proj(proj, yb, x2, conv_w, w_out, *, tm, seq):
    m = x2.shape[0]
    halo = SUBLANES_F32
    halo_row = lambda i: jnp.maximum(i * (tm // halo) - 1, 0)
    return pl.pallas_call(
        functools.partial(_out_proj_kernel, tm=tm, seq=seq),
        grid=(m // tm,),
        in_specs=[
            pl.BlockSpec((tm, CONV_WIDTH), lambda i: (i, 0)),
            pl.BlockSpec((tm, CONV_WIDTH), lambda i: (i, 1)),
            pl.BlockSpec((tm, CONV_WIDTH), lambda i: (i, 2)),
            pl.BlockSpec((halo, CONV_WIDTH), lambda i: (halo_row(i), 0)),
            pl.BlockSpec((halo, CONV_WIDTH), lambda i: (halo_row(i), 2)),
            pl.BlockSpec((tm, DN_WIDTH), lambda i: (i, 0)),
            pl.BlockSpec((tm, D_MODEL), lambda i: (i, 0)),
            pl.BlockSpec((CONV_K, CONV_WIDTH), lambda i: (0, 0)),
            pl.BlockSpec((D_MODEL, D_MODEL), lambda i: (0, 0)),
        ],
        out_specs=pl.BlockSpec((tm, D_MODEL), lambda i: (i, 0)),
        out_shape=jax.ShapeDtypeStruct((m, D_MODEL), F32),
        scratch_shapes=[pltpu.VMEM((tm + halo, CONV_WIDTH), F32)],
        compiler_params=_params(("parallel",)),
        name="out_proj",
    )(proj, proj, proj, proj, proj, yb, x2, conv_w, w_out)


def _ffn_kernel(x_ref, hx_ref, g_ref, wg_ref, wv_ref, cg_ref, cv_ref, wd_ref, out_ref,
                hs_ref, ubuf, *, tm, seq):
    halo = SUBLANES_BF16
    j = pl.program_id(1)

    @pl.when(j == 0)
    def _():
        x = x_ref[...]
        hs_ref[pl.ds(0, tm), :] = (x * _rms_scale(x) * g_ref[...]).astype(BF16)
        hx = hx_ref[...]
        seq_start = (pl.program_id(0) * tm) % seq == 0
        hh = jnp.where(seq_start, 0.0, hx * _rms_scale(hx) * g_ref[...])
        hs_ref[pl.ds(tm, halo), :] = hh.astype(BF16)
        out_ref[...] = x

    def up_conv(w_ref, cw_ref):
        up = jnp.dot(hs_ref[...], w_ref[...], preferred_element_type=F32)
        ubuf[pl.ds(0, halo), :] = up[tm:, :]
        ubuf[pl.ds(halo, tm), :] = up[:tm, :]
        acc = None
        for t in range(FFN_CONV_K):
            term = ubuf[pl.ds(halo - (FFN_CONV_K - 1) + t, tm), :] * cw_ref[pl.ds(t, 1), :]
            acc = term if acc is None else acc + term
        return acc

    gate = up_conv(wg_ref, cg_ref)
    act = _silu(gate)
    val = up_conv(wv_ref, cv_ref)
    act = (act * val).astype(BF16)
    out_ref[...] += jnp.dot(act, wd_ref[...], preferred_element_type=F32)


def _ffn(x1, g, w_up, conv_w, w_down, *, tm, bn, seq):
    m = x1.shape[0]
    halo = SUBLANES_BF16
    nj = D_FF // bn
    return pl.pallas_call(
        functools.partial(_ffn_kernel, tm=tm, seq=seq),
        grid=(m // tm, nj),
        in_specs=[
            pl.BlockSpec((tm, D_MODEL), lambda i, j: (i, 0)),
            pl.BlockSpec((halo, D_MODEL), lambda i, j: (jnp.maximum(i * (tm // halo) - 1, 0), 0)),
            pl.BlockSpec((1, D_MODEL), lambda i, j: (0, 0)),
            pl.BlockSpec((D_MODEL, bn), lambda i, j: (0, j)),
            pl.BlockSpec((D_MODEL, bn), lambda i, j: (0, j + nj)),
            pl.BlockSpec((FFN_CONV_K, bn), lambda i, j: (0, j)),
            pl.BlockSpec((FFN_CONV_K, bn), lambda i, j: (0, j + nj)),
            pl.BlockSpec((bn, D_MODEL), lambda i, j: (j, 0)),
        ],
        out_specs=pl.BlockSpec((tm, D_MODEL), lambda i, j: (i, 0)),
        out_shape=jax.ShapeDtypeStruct((m, D_MODEL), F32),
        scratch_shapes=[pltpu.VMEM((tm + halo, D_MODEL), BF16),
                        pltpu.VMEM((tm + halo, bn), F32)],
        compiler_params=_params(("parallel", "arbitrary")),
        name="conv_ffn",
    )(x1, x1, g, w_up, w_up, conv_w, conv_w, w_down)


def _ple_kernel(x_ref, p_ref, g_ref, wpg_ref, wpp_ref, fg_ref, out_ref):
    x = x_ref[...]
    hb = (x * _rms_scale(x) * g_ref[...]).astype(BF16)
    gate = jax.nn.sigmoid(jnp.dot(hb, wpg_ref[...], preferred_element_type=F32))
    emb = jnp.dot(p_ref[...].astype(BF16), wpp_ref[...], preferred_element_type=F32)
    y = x + gate * emb
    out_ref[...] = y * _rms_scale(y) * fg_ref[...]


def _ple(x2, p2, g, w_pg, w_pp, final_g, *, tm):
    m = x2.shape[0]
    return pl.pallas_call(
        _ple_kernel,
        grid=(m // tm,),
        in_specs=[
            pl.BlockSpec((tm, D_MODEL), lambda i: (i, 0)),
            pl.BlockSpec((tm, PLE_DIM), lambda i: (i, 0)),
            pl.BlockSpec((1, D_MODEL), lambda i: (0, 0)),
            pl.BlockSpec((D_MODEL, D_MODEL), lambda i: (0, 0)),
            pl.BlockSpec((PLE_DIM, D_MODEL), lambda i: (0, 0)),
            pl.BlockSpec((1, D_MODEL), lambda i: (0, 0)),
        ],
        out_specs=pl.BlockSpec((tm, D_MODEL), lambda i: (i, 0)),
        out_shape=jax.ShapeDtypeStruct((m, D_MODEL), F32),
        compiler_params=_params(("parallel",)),
        name="ple",
    )(x2, p2, g, w_pg, w_pp, final_g)


def _layer(x2, p2, seq, norm_mix_g, w_in, conv_a_w, conv_qkv_w, a_log, dt_bias, dn_norm_g,
           w_out, norm_ffn_g, w_up, conv_ffn_w, w_down, norm_ple_g, w_ple_gate, w_ple_proj,
           out_norm_g):
    row = lambda v: v.reshape(1, -1).astype(F32)
    lane_pad = lambda v: jnp.pad(row(v), ((0, 0), (0, LANES - v.shape[-1])))
    w_main = w_in.astype(BF16)
    w_gate = jnp.pad(w_in[:, PROJ_COLS:], ((0, 0), (0, LANES - 2 * DN_HEADS))).astype(BF16)

    proj, gates = _in_proj(x2, row(norm_mix_g), w_main, w_gate, lane_pad(a_log), lane_pad(dt_bias),
                           tm=1024, tn=1024)
    yb = _deltanet(proj, gates, conv_qkv_w.astype(F32), row(dn_norm_g), seq=seq, ts=512,
                   chunk=DN_CHUNK)
    x2 = _out_proj(proj, yb, x2, conv_a_w.astype(F32), w_out.astype(BF16), tm=512, seq=seq)
    x2 = _ffn(x2, row(norm_ffn_g), w_up.astype(BF16), conv_ffn_w.astype(F32), w_down.astype(BF16),
              tm=1024, bn=512, seq=seq)
    return _ple(x2, p2, row(norm_ple_g), w_ple_gate.astype(BF16), w_ple_proj.astype(BF16),
                row(out_norm_g), tm=512)


def kernel(x, p, norm_mix_g, w_in, conv_a_w, conv_qkv_w, a_log, dt_bias, dn_norm_g, w_out,
           norm_ffn_g, w_up, conv_ffn_w, w_down, norm_ple_g, w_ple_gate, w_ple_proj, final_norm_g):
    batch, seq, d_model = x.shape
    depth = p.shape[0]
    assert depth == 1 and d_model == D_MODEL
    x2 = x.reshape(batch * seq, d_model)
    p2 = p[0].reshape(batch * seq, PLE_DIM)
    out = _layer(x2, p2, seq, norm_mix_g[0], w_in[0], conv_a_w[0], conv_qkv_w[0], a_log[0],
                 dt_bias[0], dn_norm_g[0], w_out[0], norm_ffn_g[0], w_up[0], conv_ffn_w[0],
                 w_down[0], norm_ple_g[0], w_ple_gate[0], w_ple_proj[0], final_norm_g)
    return out.reshape(batch, seq, d_model)
```

```python
import functools

import jax
import jax.numpy as jnp
from jax import lax
from jax.experimental import pallas as pl
from jax.experimental.pallas import tpu as pltpu

D_MODEL = 2048
CONV_WIDTH = 1024
CONV_K = 3
DN_HEADS = 8
DN_HEAD_DIM = 128
DN_WIDTH = DN_HEADS * DN_HEAD_DIM
DN_CONV_K = 4
D_FF = 5632
FFN_CONV_K = 3
PLE_DIM = 256
EPS = 1e-6
PROJ_COLS = 3 * CONV_WIDTH + 4 * DN_WIDTH

LANES = 128
SUBLANES_F32 = 8
SUBLANES_BF16 = 16
VMEM_LIMIT_BYTES = 62 * 1024 * 1024

DN_CHUNK = 128
INV_BASE = 8

F32 = jnp.float32
BF16 = jnp.bfloat16


def _mm(a, b):
    return jnp.dot(a.astype(BF16), b.astype(BF16), preferred_element_type=F32)


def _mm_nt(a, b):
    return lax.dot_general(a.astype(BF16), b.astype(BF16), (((1,), (1,)), ((), ())),
                           preferred_element_type=F32)


def _mm_exact(a, b):
    return jnp.dot(a, b, precision=lax.Precision.HIGHEST, preferred_element_type=F32)


def _rms_scale(x):
    return lax.rsqrt(jnp.mean(x * x, axis=-1, keepdims=True) + EPS)


def _silu(x):
    return x * jax.nn.sigmoid(x)


def _params(semantics):
    return pltpu.CompilerParams(dimension_semantics=semantics, vmem_limit_bytes=VMEM_LIMIT_BYTES)


def _in_proj_kernel(x_ref, g_ref, w_ref, wg_ref, alog_ref, dtb_ref, proj_ref, gates_ref, hs_ref):
    @pl.when(pl.program_id(1) == 0)
    def _():
        x = x_ref[...]
        hb = (x * _rms_scale(x) * g_ref[...]).astype(BF16)
        hs_ref[...] = hb
        raw = jnp.dot(hb, wg_ref[...], preferred_element_type=F32)
        a = raw + dtb_ref[...]
        softplus = jnp.maximum(a, 0.0) + jnp.log(1.0 + jnp.exp(-jnp.abs(a)))
        decay = -jnp.exp(alog_ref[...]) * softplus
        lane = lax.broadcasted_iota(jnp.int32, raw.shape, 1)
        gates_ref[...] = jnp.where(lane < DN_HEADS, decay, jax.nn.sigmoid(raw))

    proj_ref[...] = jnp.dot(hs_ref[...], w_ref[...], preferred_element_type=F32)


def _in_proj(x2, g, w_main, w_gate, alog_pad, dtb_pad, *, tm, tn):
    m = x2.shape[0]
    return pl.pallas_call(
        _in_proj_kernel,
        grid=(m // tm, PROJ_COLS // tn),
        in_specs=[
            pl.BlockSpec((tm, D_MODEL), lambda i, j: (i, 0)),
            pl.BlockSpec((1, D_MODEL), lambda i, j: (0, 0)),
            pl.BlockSpec((D_MODEL, tn), lambda i, j: (0, j)),
            pl.BlockSpec((D_MODEL, LANES), lambda i, j: (0, 0)),
            pl.BlockSpec((1, LANES), lambda i, j: (0, 0)),
            pl.BlockSpec((1, LANES), lambda i, j: (0, 0)),
        ],
        out_specs=[
            pl.BlockSpec((tm, tn), lambda i, j: (i, j)),
            pl.BlockSpec((tm, LANES), lambda i, j: (i, 0)),
        ],
        out_shape=[
            jax.ShapeDtypeStruct((m, PROJ_COLS), F32),
            jax.ShapeDtypeStruct((m, LANES), F32),
        ],
        scratch_shapes=[pltpu.VMEM((tm, D_MODEL), BF16)],
        compiler_params=_params(("parallel", "arbitrary")),
        name="in_proj",
    )(x2, g, w_main, w_gate, alog_pad, dtb_pad)


def _each(fn, *lists):
    return [fn(*args) for args in zip(*lists)]


def _unit_lower_inverse(lows, row, col):
    c = lows[0].shape[0]
    eye = (row == col).astype(F32)
    base_shift = INV_BASE.bit_length() - 1
    base_mask = (row >> base_shift) == (col >> base_shift)
    l0 = [jnp.where(base_mask, low, 0.0) for low in lows]
    l2 = _each(_mm, l0, l0)
    l4 = _each(_mm, l2, l2)
    l3 = _each(_mm, l0, l2)
    p1 = _each(lambda a, b, d: eye - a + b - d, l0, l2, l3)
    inv = _each(lambda p, pl4: p + pl4, p1, _each(_mm, p1, l4))
    shift = base_shift
    while (1 << shift) < c:
        off_mask = (((row >> (shift + 1)) == (col >> (shift + 1)))
                    & ((row >> shift) != (col >> shift)))
        off_inv = _each(lambda low, t: _mm(jnp.where(off_mask, low, 0.0), t), lows, inv)
        inv = _each(lambda t, x: t - _mm(t, x), inv, off_inv)
        shift += 1
    return inv


def _deltanet_kernel(q_ref, k_ref, v_ref, z_ref, gates_ref, cw_ref, ng_ref, *refs,
                     ts, chunk, n_cast):
    cast_src = refs[:n_cast]
    out_ref = refs[n_cast]
    cast_dst = refs[n_cast + 1:2 * n_cast + 1]
    qbuf, kbuf, vbuf, state_ref, u_s, wq_s, qk_s, kdt_s = refs[2 * n_cast + 1:]

    halo = SUBLANES_F32
    first = pl.program_id(1) == 0

    @pl.when(first)
    def _():
        state_ref[...] = jnp.zeros_like(state_ref)
        for buf in (qbuf, kbuf, vbuf):
            buf[pl.ds(0, halo), :] = jnp.zeros((halo, DN_WIDTH), F32)

    @pl.when(jnp.logical_not(first))
    def _():
        for buf in (qbuf, kbuf, vbuf):
            buf[pl.ds(0, halo), :] = buf[pl.ds(ts, halo), :]

    for src, buf in ((q_ref, qbuf), (k_ref, kbuf), (v_ref, vbuf)):
        buf[pl.ds(halo, ts), :] = src[...]
    for src, dst in zip(cast_src, cast_dst):
        dst[...] = src[...].astype(dst.dtype)

    row = lax.broadcasted_iota(jnp.int32, (chunk, chunk), 0)
    col = lax.broadcasted_iota(jnp.int32, (chunk, chunk), 1)
    lower_incl = (row >= col).astype(F32)
    causal = row >= col
    strict = row > col
    scale = DN_HEAD_DIM ** -0.5
    heads = range(DN_HEADS)
    chunks = range(ts // chunk)
    head_cols = [slice(h * DN_HEAD_DIM, (h + 1) * DN_HEAD_DIM) for h in heads]
    pairs = [(c, h) for c in chunks for h in heads]
    slot = lambda c, h: c * DN_HEADS + h

    def l2_normalized(x, extra_scale):
        return x * (lax.rsqrt(jnp.sum(x * x, axis=-1, keepdims=True) + EPS) * extra_scale)

    def conv_silu(buf, c, which):
        w = [cw_ref[pl.ds(j, 1), pl.ds(which * DN_WIDTH, DN_WIDTH)] for j in range(DN_CONV_K)]
        cur = buf[pl.ds(c * chunk, chunk + halo), :]
        prev = pltpu.roll(cur, 1, axis=0)
        tail = pltpu.roll(w[1] * cur + w[0] * prev, 2, axis=0)
        return _silu((w[3] * cur + w[2] * prev + tail)[halo:, :])

    qa = [conv_silu(qbuf, c, 0) for c in chunks]
    ka = [conv_silu(kbuf, c, 1) for c in chunks]
    va = [conv_silu(vbuf, c, 2) for c in chunks]
    gts = [gates_ref[pl.ds(c * chunk, chunk), :] for c in chunks]
    gcum = [_mm_exact(lower_incl, g) for g in gts]
    gcum_t = [g.T for g in gcum]
    q = [l2_normalized(qa[c][:, head_cols[h]], scale) for c, h in pairs]
    k = [l2_normalized(ka[c][:, head_cols[h]], 1.0) for c, h in pairs]
    v = [va[c][:, head_cols[h]] for c, h in pairs]
    gc = [gcum[c][:, h:h + 1] for c, h in pairs]
    gr = [gcum_t[c][h:h + 1, :] for c, h in pairs]
    beta = [gts[c][:, DN_HEADS + h:DN_HEADS + h + 1] for c, h in pairs]
    g_last = [g[chunk - 1:chunk, :] for g in gc]
    decay = _each(lambda a, b: jnp.exp(jnp.where(causal, a - b, -1e30)), gc, gr)
    kk = _each(_mm_nt, k, k)
    low = _each(lambda m, d, b: jnp.where(strict, m * d * b, 0.0), kk, decay, beta)
    inv = _unit_lower_inverse(low, row, col)
    eg = _each(jnp.exp, gc)
    rhs = _each(lambda vv, kx, b, e: jnp.concatenate([vv * b, kx * (b * e)], axis=1),
                v, k, beta, eg)
    sol = _each(_mm, inv, rhs)
    qk = _each(lambda a, b, d: _mm_nt(a, b) * d, q, k, decay)
    for (c, h), s, qx, e, a, kx, gl, g in zip(pairs, sol, q, eg, qk, k, g_last, gc):
        i = slot(c, h)
        u_s[i] = s[:, :DN_HEAD_DIM]
        wq_s[i] = jnp.concatenate([s[:, DN_HEAD_DIM:], qx * e], axis=0).astype(BF16)
        qk_s[i] = a.astype(BF16)
        kdt_s[i] = (kx * jnp.exp(gl - g)).T.astype(BF16)
    carry_decay = _each(jnp.exp, g_last)

    state = [state_ref[h] for h in heads]
    for c in chunks:
        ids = [slot(c, h) for h in heads]
        ws_qs = [jnp.dot(wq_s[i], st.astype(BF16), preferred_element_type=F32)
                 for i, st in zip(ids, state)]
        v_new = [(u_s[i] - x[:chunk]).astype(BF16) for i, x in zip(ids, ws_qs)]
        o = [x[chunk:] + jnp.dot(qk_s[i], vn, preferred_element_type=F32)
             for i, x, vn in zip(ids, ws_qs, v_new)]
        state = [st * carry_decay[i] + jnp.dot(kdt_s[i], vn, preferred_element_type=F32)
                 for i, st, vn in zip(ids, state, v_new)]
        rows = pl.ds(c * chunk, chunk)
        for h, sl in enumerate(head_cols):
            y = o[h] * _rms_scale(o[h]) * ng_ref[...] * _silu(z_ref[rows, sl])
            out_ref[rows, sl] = y.astype(out_ref.dtype)
    for h in heads:
        state_ref[h] = state[h]


def _deltanet(proj, gates, conv_w, norm_g, cast_weights, *, seq, ts, chunk):
    m = proj.shape[0]
    tiles = seq // ts
    steps = (m // seq) * tiles
    pairs = (ts // chunk) * DN_HEADS
    col0 = 3 * CONV_WIDTH // DN_WIDTH
    row_map = lambda b, t: b * tiles + t
    qkvz = [pl.BlockSpec((ts, DN_WIDTH), functools.partial(lambda b, t, n: (b * tiles + t, col0 + n), n=n))
            for n in range(4)]
    for w in cast_weights:
        assert w.shape[0] % (steps * SUBLANES_BF16) == 0, w.shape
    cast_specs = [pl.BlockSpec((w.shape[0] // steps, w.shape[1]), lambda b, t: (row_map(b, t), 0))
                  for w in cast_weights]
    outs = pl.pallas_call(
        functools.partial(_deltanet_kernel, ts=ts, chunk=chunk, n_cast=len(cast_weights)),
        grid=(m // seq, tiles),
        in_specs=qkvz + [
            pl.BlockSpec((ts, LANES), lambda b, t: (row_map(b, t), 0)),
            pl.BlockSpec((DN_CONV_K, 3 * DN_WIDTH), lambda b, t: (0, 0)),
            pl.BlockSpec((1, DN_HEAD_DIM), lambda b, t: (0, 0)),
        ] + cast_specs,
        out_specs=[pl.BlockSpec((ts, DN_WIDTH), lambda b, t: (row_map(b, t), 0))] + cast_specs,
        out_shape=[jax.ShapeDtypeStruct((m, DN_WIDTH), BF16)]
        + [jax.ShapeDtypeStruct(w.shape, BF16) for w in cast_weights],
        scratch_shapes=[pltpu.VMEM((ts + SUBLANES_F32, DN_WIDTH), F32)] * 3 + [
            pltpu.VMEM((DN_HEADS, DN_HEAD_DIM, DN_HEAD_DIM), F32),
            pltpu.VMEM((pairs, chunk, DN_HEAD_DIM), F32),
            pltpu.VMEM((pairs, 2 * chunk, DN_HEAD_DIM), BF16),
            pltpu.VMEM((pairs, chunk, chunk), BF16),
            pltpu.VMEM((pairs, DN_HEAD_DIM, chunk), BF16),
        ],
        compiler_params=_params(("parallel", "arbitrary")),
        name="deltanet",
    )(proj, proj, proj, proj, gates, conv_w, norm_g, *cast_weights)
    return outs[0], outs[1:]


def _out_proj_kernel(ax_ref, ab_ref, ac_ref, hax_ref, hac_ref, yb_ref, x_ref, cw_ref, w_ref,
                     out_ref, pbuf, *, tm, seq):
    halo = SUBLANES_F32
    acc_b = jnp.dot(yb_ref[...], w_ref[pl.ds(CONV_WIDTH, DN_WIDTH), :], preferred_element_type=F32)
    seq_start = (pl.program_id(0) * tm) % seq == 0
    pbuf[pl.ds(0, halo), :] = jnp.where(seq_start, 0.0, hac_ref[...] * hax_ref[...])
    pbuf[pl.ds(halo, tm), :] = ac_ref[...] * ax_ref[...]
    acc = None
    for j in range(CONV_K):
        term = pbuf[pl.ds(halo - (CONV_K - 1) + j, tm), :] * cw_ref[pl.ds(j, 1), :]
        acc = term if acc is None else acc + term
    ya = (ab_ref[...] * acc).astype(BF16)
    out_ref[...] = (x_ref[...] + acc_b
                    + jnp.dot(ya, w_ref[pl.ds(0, CONV_WIDTH), :], preferred_element_type=F32))


def _out_proj(proj, yb, x2, conv_w, w_out, *, tm, seq):
    m = x2.shape[0]
    halo = SUBLANES_F32
    halo_row = lambda i: jnp.maximum(i * (tm // halo) - 1, 0)
    return pl.pallas_call(
        functools.partial(_out_proj_kernel, tm=tm, seq=seq),
        grid=(m // tm,),
        in_specs=[
            pl.BlockSpec((tm, CONV_WIDTH), lambda i: (i, 0)),
            pl.BlockSpec((tm, CONV_WIDTH), lambda i: (i, 1)),
            pl.BlockSpec((tm, CONV_WIDTH), lambda i: (i, 2)),
            pl.BlockSpec((halo, CONV_WIDTH), lambda i: (halo_row(i), 0)),
            pl.BlockSpec((halo, CONV_WIDTH), lambda i: (halo_row(i), 2)),
            pl.BlockSpec((tm, DN_WIDTH), lambda i: (i, 0)),
            pl.BlockSpec((tm, D_MODEL), lambda i: (i, 0)),
            pl.BlockSpec((CONV_K, CONV_WIDTH), lambda i: (0, 0)),
            pl.BlockSpec((D_MODEL, D_MODEL), lambda i: (0, 0)),
        ],
        out_specs=pl.BlockSpec((tm, D_MODEL), lambda i: (i, 0)),
        out_shape=jax.ShapeDtypeStruct((m, D_MODEL), F32),
        scratch_shapes=[pltpu.VMEM((tm + halo, CONV_WIDTH), F32)],
        compiler_params=_params(("parallel",)),
        name="out_proj",
    )(proj, proj, proj, proj, proj, yb, x2, conv_w, w_out)


def _ffn_kernel(x_ref, hx_ref, g_ref, wg_ref, wv_ref, cg_ref, cv_ref, wd_ref, out_ref,
                hs_ref, ubuf, *, tm, seq):
    halo = SUBLANES_BF16
    j = pl.program_id(1)

    @pl.when(j == 0)
    def _():
        x = x_ref[...]
        hs_ref[pl.ds(0, tm), :] = (x * _rms_scale(x) * g_ref[...]).astype(BF16)
        hx = hx_ref[...]
        seq_start = (pl.program_id(0) * tm) % seq == 0
        hh = jnp.where(seq_start, 0.0, hx * _rms_scale(hx) * g_ref[...])
        hs_ref[pl.ds(tm, halo), :] = hh.astype(BF16)
        out_ref[...] = x

    def up_conv(w_ref, cw_ref):
        up = jnp.dot(hs_ref[...], w_ref[...], preferred_element_type=F32)
        ubuf[pl.ds(0, halo), :] = up[tm:, :]
        ubuf[pl.ds(halo, tm), :] = up[:tm, :]
        acc = None
        for t in range(FFN_CONV_K):
            term = ubuf[pl.ds(halo - (FFN_CONV_K - 1) + t, tm), :] * cw_ref[pl.ds(t, 1), :]
            acc = term if acc is None else acc + term
        return acc

    gate = up_conv(wg_ref, cg_ref)
    act = _silu(gate)
    val = up_conv(wv_ref, cv_ref)
    act = (act * val).astype(BF16)
    out_ref[...] += jnp.dot(act, wd_ref[...], preferred_element_type=F32)


def _ffn(x1, g, w_up, conv_w, w_down, *, tm, bn, seq):
    m = x1.shape[0]
    halo = SUBLANES_BF16
    nj = D_FF // bn
    return pl.pallas_call(
        functools.partial(_ffn_kernel, tm=tm, seq=seq),
        grid=(m // tm, nj),
        in_specs=[
            pl.BlockSpec((tm, D_MODEL), lambda i, j: (i, 0)),
            pl.BlockSpec((halo, D_MODEL), lambda i, j: (jnp.maximum(i * (tm // halo) - 1, 0), 0)),
            pl.BlockSpec((1, D_MODEL), lambda i, j: (0, 0)),
            pl.BlockSpec((D_MODEL, bn), lambda i, j: (0, j)),
            pl.BlockSpec((D_MODEL, bn), lambda i, j: (0, j + nj)),
            pl.BlockSpec((FFN_CONV_K, bn), lambda i, j: (0, j)),
            pl.BlockSpec((FFN_CONV_K, bn), lambda i, j: (0, j + nj)),
            pl.BlockSpec((bn, D_MODEL), lambda i, j: (j, 0)),
        ],
        out_specs=pl.BlockSpec((tm, D_MODEL), lambda i, j: (i, 0)),
        out_shape=jax.ShapeDtypeStruct((m, D_MODEL), F32),
        scratch_shapes=[pltpu.VMEM((tm + halo, D_MODEL), BF16),
                        pltpu.VMEM((tm + halo, bn), F32)],
        compiler_params=_params(("parallel", "arbitrary")),
        name="conv_ffn",
    )(x1, x1, g, w_up, w_up, conv_w, conv_w, w_down)


def _ple_kernel(x_ref, p_ref, g_ref, wpg_ref, wpp_ref, fg_ref, out_ref):
    x = x_ref[...]
    hb = (x * _rms_scale(x) * g_ref[...]).astype(BF16)
    gate = jax.nn.sigmoid(jnp.dot(hb, wpg_ref[...], preferred_element_type=F32))
    emb = jnp.dot(p_ref[...].astype(BF16), wpp_ref[...], preferred_element_type=F32)
    y = x + gate * emb
    out_ref[...] = y * _rms_scale(y) * fg_ref[...]


def _ple(x2, p2, g, w_pg, w_pp, final_g, *, tm):
    m = x2.shape[0]
    return pl.pallas_call(
        _ple_kernel,
        grid=(m // tm,),
        in_specs=[
            pl.BlockSpec((tm, D_MODEL), lambda i: (i, 0)),
            pl.BlockSpec((tm, PLE_DIM), lambda i: (i, 0)),
            pl.BlockSpec((1, D_MODEL), lambda i: (0, 0)),
            pl.BlockSpec((D_MODEL, D_MODEL), lambda i: (0, 0)),
            pl.BlockSpec((PLE_DIM, D_MODEL), lambda i: (0, 0)),
            pl.BlockSpec((1, D_MODEL), lambda i: (0, 0)),
        ],
        out_specs=pl.BlockSpec((tm, D_MODEL), lambda i: (i, 0)),
        out_shape=jax.ShapeDtypeStruct((m, D_MODEL), F32),
        compiler_params=_params(("parallel",)),
        name="ple",
    )(x2, p2, g, w_pg, w_pp, final_g)


def _layer(x2, p2, seq, norm_mix_g, w_in, conv_a_w, conv_qkv_w, a_log, dt_bias, dn_norm_g,
           w_out, norm_ffn_g, w_up, conv_ffn_w, w_down, norm_ple_g, w_ple_gate, w_ple_proj,
           out_norm_g):
    row = lambda v: v.reshape(1, -1).astype(F32)
    lane_pad = lambda v: jnp.pad(row(v), ((0, 0), (0, LANES - v.shape[-1])))
    w_main = w_in.astype(BF16)
    w_gate = jnp.pad(w_in[:, PROJ_COLS:], ((0, 0), (0, LANES - 2 * DN_HEADS))).astype(BF16)

    proj, gates = _in_proj(x2, row(norm_mix_g), w_main, w_gate, lane_pad(a_log), lane_pad(dt_bias),
                           tm=1024, tn=1024)
    yb, (w_up_b, w_down_b) = _deltanet(proj, gates, conv_qkv_w.astype(F32), row(dn_norm_g),
                                       (w_up, w_down), seq=seq, ts=512, chunk=DN_CHUNK)
    x2 = _out_proj(proj, yb, x2, conv_a_w.astype(F32), w_out.astype(BF16), tm=512, seq=seq)
    x2 = _ffn(x2, row(norm_ffn_g), w_up_b, conv_ffn_w.astype(F32), w_down_b,
              tm=1024, bn=512, seq=seq)
    return _ple(x2, p2, row(norm_ple_g), w_ple_gate.astype(BF16), w_ple_proj.astype(BF16),
                row(out_norm_g), tm=512)


def kernel(x, p, norm_mix_g, w_in, conv_a_w, conv_qkv_w, a_log, dt_bias, dn_norm_g, w_out,
           norm_ffn_g, w_up, conv_ffn_w, w_down, norm_ple_g, w_ple_gate, w_ple_proj, final_norm_g):
    batch, seq, d_model = x.shape
    depth = p.shape[0]
    assert depth == 1 and d_model == D_MODEL
    x2 = x.reshape(batch * seq, d_model)
    p2 = p[0].reshape(batch * seq, PLE_DIM)
    out = _layer(x2, p2, seq, norm_mix_g[0], w_in[0], conv_a_w[0], conv_qkv_w[0], a_log[0],
                 dt_bias[0], dn_norm_g[0], w_out[0], norm_ffn_g[0], w_up[0], conv_ffn_w[0],
                 w_down[0], norm_ple_g[0], w_ple_gate[0], w_ple_proj[0], final_norm_g)
    return out.reshape(batch, seq, d_model)
```

```python
import functools

import jax
import jax.numpy as jnp
from jax import lax
from jax.experimental import pallas as pl
from jax.experimental.pallas import tpu as pltpu

D_MODEL = 2048
CONV_WIDTH = 1024
CONV_K = 3
DN_HEADS = 8
DN_HEAD_DIM = 128
DN_WIDTH = DN_HEADS * DN_HEAD_DIM
DN_CONV_K = 4
D_FF = 5632
FFN_CONV_K = 3
PLE_DIM = 256
EPS = 1e-6
PROJ_COLS = 3 * CONV_WIDTH + 4 * DN_WIDTH

LANES = 128
SUBLANES_F32 = 8
SUBLANES_BF16 = 16
VMEM_LIMIT_BYTES = 62 * 1024 * 1024

DN_CHUNK = 128
INV_BASE = 8

F32 = jnp.float32
BF16 = jnp.bfloat16


def _mm(a, b):
    return jnp.dot(a.astype(BF16), b.astype(BF16), preferred_element_type=F32)


def _mm_nt(a, b):
    return lax.dot_general(a.astype(BF16), b.astype(BF16), (((1,), (1,)), ((), ())),
                           preferred_element_type=F32)


def _mm_exact(a, b):
    return jnp.dot(a, b, precision=lax.Precision.HIGHEST, preferred_element_type=F32)


def _rms_scale(x):
    return lax.rsqrt(jnp.mean(x * x, axis=-1, keepdims=True) + EPS)


def _silu(x):
    return x * jax.nn.sigmoid(x)


def _params(semantics):
    return pltpu.CompilerParams(dimension_semantics=semantics, vmem_limit_bytes=VMEM_LIMIT_BYTES)


def _in_proj_kernel(x_ref, g_ref, w_ref, wg_ref, alog_ref, dtb_ref, *refs, n_cast):
    cast_src = refs[:n_cast]
    proj_ref, gates_ref = refs[n_cast:n_cast + 2]
    cast_dst = refs[n_cast + 2:2 * n_cast + 2]
    hs_ref = refs[2 * n_cast + 2]

    @pl.when(pl.program_id(1) == 0)
    def _():
        x = x_ref[...]
        hb = (x * _rms_scale(x) * g_ref[...]).astype(BF16)
        hs_ref[...] = hb
        raw = jnp.dot(hb, wg_ref[...], preferred_element_type=F32)
        a = raw + dtb_ref[...]
        softplus = jnp.maximum(a, 0.0) + jnp.log(1.0 + jnp.exp(-jnp.abs(a)))
        decay = -jnp.exp(alog_ref[...]) * softplus
        lane = lax.broadcasted_iota(jnp.int32, raw.shape, 1)
        gates_ref[...] = jnp.where(lane < DN_HEADS, decay, jax.nn.sigmoid(raw))
        for src, dst in zip(cast_src, cast_dst):
            dst[...] = src[...].astype(dst.dtype)

    proj_ref[...] = jnp.dot(hs_ref[...], w_ref[...], preferred_element_type=F32)


def _in_proj(x2, g, w_main, w_gate, alog_pad, dtb_pad, cast_weights, *, tm, tn):
    m = x2.shape[0]
    row_tiles = m // tm
    for w in cast_weights:
        assert w.shape[0] % (row_tiles * SUBLANES_BF16) == 0, w.shape
    cast_specs = [pl.BlockSpec((w.shape[0] // row_tiles, w.shape[1]), lambda i, j: (i, 0))
                  for w in cast_weights]
    outs = pl.pallas_call(
        functools.partial(_in_proj_kernel, n_cast=len(cast_weights)),
        grid=(row_tiles, PROJ_COLS // tn),
        in_specs=[
            pl.BlockSpec((tm, D_MODEL), lambda i, j: (i, 0)),
            pl.BlockSpec((1, D_MODEL), lambda i, j: (0, 0)),
            pl.BlockSpec((D_MODEL, tn), lambda i, j: (0, j)),
            pl.BlockSpec((D_MODEL, LANES), lambda i, j: (0, 0)),
            pl.BlockSpec((1, LANES), lambda i, j: (0, 0)),
            pl.BlockSpec((1, LANES), lambda i, j: (0, 0)),
        ] + cast_specs,
        out_specs=[
            pl.BlockSpec((tm, tn), lambda i, j: (i, j)),
            pl.BlockSpec((tm, LANES), lambda i, j: (i, 0)),
        ] + cast_specs,
        out_shape=[
            jax.ShapeDtypeStruct((m, PROJ_COLS), F32),
            jax.ShapeDtypeStruct((m, LANES), F32),
        ] + [jax.ShapeDtypeStruct(w.shape, BF16) for w in cast_weights],
        scratch_shapes=[pltpu.VMEM((tm, D_MODEL), BF16)],
        compiler_params=_params(("parallel", "arbitrary")),
        name="in_proj",
    )(x2, g, w_main, w_gate, alog_pad, dtb_pad, *cast_weights)
    return outs[0], outs[1], outs[2:]


def _each(fn, *lists):
    return [fn(*args) for args in zip(*lists)]


def _unit_lower_inverse(lows, row, col):
    c = lows[0].shape[0]
    eye = (row == col).astype(F32)
    base_shift = INV_BASE.bit_length() - 1
    base_mask = (row >> base_shift) == (col >> base_shift)
    l0 = [jnp.where(base_mask, low, 0.0) for low in lows]
    l2 = _each(_mm, l0, l0)
    l4 = _each(_mm, l2, l2)
    l3 = _each(_mm, l0, l2)
    p1 = _each(lambda a, b, d: eye - a + b - d, l0, l2, l3)
    inv = _each(lambda p, pl4: p + pl4, p1, _each(_mm, p1, l4))
    shift = base_shift
    while (1 << shift) < c:
        off_mask = (((row >> (shift + 1)) == (col >> (shift + 1)))
                    & ((row >> shift) != (col >> shift)))
        off_inv = _each(lambda low, t: _mm(jnp.where(off_mask, low, 0.0), t), lows, inv)
        inv = _each(lambda t, x: t - _mm(t, x), inv, off_inv)
        shift += 1
    return inv


def _deltanet_kernel(q_ref, k_ref, v_ref, z_ref, gates_ref, cw_ref, ng_ref, *refs,
                     ts, chunk, n_cast):
    cast_src = refs[:n_cast]
    out_ref = refs[n_cast]
    cast_dst = refs[n_cast + 1:2 * n_cast + 1]
    qbuf, kbuf, vbuf, state_ref, u_s, wq_s, qk_s, kdt_s = refs[2 * n_cast + 1:]

    halo = SUBLANES_F32
    first = pl.program_id(1) == 0

    @pl.when(first)
    def _():
        state_ref[...] = jnp.zeros_like(state_ref)
        for buf in (qbuf, kbuf, vbuf):
            buf[pl.ds(0, halo), :] = jnp.zeros((halo, DN_WIDTH), F32)

    @pl.when(jnp.logical_not(first))
    def _():
        for buf in (qbuf, kbuf, vbuf):
            buf[pl.ds(0, halo), :] = buf[pl.ds(ts, halo), :]

    for src, buf in ((q_ref, qbuf), (k_ref, kbuf), (v_ref, vbuf)):
        buf[pl.ds(halo, ts), :] = src[...]
    for src, dst in zip(cast_src, cast_dst):
        dst[...] = src[...].astype(dst.dtype)

    row = lax.broadcasted_iota(jnp.int32, (chunk, chunk), 0)
    col = lax.broadcasted_iota(jnp.int32, (chunk, chunk), 1)
    lower_incl = (row >= col).astype(F32)
    causal = row >= col
    strict = row > col
    scale = DN_HEAD_DIM ** -0.5
    heads = range(DN_HEADS)
    chunks = range(ts // chunk)
    head_cols = [slice(h * DN_HEAD_DIM, (h + 1) * DN_HEAD_DIM) for h in heads]
    pairs = [(c, h) for c in chunks for h in heads]
    slot = lambda c, h: c * DN_HEADS + h

    def l2_normalized(x, extra_scale):
        return x * (lax.rsqrt(jnp.sum(x * x, axis=-1, keepdims=True) + EPS) * extra_scale)

    def conv_silu(buf, c, which):
        w = [cw_ref[pl.ds(j, 1), pl.ds(which * DN_WIDTH, DN_WIDTH)] for j in range(DN_CONV_K)]
        cur = buf[pl.ds(c * chunk, chunk + halo), :]
        prev = pltpu.roll(cur, 1, axis=0)
        tail = pltpu.roll(w[1] * cur + w[0] * prev, 2, axis=0)
        return _silu((w[3] * cur + w[2] * prev + tail)[halo:, :])

    qa = [conv_silu(qbuf, c, 0) for c in chunks]
    ka = [conv_silu(kbuf, c, 1) for c in chunks]
    va = [conv_silu(vbuf, c, 2) for c in chunks]
    gts = [gates_ref[pl.ds(c * chunk, chunk), :] for c in chunks]
    gcum = [_mm_exact(lower_incl, g) for g in gts]
    gcum_t = [g.T for g in gcum]
    q = [l2_normalized(qa[c][:, head_cols[h]], scale) for c, h in pairs]
    k = [l2_normalized(ka[c][:, head_cols[h]], 1.0) for c, h in pairs]
    v = [va[c][:, head_cols[h]] for c, h in pairs]
    gc = [gcum[c][:, h:h + 1] for c, h in pairs]
    gr = [gcum_t[c][h:h + 1, :] for c, h in pairs]
    beta = [gts[c][:, DN_HEADS + h:DN_HEADS + h + 1] for c, h in pairs]
    g_last = [g[chunk - 1:chunk, :] for g in gc]
    decay = _each(lambda a, b: jnp.exp(jnp.where(causal, a - b, -1e30)), gc, gr)
    kk = _each(_mm_nt, k, k)
    low = _each(lambda m, d, b: jnp.where(strict, m * d * b, 0.0), kk, decay, beta)
    inv = _unit_lower_inverse(low, row, col)
    eg = _each(jnp.exp, gc)
    rhs = _each(lambda vv, kx, b, e: jnp.concatenate([vv * b, kx * (b * e)], axis=1),
                v, k, beta, eg)
    sol = _each(_mm, inv, rhs)
    qk = _each(lambda a, b, d: _mm_nt(a, b) * d, q, k, decay)
    for (c, h), s, qx, e, a, kx, gl, g in zip(pairs, sol, q, eg, qk, k, g_last, gc):
        i = slot(c, h)
        u_s[i] = s[:, :DN_HEAD_DIM]
        wq_s[i] = jnp.concatenate([s[:, DN_HEAD_DIM:], qx * e], axis=0).astype(BF16)
        qk_s[i] = a.astype(BF16)
        kdt_s[i] = (kx * jnp.exp(gl - g)).T.astype(BF16)
    carry_decay = _each(jnp.exp, g_last)

    state = [state_ref[h] for h in heads]
    for c in chunks:
        ids = [slot(c, h) for h in heads]
        ws_qs = [jnp.dot(wq_s[i], st.astype(BF16), preferred_element_type=F32)
                 for i, st in zip(ids, state)]
        v_new = [(u_s[i] - x[:chunk]).astype(BF16) for i, x in zip(ids, ws_qs)]
        o = [x[chunk:] + jnp.dot(qk_s[i], vn, preferred_element_type=F32)
             for i, x, vn in zip(ids, ws_qs, v_new)]
        state = [st * carry_decay[i] + jnp.dot(kdt_s[i], vn, preferred_element_type=F32)
                 for i, st, vn in zip(ids, state, v_new)]
        rows = pl.ds(c * chunk, chunk)
        for h, sl in enumerate(head_cols):
            y = o[h] * _rms_scale(o[h]) * ng_ref[...] * _silu(z_ref[rows, sl])
            out_ref[rows, sl] = y.astype(out_ref.dtype)
    for h in heads:
        state_ref[h] = state[h]


def _deltanet(proj, gates, conv_w, norm_g, cast_weights, *, seq, ts, chunk):
    m = proj.shape[0]
    tiles = seq // ts
    steps = (m // seq) * tiles
    pairs = (ts // chunk) * DN_HEADS
    col0 = 3 * CONV_WIDTH // DN_WIDTH
    row_map = lambda b, t: b * tiles + t
    qkvz = [pl.BlockSpec((ts, DN_WIDTH), functools.partial(lambda b, t, n: (b * tiles + t, col0 + n), n=n))
            for n in range(4)]
    for w in cast_weights:
        assert w.shape[0] % (steps * SUBLANES_BF16) == 0, w.shape
    cast_specs = [pl.BlockSpec((w.shape[0] // steps, w.shape[1]), lambda b, t: (row_map(b, t), 0))
                  for w in cast_weights]
    outs = pl.pallas_call(
        functools.partial(_deltanet_kernel, ts=ts, chunk=chunk, n_cast=len(cast_weights)),
        grid=(m // seq, tiles),
        in_specs=qkvz + [
            pl.BlockSpec((ts, LANES), lambda b, t: (row_map(b, t), 0)),
            pl.BlockSpec((DN_CONV_K, 3 * DN_WIDTH), lambda b, t: (0, 0)),
            pl.BlockSpec((1, DN_HEAD_DIM), lambda b, t: (0, 0)),
        ] + cast_specs,
        out_specs=[pl.BlockSpec((ts, DN_WIDTH), lambda b, t: (row_map(b, t), 0))] + cast_specs,
        out_shape=[jax.ShapeDtypeStruct((m, DN_WIDTH), BF16)]
        + [jax.ShapeDtypeStruct(w.shape, BF16) for w in cast_weights],
        scratch_shapes=[pltpu.VMEM((ts + SUBLANES_F32, DN_WIDTH), F32)] * 3 + [
            pltpu.VMEM((DN_HEADS, DN_HEAD_DIM, DN_HEAD_DIM), F32),
            pltpu.VMEM((pairs, chunk, DN_HEAD_DIM), F32),
            pltpu.VMEM((pairs, 2 * chunk, DN_HEAD_DIM), BF16),
            pltpu.VMEM((pairs, chunk, chunk), BF16),
            pltpu.VMEM((pairs, DN_HEAD_DIM, chunk), BF16),
        ],
        compiler_params=_params(("parallel", "arbitrary")),
        name="deltanet",
    )(proj, proj, proj, proj, gates, conv_w, norm_g, *cast_weights)
    return outs[0], outs[1:]


def _out_proj_kernel(ax_ref, ab_ref, ac_ref, hax_ref, hac_ref, yb_ref, x_ref, cw_ref, w_ref,
                     out_ref, pbuf, *, tm, seq):
    halo = SUBLANES_F32
    acc_b = jnp.dot(yb_ref[...], w_ref[pl.ds(CONV_WIDTH, DN_WIDTH), :], preferred_element_type=F32)
    seq_start = (pl.program_id(0) * tm) % seq == 0
    pbuf[pl.ds(0, halo), :] = jnp.where(seq_start, 0.0, hac_ref[...] * hax_ref[...])
    pbuf[pl.ds(halo, tm), :] = ac_ref[...] * ax_ref[...]
    acc = None
    for j in range(CONV_K):
        term = pbuf[pl.ds(halo - (CONV_K - 1) + j, tm), :] * cw_ref[pl.ds(j, 1), :]
        acc = term if acc is None else acc + term
    ya = (ab_ref[...] * acc).astype(BF16)
    out_ref[...] = (x_ref[...] + acc_b
                    + jnp.dot(ya, w_ref[pl.ds(0, CONV_WIDTH), :], preferred_element_type=F32))


def _out_proj(proj, yb, x2, conv_w, w_out, *, tm, seq):
    m = x2.shape[0]
    halo = SUBLANES_F32
    halo_row = lambda i: jnp.maximum(i * (tm // halo) - 1, 0)
    return pl.pallas_call(
        functools.partial(_out_proj_kernel, tm=tm, seq=seq),
        grid=(m // tm,),
        in_specs=[
            pl.BlockSpec((tm, CONV_WIDTH), lambda i: (i, 0)),
            pl.BlockSpec((tm, CONV_WIDTH), lambda i: (i, 1)),
            pl.BlockSpec((tm, CONV_WIDTH), lambda i: (i, 2)),
            pl.BlockSpec((halo, CONV_WIDTH), lambda i: (halo_row(i), 0)),
            pl.BlockSpec((halo, CONV_WIDTH), lambda i: (halo_row(i), 2)),
            pl.BlockSpec((tm, DN_WIDTH), lambda i: (i, 0)),
            pl.BlockSpec((tm, D_MODEL), lambda i: (i, 0)),
            pl.BlockSpec((CONV_K, CONV_WIDTH), lambda i: (0, 0)),
            pl.BlockSpec((D_MODEL, D_MODEL), lambda i: (0, 0)),
        ],
        out_specs=pl.BlockSpec((tm, D_MODEL), lambda i: (i, 0)),
        out_shape=jax.ShapeDtypeStruct((m, D_MODEL), F32),
        scratch_shapes=[pltpu.VMEM((tm + halo, CONV_WIDTH), F32)],
        compiler_params=_params(("parallel",)),
        name="out_proj",
    )(proj, proj, proj, proj, proj, yb, x2, conv_w, w_out)


def _ffn_kernel(x_ref, g_ref, wg_ref, wv_ref, cg_ref, cv_ref, wd_ref, out_ref,
                hs_ref, ubuf, carry, *, tm, seq):
    halo = SUBLANES_F32
    i = pl.program_id(0)
    j = pl.program_id(1)
    seq_start = (i * tm) % seq == 0

    @pl.when(jnp.logical_and(i == 0, j == 0))
    def _():
        carry[...] = jnp.zeros_like(carry)

    @pl.when(j == 0)
    def _():
        x = x_ref[...]
        hs_ref[...] = (x * _rms_scale(x) * g_ref[...]).astype(BF16)
        out_ref[...] = x

    def up_conv(w_ref, cw_ref, which):
        up = jnp.dot(hs_ref[...], w_ref[...], preferred_element_type=F32)
        ubuf[pl.ds(0, halo), :] = jnp.where(seq_start, 0.0, carry[which, j])
        ubuf[pl.ds(halo, tm), :] = up
        carry[which, j] = up[tm - halo:, :]
        acc = None
        for t in range(FFN_CONV_K):
            term = ubuf[pl.ds(halo - (FFN_CONV_K - 1) + t, tm), :] * cw_ref[pl.ds(t, 1), :]
            acc = term if acc is None else acc + term
        return acc

    gate = up_conv(wg_ref, cg_ref, 0)
    act = _silu(gate)
    val = up_conv(wv_ref, cv_ref, 1)
    act = (act * val).astype(BF16)
    out_ref[...] += jnp.dot(act, wd_ref[...], preferred_element_type=F32)


def _ffn(x1, g, w_up, conv_w, w_down, *, tm, bn, seq):
    m = x1.shape[0]
    halo = SUBLANES_F32
    nj = D_FF // bn
    return pl.pallas_call(
        functools.partial(_ffn_kernel, tm=tm, seq=seq),
        grid=(m // tm, nj),
        in_specs=[
            pl.BlockSpec((tm, D_MODEL), lambda i, j: (i, 0)),
            pl.BlockSpec((1, D_MODEL), lambda i, j: (0, 0)),
            pl.BlockSpec((D_MODEL, bn), lambda i, j: (0, j)),
            pl.BlockSpec((D_MODEL, bn), lambda i, j: (0, j + nj)),
            pl.BlockSpec((FFN_CONV_K, bn), lambda i, j: (0, j)),
            pl.BlockSpec((FFN_CONV_K, bn), lambda i, j: (0, j + nj)),
            pl.BlockSpec((bn, D_MODEL), lambda i, j: (j, 0)),
        ],
        out_specs=pl.BlockSpec((tm, D_MODEL), lambda i, j: (i, 0)),
        out_shape=jax.ShapeDtypeStruct((m, D_MODEL), F32),
        scratch_shapes=[pltpu.VMEM((tm, D_MODEL), BF16),
                        pltpu.VMEM((tm + halo, bn), F32),
                        pltpu.VMEM((2, nj, halo, bn), F32)],
        compiler_params=_params(("arbitrary", "arbitrary")),
        name="conv_ffn",
    )(x1, g, w_up, w_up, conv_w, conv_w, w_down)


def _ple_kernel(x_ref, p_ref, g_ref, wpg_ref, wpp_ref, fg_ref, out_ref):
    x = x_ref[...]
    hb = (x * _rms_scale(x) * g_ref[...]).astype(BF16)
    gate = jax.nn.sigmoid(jnp.dot(hb, wpg_ref[...], preferred_element_type=F32))
    emb = jnp.dot(p_ref[...].astype(BF16), wpp_ref[...], preferred_element_type=F32)
    y = x + gate * emb
    out_ref[...] = y * _rms_scale(y) * fg_ref[...]


def _ple(x2, p2, g, w_pg, w_pp, final_g, *, tm):
    m = x2.shape[0]
    return pl.pallas_call(
        _ple_kernel,
        grid=(m // tm,),
        in_specs=[
            pl.BlockSpec((tm, D_MODEL), lambda i: (i, 0)),
            pl.BlockSpec((tm, PLE_DIM), lambda i: (i, 0)),
            pl.BlockSpec((1, D_MODEL), lambda i: (0, 0)),
            pl.BlockSpec((D_MODEL, D_MODEL), lambda i: (0, 0)),
            pl.BlockSpec((PLE_DIM, D_MODEL), lambda i: (0, 0)),
            pl.BlockSpec((1, D_MODEL), lambda i: (0, 0)),
        ],
        out_specs=pl.BlockSpec((tm, D_MODEL), lambda i: (i, 0)),
        out_shape=jax.ShapeDtypeStruct((m, D_MODEL), F32),
        compiler_params=_params(("parallel",)),
        name="ple",
    )(x2, p2, g, w_pg, w_pp, final_g)


def _layer(x2, p2, seq, norm_mix_g, w_in, conv_a_w, conv_qkv_w, a_log, dt_bias, dn_norm_g,
           w_out, norm_ffn_g, w_up, conv_ffn_w, w_down, norm_ple_g, w_ple_gate, w_ple_proj,
           out_norm_g):
    row = lambda v: v.reshape(1, -1).astype(F32)
    lane_pad = lambda v: jnp.pad(row(v), ((0, 0), (0, LANES - v.shape[-1])))
    w_main = w_in.astype(BF16)
    w_gate = jnp.pad(w_in[:, PROJ_COLS:], ((0, 0), (0, LANES - 2 * DN_HEADS))).astype(BF16)

    proj, gates, (w_out_b, w_pg_b) = _in_proj(x2, row(norm_mix_g), w_main, w_gate, lane_pad(a_log),
                                              lane_pad(dt_bias), (w_out, w_ple_gate),
                                              tm=1024, tn=1024)
    yb, (w_up_b, w_down_b) = _deltanet(proj, gates, conv_qkv_w.astype(F32), row(dn_norm_g),
                                       (w_up, w_down), seq=seq, ts=512, chunk=DN_CHUNK)
    x2 = _out_proj(proj, yb, x2, conv_a_w.astype(F32), w_out_b, tm=512, seq=seq)
    x2 = _ffn(x2, row(norm_ffn_g), w_up_b, conv_ffn_w.astype(F32), w_down_b,
              tm=1024, bn=512, seq=seq)
    return _ple(x2, p2, row(norm_ple_g), w_pg_b, w_ple_proj.astype(BF16),
                row(out_norm_g), tm=512)


def kernel(x, p, norm_mix_g, w_in, conv_a_w, conv_qkv_w, a_log, dt_bias, dn_norm_g, w_out,
           norm_ffn_g, w_up, conv_ffn_w, w_down, norm_ple_g, w_ple_gate, w_ple_proj, final_norm_g):
    batch, seq, d_model = x.shape
    depth = p.shape[0]
    assert depth == 1 and d_model == D_MODEL
    x2 = x.reshape(batch * seq, d_model)
    p2 = p[0].reshape(batch * seq, PLE_DIM)
    out = _layer(x2, p2, seq, norm_mix_g[0], w_in[0], conv_a_w[0], conv_qkv_w[0], a_log[0],
                 dt_bias[0], dn_norm_g[0], w_out[0], norm_ffn_g[0], w_up[0], conv_ffn_w[0],
                 w_down[0], norm_ple_g[0], w_ple_gate[0], w_ple_proj[0], final_norm_g)
    return out.reshape(batch, seq, d_model)
```

```python
import functools

import jax
import jax.numpy as jnp
from jax import lax
from jax.experimental import pallas as pl
from jax.experimental.pallas import tpu as pltpu

D_MODEL = 2048
CONV_WIDTH = 1024
CONV_K = 3
DN_HEADS = 8
DN_HEAD_DIM = 128
DN_WIDTH = DN_HEADS * DN_HEAD_DIM
DN_CONV_K = 4
D_FF = 5632
FFN_CONV_K = 3
PLE_DIM = 256
EPS = 1e-6
PROJ_COLS = 3 * CONV_WIDTH + 4 * DN_WIDTH

LANES = 128
SUBLANES_F32 = 8
SUBLANES_BF16 = 16
VMEM_LIMIT_BYTES = 62 * 1024 * 1024

DN_CHUNK = 128
INV_BASE = 8

F32 = jnp.float32
BF16 = jnp.bfloat16


def _mm(a, b):
    return jnp.dot(a.astype(BF16), b.astype(BF16), preferred_element_type=F32)


def _mm_nt(a, b):
    return lax.dot_general(a.astype(BF16), b.astype(BF16), (((1,), (1,)), ((), ())),
                           preferred_element_type=F32)


def _mm_exact(a, b):
    return jnp.dot(a, b, precision=lax.Precision.HIGHEST, preferred_element_type=F32)


def _rms_scale(x):
    return lax.rsqrt(jnp.mean(x * x, axis=-1, keepdims=True) + EPS)


def _silu(x):
    return x * jax.nn.sigmoid(x)


def _params(semantics):
    return pltpu.CompilerParams(dimension_semantics=semantics, vmem_limit_bytes=VMEM_LIMIT_BYTES)


def _in_proj_kernel(x_ref, g_ref, w_ref, wg_ref, alog_ref, dtb_ref, proj_ref, gates_ref, hs_ref):
    @pl.when(pl.program_id(1) == 0)
    def _():
        x = x_ref[...]
        hb = (x * _rms_scale(x) * g_ref[...]).astype(BF16)
        hs_ref[...] = hb
        raw = jnp.dot(hb, wg_ref[...], preferred_element_type=F32)
        a = raw + dtb_ref[...]
        softplus = jnp.maximum(a, 0.0) + jnp.log(1.0 + jnp.exp(-jnp.abs(a)))
        decay = -jnp.exp(alog_ref[...]) * softplus
        lane = lax.broadcasted_iota(jnp.int32, raw.shape, 1)
        gates_ref[...] = jnp.where(lane < DN_HEADS, decay, jax.nn.sigmoid(raw))

    tn = proj_ref.shape[1]
    for block in range(PROJ_COLS // tn):
        @pl.when(pl.program_id(1) == block)
        def _(block=block):
            proj_ref[...] = jnp.dot(hs_ref[...], w_ref[:, pl.ds(block * tn, tn)],
                                    preferred_element_type=F32)


def _in_proj(x2, g, w_main, w_gate, alog_pad, dtb_pad, *, tm, tn):
    m = x2.shape[0]
    return pl.pallas_call(
        _in_proj_kernel,
        grid=(m // tm, PROJ_COLS // tn),
        in_specs=[
            pl.BlockSpec((tm, D_MODEL), lambda i, j: (i, 0)),
            pl.BlockSpec((1, D_MODEL), lambda i, j: (0, 0)),
            pl.BlockSpec(w_main.shape, lambda i, j: (0, 0), pipeline_mode=pl.Buffered(1)),
            pl.BlockSpec((D_MODEL, LANES), lambda i, j: (0, 0)),
            pl.BlockSpec((1, LANES), lambda i, j: (0, 0)),
            pl.BlockSpec((1, LANES), lambda i, j: (0, 0)),
        ],
        out_specs=[
            pl.BlockSpec((tm, tn), lambda i, j: (i, j)),
            pl.BlockSpec((tm, LANES), lambda i, j: (i, 0)),
        ],
        out_shape=[
            jax.ShapeDtypeStruct((m, PROJ_COLS), F32),
            jax.ShapeDtypeStruct((m, LANES), F32),
        ],
        scratch_shapes=[pltpu.VMEM((tm, D_MODEL), BF16)],
        compiler_params=_params(("parallel", "arbitrary")),
        name="in_proj",
    )(x2, g, w_main, w_gate, alog_pad, dtb_pad)


def _each(fn, *lists):
    return [fn(*args) for args in zip(*lists)]


def _unit_lower_inverse(lows, row, col):
    c = lows[0].shape[0]
    eye = (row == col).astype(F32)
    base_shift = INV_BASE.bit_length() - 1
    base_mask = (row >> base_shift) == (col >> base_shift)
    l0 = [jnp.where(base_mask, low, 0.0) for low in lows]
    l2 = _each(_mm, l0, l0)
    l4 = _each(_mm, l2, l2)
    l3 = _each(_mm, l0, l2)
    p1 = _each(lambda a, b, d: eye - a + b - d, l0, l2, l3)
    inv = _each(lambda p, pl4: p + pl4, p1, _each(_mm, p1, l4))
    shift = base_shift
    while (1 << shift) < c:
        off_mask = (((row >> (shift + 1)) == (col >> (shift + 1)))
                    & ((row >> shift) != (col >> shift)))
        off_inv = _each(lambda low, t: _mm(jnp.where(off_mask, low, 0.0), t), lows, inv)
        inv = _each(lambda t, x: t - _mm(t, x), inv, off_inv)
        shift += 1
    return inv


def _deltanet_kernel(q_ref, k_ref, v_ref, z_ref, gates_ref, cw_ref, ng_ref, *refs,
                     ts, chunk, n_cast):
    cast_src = refs[:n_cast]
    out_ref = refs[n_cast]
    cast_dst = refs[n_cast + 1:2 * n_cast + 1]
    qbuf, kbuf, vbuf, state_ref, u_s, wq_s, qk_s, kdt_s = refs[2 * n_cast + 1:]

    halo = SUBLANES_F32
    first = pl.program_id(1) == 0

    @pl.when(first)
    def _():
        state_ref[...] = jnp.zeros_like(state_ref)
        for buf in (qbuf, kbuf, vbuf):
            buf[pl.ds(0, halo), :] = jnp.zeros((halo, DN_WIDTH), F32)

    @pl.when(jnp.logical_not(first))
    def _():
        for buf in (qbuf, kbuf, vbuf):
            buf[pl.ds(0, halo), :] = buf[pl.ds(ts, halo), :]

    for src, buf in ((q_ref, qbuf), (k_ref, kbuf), (v_ref, vbuf)):
        buf[pl.ds(halo, ts), :] = src[...]
    for src, dst in zip(cast_src, cast_dst):
        dst[...] = src[...].astype(dst.dtype)

    row = lax.broadcasted_iota(jnp.int32, (chunk, chunk), 0)
    col = lax.broadcasted_iota(jnp.int32, (chunk, chunk), 1)
    lower_incl = (row >= col).astype(F32)
    causal = row >= col
    strict = row > col
    scale = DN_HEAD_DIM ** -0.5
    heads = range(DN_HEADS)
    chunks = range(ts // chunk)
    head_cols = [slice(h * DN_HEAD_DIM, (h + 1) * DN_HEAD_DIM) for h in heads]
    pairs = [(c, h) for c in chunks for h in heads]
    slot = lambda c, h: c * DN_HEADS + h

    def l2_normalized(x, extra_scale):
        return x * (lax.rsqrt(jnp.sum(x * x, axis=-1, keepdims=True) + EPS) * extra_scale)

    def conv_silu(buf, c, which):
        w = [cw_ref[pl.ds(j, 1), pl.ds(which * DN_WIDTH, DN_WIDTH)] for j in range(DN_CONV_K)]
        cur = buf[pl.ds(c * chunk, chunk + halo), :]
        prev = pltpu.roll(cur, 1, axis=0)
        tail = pltpu.roll(w[1] * cur + w[0] * prev, 2, axis=0)
        return _silu((w[3] * cur + w[2] * prev + tail)[halo:, :])

    qa = [conv_silu(qbuf, c, 0) for c in chunks]
    ka = [conv_silu(kbuf, c, 1) for c in chunks]
    va = [conv_silu(vbuf, c, 2) for c in chunks]
    gts = [gates_ref[pl.ds(c * chunk, chunk), :] for c in chunks]
    gcum = [_mm_exact(lower_incl, g) for g in gts]
    gcum_t = [g.T for g in gcum]
    q = [l2_normalized(qa[c][:, head_cols[h]], scale) for c, h in pairs]
    k = [l2_normalized(ka[c][:, head_cols[h]], 1.0) for c, h in pairs]
    v = [va[c][:, head_cols[h]] for c, h in pairs]
    gc = [gcum[c][:, h:h + 1] for c, h in pairs]
    gr = [gcum_t[c][h:h + 1, :] for c, h in pairs]
    beta = [gts[c][:, DN_HEADS + h:DN_HEADS + h + 1] for c, h in pairs]
    g_last = [g[chunk - 1:chunk, :] for g in gc]
    decay = _each(lambda a, b: jnp.exp(jnp.where(causal, a - b, -1e30)), gc, gr)
    kk = _each(_mm_nt, k, k)
    low = _each(lambda m, d, b: jnp.where(strict, m * d * b, 0.0), kk, decay, beta)
    inv = _unit_lower_inverse(low, row, col)
    eg = _each(jnp.exp, gc)
    rhs = _each(lambda vv, kx, b, e: jnp.concatenate([vv * b, kx * (b * e)], axis=1),
                v, k, beta, eg)
    sol = _each(_mm, inv, rhs)
    qk = _each(lambda a, b, d: _mm_nt(a, b) * d, q, k, decay)
    for (c, h), s, qx, e, a, kx, gl, g in zip(pairs, sol, q, eg, qk, k, g_last, gc):
        i = slot(c, h)
        u_s[i] = s[:, :DN_HEAD_DIM]
        wq_s[i] = jnp.concatenate([s[:, DN_HEAD_DIM:], qx * e], axis=0).astype(BF16)
        qk_s[i] = a.astype(BF16)
        kdt_s[i] = (kx * jnp.exp(gl - g)).T.astype(BF16)
    carry_decay = _each(jnp.exp, g_last)

    state = [state_ref[h] for h in heads]
    for c in chunks:
        ids = [slot(c, h) for h in heads]
        ws_qs = [jnp.dot(wq_s[i], st.astype(BF16), preferred_element_type=F32)
                 for i, st in zip(ids, state)]
        v_new = [(u_s[i] - x[:chunk]).astype(BF16) for i, x in zip(ids, ws_qs)]
        o = [x[chunk:] + jnp.dot(qk_s[i], vn, preferred_element_type=F32)
             for i, x, vn in zip(ids, ws_qs, v_new)]
        state = [st * carry_decay[i] + jnp.dot(kdt_s[i], vn, preferred_element_type=F32)
                 for i, st, vn in zip(ids, state, v_new)]
        rows = pl.ds(c * chunk, chunk)
        for h, sl in enumerate(head_cols):
            y = o[h] * _rms_scale(o[h]) * ng_ref[...] * _silu(z_ref[rows, sl])
            out_ref[rows, sl] = y.astype(out_ref.dtype)
    for h in heads:
        state_ref[h] = state[h]


def _deltanet(proj, gates, conv_w, norm_g, cast_weights, *, seq, ts, chunk):
    m = proj.shape[0]
    tiles = seq // ts
    steps = (m // seq) * tiles
    pairs = (ts // chunk) * DN_HEADS
    col0 = 3 * CONV_WIDTH // DN_WIDTH
    row_map = lambda b, t: b * tiles + t
    qkvz = [pl.BlockSpec((ts, DN_WIDTH), functools.partial(lambda b, t, n: (b * tiles + t, col0 + n), n=n))
            for n in range(4)]
    for w in cast_weights:
        assert w.shape[0] % (steps * SUBLANES_BF16) == 0, w.shape
    cast_specs = [pl.BlockSpec((w.shape[0] // steps, w.shape[1]), lambda b, t: (row_map(b, t), 0))
                  for w in cast_weights]
    outs = pl.pallas_call(
        functools.partial(_deltanet_kernel, ts=ts, chunk=chunk, n_cast=len(cast_weights)),
        grid=(m // seq, tiles),
        in_specs=qkvz + [
            pl.BlockSpec((ts, LANES), lambda b, t: (row_map(b, t), 0)),
            pl.BlockSpec((DN_CONV_K, 3 * DN_WIDTH), lambda b, t: (0, 0)),
            pl.BlockSpec((1, DN_HEAD_DIM), lambda b, t: (0, 0)),
        ] + cast_specs,
        out_specs=[pl.BlockSpec((ts, DN_WIDTH), lambda b, t: (row_map(b, t), 0))] + cast_specs,
        out_shape=[jax.ShapeDtypeStruct((m, DN_WIDTH), BF16)]
        + [jax.ShapeDtypeStruct(w.shape, BF16) for w in cast_weights],
        scratch_shapes=[pltpu.VMEM((ts + SUBLANES_F32, DN_WIDTH), F32)] * 3 + [
            pltpu.VMEM((DN_HEADS, DN_HEAD_DIM, DN_HEAD_DIM), F32),
            pltpu.VMEM((pairs, chunk, DN_HEAD_DIM), F32),
            pltpu.VMEM((pairs, 2 * chunk, DN_HEAD_DIM), BF16),
            pltpu.VMEM((pairs, chunk, chunk), BF16),
            pltpu.VMEM((pairs, DN_HEAD_DIM, chunk), BF16),
        ],
        compiler_params=_params(("parallel", "arbitrary")),
        name="deltanet",
    )(proj, proj, proj, proj, gates, conv_w, norm_g, *cast_weights)
    return outs[0], outs[1:]


def _out_proj_kernel(ax_ref, ab_ref, ac_ref, hax_ref, hac_ref, yb_ref, x_ref, cw_ref, w_ref,
                     out_ref, pbuf, *, tm, seq):
    halo = SUBLANES_F32
    acc_b = jnp.dot(yb_ref[...], w_ref[pl.ds(CONV_WIDTH, DN_WIDTH), :], preferred_element_type=F32)
    seq_start = (pl.program_id(0) * tm) % seq == 0
    pbuf[pl.ds(0, halo), :] = jnp.where(seq_start, 0.0, hac_ref[...] * hax_ref[...])
    pbuf[pl.ds(halo, tm), :] = ac_ref[...] * ax_ref[...]
    acc = None
    for j in range(CONV_K):
        term = pbuf[pl.ds(halo - (CONV_K - 1) + j, tm), :] * cw_ref[pl.ds(j, 1), :]
        acc = term if acc is None else acc + term
    ya = (ab_ref[...] * acc).astype(BF16)
    out_ref[...] = (x_ref[...] + acc_b
                    + jnp.dot(ya, w_ref[pl.ds(0, CONV_WIDTH), :], preferred_element_type=F32))


def _out_proj(proj, yb, x2, conv_w, w_out, *, tm, seq):
    m = x2.shape[0]
    halo = SUBLANES_F32
    halo_row = lambda i: jnp.maximum(i * (tm // halo) - 1, 0)
    return pl.pallas_call(
        functools.partial(_out_proj_kernel, tm=tm, seq=seq),
        grid=(m // tm,),
        in_specs=[
            pl.BlockSpec((tm, CONV_WIDTH), lambda i: (i, 0)),
            pl.BlockSpec((tm, CONV_WIDTH), lambda i: (i, 1)),
            pl.BlockSpec((tm, CONV_WIDTH), lambda i: (i, 2)),
            pl.BlockSpec((halo, CONV_WIDTH), lambda i: (halo_row(i), 0)),
            pl.BlockSpec((halo, CONV_WIDTH), lambda i: (halo_row(i), 2)),
            pl.BlockSpec((tm, DN_WIDTH), lambda i: (i, 0)),
            pl.BlockSpec((tm, D_MODEL), lambda i: (i, 0)),
            pl.BlockSpec((CONV_K, CONV_WIDTH), lambda i: (0, 0)),
            pl.BlockSpec((D_MODEL, D_MODEL), lambda i: (0, 0)),
        ],
        out_specs=pl.BlockSpec((tm, D_MODEL), lambda i: (i, 0)),
        out_shape=jax.ShapeDtypeStruct((m, D_MODEL), F32),
        scratch_shapes=[pltpu.VMEM((tm + halo, CONV_WIDTH), F32)],
        compiler_params=_params(("parallel",)),
        name="out_proj",
    )(proj, proj, proj, proj, proj, yb, x2, conv_w, w_out)


def _ffn_kernel(x_ref, g_ref, wg_ref, wv_ref, cg_ref, cv_ref, wd_ref, out_ref,
                hs_ref, ubuf, carry, *, tm, seq):
    halo = SUBLANES_F32
    i = pl.program_id(0)
    j = pl.program_id(1)
    seq_start = (i * tm) % seq == 0

    @pl.when(jnp.logical_and(i == 0, j == 0))
    def _():
        carry[...] = jnp.zeros_like(carry)

    @pl.when(j == 0)
    def _():
        x = x_ref[...]
        hs_ref[...] = (x * _rms_scale(x) * g_ref[...]).astype(BF16)
        out_ref[...] = x

    def up_conv(w_ref, cw_ref, which):
        up = jnp.dot(hs_ref[...], w_ref[...], preferred_element_type=F32)
        ubuf[pl.ds(0, halo), :] = jnp.where(seq_start, 0.0, carry[which, j])
        ubuf[pl.ds(halo, tm), :] = up
        carry[which, j] = up[tm - halo:, :]
        acc = None
        for t in range(FFN_CONV_K):
            term = ubuf[pl.ds(halo - (FFN_CONV_K - 1) + t, tm), :] * cw_ref[pl.ds(t, 1), :]
            acc = term if acc is None else acc + term
        return acc

    gate = up_conv(wg_ref, cg_ref, 0)
    act = _silu(gate)
    val = up_conv(wv_ref, cv_ref, 1)
    act = (act * val).astype(BF16)
    out_ref[...] += jnp.dot(act, wd_ref[...], preferred_element_type=F32)


def _ffn(x1, g, w_up, conv_w, w_down, *, tm, bn, seq):
    m = x1.shape[0]
    halo = SUBLANES_F32
    nj = D_FF // bn
    return pl.pallas_call(
        functools.partial(_ffn_kernel, tm=tm, seq=seq),
        grid=(m // tm, nj),
        in_specs=[
            pl.BlockSpec((tm, D_MODEL), lambda i, j: (i, 0)),
            pl.BlockSpec((1, D_MODEL), lambda i, j: (0, 0)),
            pl.BlockSpec((D_MODEL, bn), lambda i, j: (0, j)),
            pl.BlockSpec((D_MODEL, bn), lambda i, j: (0, j + nj)),
            pl.BlockSpec((FFN_CONV_K, bn), lambda i, j: (0, j)),
            pl.BlockSpec((FFN_CONV_K, bn), lambda i, j: (0, j + nj)),
            pl.BlockSpec((bn, D_MODEL), lambda i, j: (j, 0)),
        ],
        out_specs=pl.BlockSpec((tm, D_MODEL), lambda i, j: (i, 0)),
        out_shape=jax.ShapeDtypeStruct((m, D_MODEL), F32),
        scratch_shapes=[pltpu.VMEM((tm, D_MODEL), BF16),
                        pltpu.VMEM((tm + halo, bn), F32),
                        pltpu.VMEM((2, nj, halo, bn), F32)],
        compiler_params=_params(("arbitrary", "arbitrary")),
        name="conv_ffn",
    )(x1, g, w_up, w_up, conv_w, conv_w, w_down)


def _ple_kernel(x_ref, p_ref, g_ref, wpg_ref, wpp_ref, fg_ref, out_ref):
    x = x_ref[...]
    hb = (x * _rms_scale(x) * g_ref[...]).astype(BF16)
    gate = jax.nn.sigmoid(jnp.dot(hb, wpg_ref[...], preferred_element_type=F32))
    emb = jnp.dot(p_ref[...].astype(BF16), wpp_ref[...], preferred_element_type=F32)
    y = x + gate * emb
    out_ref[...] = y * _rms_scale(y) * fg_ref[...]


def _ple(x2, p2, g, w_pg, w_pp, final_g, *, tm):
    m = x2.shape[0]
    return pl.pallas_call(
        _ple_kernel,
        grid=(m // tm,),
        in_specs=[
            pl.BlockSpec((tm, D_MODEL), lambda i: (i, 0)),
            pl.BlockSpec((tm, PLE_DIM), lambda i: (i, 0)),
            pl.BlockSpec((1, D_MODEL), lambda i: (0, 0)),
            pl.BlockSpec((D_MODEL, D_MODEL), lambda i: (0, 0)),
            pl.BlockSpec((PLE_DIM, D_MODEL), lambda i: (0, 0)),
            pl.BlockSpec((1, D_MODEL), lambda i: (0, 0)),
        ],
        out_specs=pl.BlockSpec((tm, D_MODEL), lambda i: (i, 0)),
        out_shape=jax.ShapeDtypeStruct((m, D_MODEL), F32),
        compiler_params=_params(("parallel",)),
        name="ple",
    )(x2, p2, g, w_pg, w_pp, final_g)


def _layer(x2, p2, seq, norm_mix_g, w_in, conv_a_w, conv_qkv_w, a_log, dt_bias, dn_norm_g,
           w_out, norm_ffn_g, w_up, conv_ffn_w, w_down, norm_ple_g, w_ple_gate, w_ple_proj,
           out_norm_g):
    row = lambda v: v.reshape(1, -1).astype(F32)
    lane_pad = lambda v: jnp.pad(row(v), ((0, 0), (0, LANES - v.shape[-1])))
    w_main = w_in.astype(BF16)
    w_gate = jnp.pad(w_in[:, PROJ_COLS:], ((0, 0), (0, LANES - 2 * DN_HEADS))).astype(BF16)

    proj, gates = _in_proj(x2, row(norm_mix_g), w_main, w_gate, lane_pad(a_log), lane_pad(dt_bias),
                           tm=1024, tn=1024)
    yb, (w_up_b, w_down_b) = _deltanet(proj, gates, conv_qkv_w.astype(F32), row(dn_norm_g),
                                       (w_up, w_down), seq=seq, ts=512, chunk=DN_CHUNK)
    x2 = _out_proj(proj, yb, x2, conv_a_w.astype(F32), w_out.astype(BF16), tm=512, seq=seq)
    x2 = _ffn(x2, row(norm_ffn_g), w_up_b, conv_ffn_w.astype(F32), w_down_b,
              tm=1024, bn=512, seq=seq)
    return _ple(x2, p2, row(norm_ple_g), w_ple_gate.astype(BF16), w_ple_proj.astype(BF16),
                row(out_norm_g), tm=512)


def kernel(x, p, norm_mix_g, w_in, conv_a_w, conv_qkv_w, a_log, dt_bias, dn_norm_g, w_out,
           norm_ffn_g, w_up, conv_ffn_w, w_down, norm_ple_g, w_ple_gate, w_ple_proj, final_norm_g):
    batch, seq, d_model = x.shape
    depth = p.shape[0]
    assert depth == 1 and d_model == D_MODEL
    x2 = x.reshape(batch * seq, d_model)
    p2 = p[0].reshape(batch * seq, PLE_DIM)
    out = _layer(x2, p2, seq, norm_mix_g[0], w_in[0], conv_a_w[0], conv_qkv_w[0], a_log[0],
                 dt_bias[0], dn_norm_g[0], w_out[0], norm_ffn_g[0], w_up[0], conv_ffn_w[0],
                 w_down[0], norm_ple_g[0], w_ple_gate[0], w_ple_proj[0], final_norm_g)
    return out.reshape(batch, seq, d_model)
```

```python
import functools

import jax
import jax.numpy as jnp
from jax import lax
from jax.experimental import pallas as pl
from jax.experimental.pallas import tpu as pltpu

D_MODEL = 2048
CONV_WIDTH = 1024
CONV_K = 3
DN_HEADS = 8
DN_HEAD_DIM = 128
DN_WIDTH = DN_HEADS * DN_HEAD_DIM
DN_CONV_K = 4
D_FF = 5632
FFN_CONV_K = 3
PLE_DIM = 256
EPS = 1e-6
PROJ_COLS = 3 * CONV_WIDTH + 4 * DN_WIDTH

LANES = 128
SUBLANES_F32 = 8
SUBLANES_BF16 = 16
V7X_VMEM_BYTES = 64 * 1024 * 1024
VMEM_LIMIT_BYTES = V7X_VMEM_BYTES - 2 * 1024 * 1024

IN_TM, IN_TN = 1024, 1024
DN_TS = 256
OUT_TM = 512
FFN_TM, FFN_BN = 1024, 512
PLE_TM = 512

DN_CHUNK = 128
INV_BASE = 8

F32 = jnp.float32
BF16 = jnp.bfloat16


def _mm(a, b):
    return jnp.dot(a.astype(BF16), b.astype(BF16), preferred_element_type=F32)


def _mm_nt(a, b):
    return lax.dot_general(a.astype(BF16), b.astype(BF16), (((1,), (1,)), ((), ())),
                           preferred_element_type=F32)


def _mm_exact(a, b):
    return jnp.dot(a, b, precision=lax.Precision.HIGHEST, preferred_element_type=F32)


def _rms_scale(x):
    return lax.rsqrt(jnp.mean(x * x, axis=-1, keepdims=True) + EPS)


def _silu(x):
    return x * jax.nn.sigmoid(x)


def _params(semantics):
    return pltpu.CompilerParams(dimension_semantics=semantics, vmem_limit_bytes=VMEM_LIMIT_BYTES)


def _in_proj_kernel(x_ref, g_ref, w_ref, wg_ref, alog_ref, dtb_ref, proj_ref, gates_ref, hs_ref):
    @pl.when(pl.program_id(1) == 0)
    def _():
        x = x_ref[...]
        hb = (x * _rms_scale(x) * g_ref[...]).astype(BF16)
        hs_ref[...] = hb
        raw = jnp.dot(hb, wg_ref[...], preferred_element_type=F32)
        a = raw + dtb_ref[...]
        softplus = jnp.maximum(a, 0.0) + jnp.log(1.0 + jnp.exp(-jnp.abs(a)))
        decay = -jnp.exp(alog_ref[...]) * softplus
        lane = lax.broadcasted_iota(jnp.int32, raw.shape, 1)
        gates_ref[...] = jnp.where(lane < DN_HEADS, decay, jax.nn.sigmoid(raw))

    tn = proj_ref.shape[1]
    for block in range(PROJ_COLS // tn):
        @pl.when(pl.program_id(1) == block)
        def _(block=block):
            proj_ref[...] = jnp.dot(hs_ref[...], w_ref[:, pl.ds(block * tn, tn)],
                                    preferred_element_type=F32)


def _in_proj(x2, g, w_main, w_gate, alog_pad, dtb_pad, *, tm, tn):
    m = x2.shape[0]
    return pl.pallas_call(
        _in_proj_kernel,
        grid=(m // tm, PROJ_COLS // tn),
        in_specs=[
            pl.BlockSpec((tm, D_MODEL), lambda i, j: (i, 0)),
            pl.BlockSpec((1, D_MODEL), lambda i, j: (0, 0)),
            pl.BlockSpec(w_main.shape, lambda i, j: (0, 0), pipeline_mode=pl.Buffered(1)),
            pl.BlockSpec((D_MODEL, LANES), lambda i, j: (0, 0)),
            pl.BlockSpec((1, LANES), lambda i, j: (0, 0)),
            pl.BlockSpec((1, LANES), lambda i, j: (0, 0)),
        ],
        out_specs=[
            pl.BlockSpec((tm, tn), lambda i, j: (i, j)),
            pl.BlockSpec((tm, LANES), lambda i, j: (i, 0)),
        ],
        out_shape=[
            jax.ShapeDtypeStruct((m, PROJ_COLS), F32),
            jax.ShapeDtypeStruct((m, LANES), F32),
        ],
        scratch_shapes=[pltpu.VMEM((tm, D_MODEL), BF16)],
        compiler_params=_params(("parallel", "arbitrary")),
        name="in_proj",
    )(x2, g, w_main, w_gate, alog_pad, dtb_pad)


def _each(fn, *lists):
    return [fn(*args) for args in zip(*lists)]


def _unit_lower_inverse(lows, row, col):
    c = lows[0].shape[0]
    eye = (row == col).astype(F32)
    base_shift = INV_BASE.bit_length() - 1
    base_mask = (row >> base_shift) == (col >> base_shift)
    l0 = [jnp.where(base_mask, low, 0.0) for low in lows]
    l2 = _each(_mm, l0, l0)
    l4 = _each(_mm, l2, l2)
    l3 = _each(_mm, l0, l2)
    p1 = _each(lambda a, b, d: eye - a + b - d, l0, l2, l3)
    inv = _each(lambda p, pl4: p + pl4, p1, _each(_mm, p1, l4))
    shift = base_shift
    while (1 << shift) < c:
        off_mask = (((row >> (shift + 1)) == (col >> (shift + 1)))
                    & ((row >> shift) != (col >> shift)))
        off_inv = _each(lambda low, t: _mm(jnp.where(off_mask, low, 0.0), t), lows, inv)
        inv = _each(lambda t, x: t - _mm(t, x), inv, off_inv)
        shift += 1
    return inv


def _deltanet_kernel(q_ref, k_ref, v_ref, z_ref, gates_ref, cw_ref, ng_ref, *refs,
                     ts, chunk, n_cast):
    cast_src = refs[:n_cast]
    out_ref = refs[n_cast]
    cast_dst = refs[n_cast + 1:2 * n_cast + 1]
    qbuf, kbuf, vbuf, state_ref, u_s, wq_s, qk_s, kdt_s = refs[2 * n_cast + 1:]

    halo = SUBLANES_F32
    first = pl.program_id(1) == 0

    @pl.when(first)
    def _():
        state_ref[...] = jnp.zeros_like(state_ref)
        for buf in (qbuf, kbuf, vbuf):
            buf[pl.ds(0, halo), :] = jnp.zeros((halo, DN_WIDTH), F32)

    @pl.when(jnp.logical_not(first))
    def _():
        for buf in (qbuf, kbuf, vbuf):
            buf[pl.ds(0, halo), :] = buf[pl.ds(ts, halo), :]

    for src, buf in ((q_ref, qbuf), (k_ref, kbuf), (v_ref, vbuf)):
        buf[pl.ds(halo, ts), :] = src[...]
    for src, dst in zip(cast_src, cast_dst):
        dst[...] = src[...].astype(dst.dtype)

    row = lax.broadcasted_iota(jnp.int32, (chunk, chunk), 0)
    col = lax.broadcasted_iota(jnp.int32, (chunk, chunk), 1)
    lower_incl = (row >= col).astype(F32)
    causal = row >= col
    strict = row > col
    scale = DN_HEAD_DIM ** -0.5
    heads = range(DN_HEADS)
    chunks = range(ts // chunk)
    head_cols = [slice(h * DN_HEAD_DIM, (h + 1) * DN_HEAD_DIM) for h in heads]
    pairs = [(c, h) for c in chunks for h in heads]
    slot = lambda c, h: c * DN_HEADS + h

    def l2_normalized(x, extra_scale):
        return x * (lax.rsqrt(jnp.sum(x * x, axis=-1, keepdims=True) + EPS) * extra_scale)

    def conv_silu(buf, c, which):
        w = [cw_ref[pl.ds(j, 1), pl.ds(which * DN_WIDTH, DN_WIDTH)] for j in range(DN_CONV_K)]
        cur = buf[pl.ds(c * chunk, chunk + halo), :]
        prev = pltpu.roll(cur, 1, axis=0)
        tail = pltpu.roll(w[1] * cur + w[0] * prev, 2, axis=0)
        return _silu((w[3] * cur + w[2] * prev + tail)[halo:, :])

    qa = [conv_silu(qbuf, c, 0) for c in chunks]
    ka = [conv_silu(kbuf, c, 1) for c in chunks]
    va = [conv_silu(vbuf, c, 2) for c in chunks]
    gts = [gates_ref[pl.ds(c * chunk, chunk), :] for c in chunks]
    gcum = [_mm_exact(lower_incl, g) for g in gts]
    gcum_t = [g.T for g in gcum]
    q = [l2_normalized(qa[c][:, head_cols[h]], scale) for c, h in pairs]
    k = [l2_normalized(ka[c][:, head_cols[h]], 1.0) for c, h in pairs]
    v = [va[c][:, head_cols[h]] for c, h in pairs]
    gc = [gcum[c][:, h:h + 1] for c, h in pairs]
    gr = [gcum_t[c][h:h + 1, :] for c, h in pairs]
    beta = [gts[c][:, DN_HEADS + h:DN_HEADS + h + 1] for c, h in pairs]
    g_last = [g[chunk - 1:chunk, :] for g in gc]
    decay = _each(lambda a, b: jnp.exp(jnp.where(causal, a - b, -1e30)), gc, gr)
    kk = _each(_mm_nt, k, k)
    low = _each(lambda m, d, b: jnp.where(strict, m * d * b, 0.0), kk, decay, beta)
    inv = _unit_lower_inverse(low, row, col)
    eg = _each(jnp.exp, gc)
    rhs = _each(lambda vv, kx, b, e: jnp.concatenate([vv * b, kx * (b * e)], axis=1),
                v, k, beta, eg)
    sol = _each(_mm, inv, rhs)
    qk = _each(lambda a, b, d: _mm_nt(a, b) * d, q, k, decay)
    for (c, h), s, qx, e, a, kx, gl, g in zip(pairs, sol, q, eg, qk, k, g_last, gc):
        i = slot(c, h)
        u_s[i] = s[:, :DN_HEAD_DIM]
        wq_s[i] = jnp.concatenate([s[:, DN_HEAD_DIM:], qx * e], axis=0).astype(BF16)
        qk_s[i] = a.astype(BF16)
        kdt_s[i] = (kx * jnp.exp(gl - g)).T.astype(BF16)
    carry_decay = _each(jnp.exp, g_last)

    state = [state_ref[h] for h in heads]
    for c in chunks:
        ids = [slot(c, h) for h in heads]
        ws_qs = [jnp.dot(wq_s[i], st.astype(BF16), preferred_element_type=F32)
                 for i, st in zip(ids, state)]
        v_new = [(u_s[i] - x[:chunk]).astype(BF16) for i, x in zip(ids, ws_qs)]
        o = [x[chunk:] + jnp.dot(qk_s[i], vn, preferred_element_type=F32)
             for i, x, vn in zip(ids, ws_qs, v_new)]
        state = [st * carry_decay[i] + jnp.dot(kdt_s[i], vn, preferred_element_type=F32)
                 for i, st, vn in zip(ids, state, v_new)]
        rows = pl.ds(c * chunk, chunk)
        for h, sl in enumerate(head_cols):
            y = o[h] * _rms_scale(o[h]) * ng_ref[...] * _silu(z_ref[rows, sl])
            out_ref[rows, sl] = y.astype(out_ref.dtype)
    for h in heads:
        state_ref[h] = state[h]


def _deltanet(proj, gates, conv_w, norm_g, cast_weights, *, seq, ts, chunk):
    m = proj.shape[0]
    tiles = seq // ts
    steps = (m // seq) * tiles
    pairs = (ts // chunk) * DN_HEADS
    col0 = 3 * CONV_WIDTH // DN_WIDTH
    row_map = lambda b, t: b * tiles + t
    qkvz = [pl.BlockSpec((ts, DN_WIDTH), functools.partial(lambda b, t, n: (b * tiles + t, col0 + n), n=n))
            for n in range(4)]
    cast_specs = []
    for w in cast_weights:
        share = 1 if w.shape[0] % (steps * SUBLANES_BF16) == 0 else 2
        assert w.shape[0] % (steps // share * SUBLANES_BF16) == 0, w.shape
        cast_specs.append(pl.BlockSpec(
            (w.shape[0] // (steps // share), w.shape[1]),
            functools.partial(lambda b, t, share: (row_map(b, t) // share, 0), share=share)))
    outs = pl.pallas_call(
        functools.partial(_deltanet_kernel, ts=ts, chunk=chunk, n_cast=len(cast_weights)),
        grid=(m // seq, tiles),
        in_specs=qkvz + [
            pl.BlockSpec((ts, LANES), lambda b, t: (row_map(b, t), 0)),
            pl.BlockSpec((DN_CONV_K, 3 * DN_WIDTH), lambda b, t: (0, 0)),
            pl.BlockSpec((1, DN_HEAD_DIM), lambda b, t: (0, 0)),
        ] + cast_specs,
        out_specs=[pl.BlockSpec((ts, DN_WIDTH), lambda b, t: (row_map(b, t), 0))] + cast_specs,
        out_shape=[jax.ShapeDtypeStruct((m, DN_WIDTH), BF16)]
        + [jax.ShapeDtypeStruct(w.shape, BF16) for w in cast_weights],
        scratch_shapes=[pltpu.VMEM((ts + SUBLANES_F32, DN_WIDTH), F32)] * 3 + [
            pltpu.VMEM((DN_HEADS, DN_HEAD_DIM, DN_HEAD_DIM), F32),
            pltpu.VMEM((pairs, chunk, DN_HEAD_DIM), F32),
            pltpu.VMEM((pairs, 2 * chunk, DN_HEAD_DIM), BF16),
            pltpu.VMEM((pairs, chunk, chunk), BF16),
            pltpu.VMEM((pairs, DN_HEAD_DIM, chunk), BF16),
        ],
        compiler_params=_params(("parallel", "arbitrary")),
        name="deltanet",
    )(proj, proj, proj, proj, gates, conv_w, norm_g, *cast_weights)
    return outs[0], outs[1:]


def _out_proj_kernel(ax_ref, ab_ref, ac_ref, hax_ref, hac_ref, yb_ref, x_ref, cw_ref, w_ref,
                     out_ref, pbuf, *, tm, seq):
    halo = SUBLANES_F32
    acc_b = jnp.dot(yb_ref[...], w_ref[pl.ds(CONV_WIDTH, DN_WIDTH), :], preferred_element_type=F32)
    seq_start = (pl.program_id(0) * tm) % seq == 0
    pbuf[pl.ds(0, halo), :] = jnp.where(seq_start, 0.0, hac_ref[...] * hax_ref[...])
    pbuf[pl.ds(halo, tm), :] = ac_ref[...] * ax_ref[...]
    acc = None
    for j in range(CONV_K):
        term = pbuf[pl.ds(halo - (CONV_K - 1) + j, tm), :] * cw_ref[pl.ds(j, 1), :]
        acc = term if acc is None else acc + term
    ya = (ab_ref[...] * acc).astype(BF16)
    out_ref[...] = (x_ref[...] + acc_b
                    + jnp.dot(ya, w_ref[pl.ds(0, CONV_WIDTH), :], preferred_element_type=F32))


def _out_proj(proj, yb, x2, conv_w, w_out, *, tm, seq):
    m = x2.shape[0]
    halo = SUBLANES_F32
    halo_row = lambda i: jnp.maximum(i * (tm // halo) - 1, 0)
    return pl.pallas_call(
        functools.partial(_out_proj_kernel, tm=tm, seq=seq),
        grid=(m // tm,),
        in_specs=[
            pl.BlockSpec((tm, CONV_WIDTH), lambda i: (i, 0)),
            pl.BlockSpec((tm, CONV_WIDTH), lambda i: (i, 1)),
            pl.BlockSpec((tm, CONV_WIDTH), lambda i: (i, 2)),
            pl.BlockSpec((halo, CONV_WIDTH), lambda i: (halo_row(i), 0)),
            pl.BlockSpec((halo, CONV_WIDTH), lambda i: (halo_row(i), 2)),
            pl.BlockSpec((tm, DN_WIDTH), lambda i: (i, 0)),
            pl.BlockSpec((tm, D_MODEL), lambda i: (i, 0)),
            pl.BlockSpec((CONV_K, CONV_WIDTH), lambda i: (0, 0)),
            pl.BlockSpec((D_MODEL, D_MODEL), lambda i: (0, 0)),
        ],
        out_specs=pl.BlockSpec((tm, D_MODEL), lambda i: (i, 0)),
        out_shape=jax.ShapeDtypeStruct((m, D_MODEL), F32),
        scratch_shapes=[pltpu.VMEM((tm + halo, CONV_WIDTH), F32)],
        compiler_params=_params(("parallel",)),
        name="out_proj",
    )(proj, proj, proj, proj, proj, yb, x2, conv_w, w_out)


def _ffn_kernel(x_ref, g_ref, wg_ref, wv_ref, cg_ref, cv_ref, wd_ref, out_ref,
                hs_ref, ubuf, carry, *, tm, seq):
    halo = SUBLANES_F32
    i = pl.program_id(0)
    j = pl.program_id(1)
    seq_start = (i * tm) % seq == 0

    @pl.when(jnp.logical_and(i == 0, j == 0))
    def _():
        carry[...] = jnp.zeros_like(carry)

    @pl.when(j == 0)
    def _():
        x = x_ref[...]
        hs_ref[...] = (x * _rms_scale(x) * g_ref[...]).astype(BF16)
        out_ref[...] = x

    def up_conv(w_ref, cw_ref, which):
        up = jnp.dot(hs_ref[...], w_ref[...], preferred_element_type=F32)
        ubuf[pl.ds(0, halo), :] = jnp.where(seq_start, 0.0, carry[which, j])
        ubuf[pl.ds(halo, tm), :] = up
        carry[which, j] = up[tm - halo:, :]
        acc = None
        for t in range(FFN_CONV_K):
            term = ubuf[pl.ds(halo - (FFN_CONV_K - 1) + t, tm), :] * cw_ref[pl.ds(t, 1), :]
            acc = term if acc is None else acc + term
        return acc

    gate = up_conv(wg_ref, cg_ref, 0)
    act = _silu(gate)
    val = up_conv(wv_ref, cv_ref, 1)
    act = (act * val).astype(BF16)
    out_ref[...] += jnp.dot(act, wd_ref[...], preferred_element_type=F32)


def _ffn(x1, g, w_up, conv_w, w_down, *, tm, bn, seq):
    m = x1.shape[0]
    halo = SUBLANES_F32
    nj = D_FF // bn
    return pl.pallas_call(
        functools.partial(_ffn_kernel, tm=tm, seq=seq),
        grid=(m // tm, nj),
        in_specs=[
            pl.BlockSpec((tm, D_MODEL), lambda i, j: (i, 0)),
            pl.BlockSpec((1, D_MODEL), lambda i, j: (0, 0)),
            pl.BlockSpec((D_MODEL, bn), lambda i, j: (0, j)),
            pl.BlockSpec((D_MODEL, bn), lambda i, j: (0, j + nj)),
            pl.BlockSpec((FFN_CONV_K, bn), lambda i, j: (0, j)),
            pl.BlockSpec((FFN_CONV_K, bn), lambda i, j: (0, j + nj)),
            pl.BlockSpec((bn, D_MODEL), lambda i, j: (j, 0)),
        ],
        out_specs=pl.BlockSpec((tm, D_MODEL), lambda i, j: (i, 0)),
        out_shape=jax.ShapeDtypeStruct((m, D_MODEL), F32),
        scratch_shapes=[pltpu.VMEM((tm, D_MODEL), BF16),
                        pltpu.VMEM((tm + halo, bn), F32),
                        pltpu.VMEM((2, nj, halo, bn), F32)],
        compiler_params=_params(("arbitrary", "arbitrary")),
        name="conv_ffn",
    )(x1, g, w_up, w_up, conv_w, conv_w, w_down)


def _ple_kernel(x_ref, p_ref, g_ref, wpg_ref, wpp_ref, fg_ref, out_ref):
    x = x_ref[...]
    hb = (x * _rms_scale(x) * g_ref[...]).astype(BF16)
    gate = jax.nn.sigmoid(jnp.dot(hb, wpg_ref[...], preferred_element_type=F32))
    emb = jnp.dot(p_ref[...].astype(BF16), wpp_ref[...], preferred_element_type=F32)
    y = x + gate * emb
    out_ref[...] = y * _rms_scale(y) * fg_ref[...]


def _ple(x2, p2, g, w_pg, w_pp, final_g, *, tm):
    m = x2.shape[0]
    return pl.pallas_call(
        _ple_kernel,
        grid=(m // tm,),
        in_specs=[
            pl.BlockSpec((tm, D_MODEL), lambda i: (i, 0)),
            pl.BlockSpec((tm, PLE_DIM), lambda i: (i, 0)),
            pl.BlockSpec((1, D_MODEL), lambda i: (0, 0)),
            pl.BlockSpec((D_MODEL, D_MODEL), lambda i: (0, 0)),
            pl.BlockSpec((PLE_DIM, D_MODEL), lambda i: (0, 0)),
            pl.BlockSpec((1, D_MODEL), lambda i: (0, 0)),
        ],
        out_specs=pl.BlockSpec((tm, D_MODEL), lambda i: (i, 0)),
        out_shape=jax.ShapeDtypeStruct((m, D_MODEL), F32),
        compiler_params=_params(("parallel",)),
        name="ple",
    )(x2, p2, g, w_pg, w_pp, final_g)


def _layer(x2, p2, seq, norm_mix_g, w_in, conv_a_w, conv_qkv_w, a_log, dt_bias, dn_norm_g,
           w_out, norm_ffn_g, w_up, conv_ffn_w, w_down, norm_ple_g, w_ple_gate, w_ple_proj,
           out_norm_g):
    row = lambda v: v.reshape(1, -1).astype(F32)
    lane_pad = lambda v: jnp.pad(row(v), ((0, 0), (0, LANES - v.shape[-1])))
    w_main = w_in.astype(BF16)
    w_gate = jnp.pad(w_in[:, PROJ_COLS:], ((0, 0), (0, LANES - 2 * DN_HEADS))).astype(BF16)

    proj, gates = _in_proj(x2, row(norm_mix_g), w_main, w_gate, lane_pad(a_log), lane_pad(dt_bias),
                           tm=IN_TM, tn=IN_TN)
    yb, (w_out_b, w_up_b, w_down_b, w_pg_b) = _deltanet(
        proj, gates, conv_qkv_w.astype(F32), row(dn_norm_g), (w_out, w_up, w_down, w_ple_gate),
        seq=seq, ts=DN_TS, chunk=DN_CHUNK)
    x2 = _out_proj(proj, yb, x2, conv_a_w.astype(F32), w_out_b, tm=OUT_TM, seq=seq)
    x2 = _ffn(x2, row(norm_ffn_g), w_up_b, conv_ffn_w.astype(F32), w_down_b,
              tm=FFN_TM, bn=FFN_BN, seq=seq)
    return _ple(x2, p2, row(norm_ple_g), w_pg_b, w_ple_proj.astype(BF16),
                row(out_norm_g), tm=PLE_TM)


def kernel(x, p, norm_mix_g, w_in, conv_a_w, conv_qkv_w, a_log, dt_bias, dn_norm_g, w_out,
           norm_ffn_g, w_up, conv_ffn_w, w_down, norm_ple_g, w_ple_gate, w_ple_proj, final_norm_g):
    batch, seq, d_model = x.shape
    depth = p.shape[0]
    assert depth == 1 and d_model == D_MODEL
    x2 = x.reshape(batch * seq, d_model)
    p2 = p[0].reshape(batch * seq, PLE_DIM)
    out = _layer(x2, p2, seq, norm_mix_g[0], w_in[0], conv_a_w[0], conv_qkv_w[0], a_log[0],
                 dt_bias[0], dn_norm_g[0], w_out[0], norm_ffn_g[0], w_up[0], conv_ffn_w[0],
                 w_down[0], norm_ple_g[0], w_ple_gate[0], w_ple_proj[0], final_norm_g)
    return out.reshape(batch, seq, d_model)
```

```python
import functools

import jax
import jax.numpy as jnp
from jax import lax
from jax.experimental import pallas as pl
from jax.experimental.pallas import tpu as pltpu

D_MODEL = 2048
CONV_WIDTH = 1024
CONV_K = 3
DN_HEADS = 8
DN_HEAD_DIM = 128
DN_WIDTH = DN_HEADS * DN_HEAD_DIM
DN_CONV_K = 4
D_FF = 5632
FFN_CONV_K = 3
PLE_DIM = 256
EPS = 1e-6
PROJ_COLS = 3 * CONV_WIDTH + 4 * DN_WIDTH

LANES = 128
SUBLANES_F32 = 8
SUBLANES_BF16 = 16
V7X_VMEM_BYTES = 64 * 1024 * 1024
VMEM_LIMIT_BYTES = V7X_VMEM_BYTES - 2 * 1024 * 1024

IN_TM, IN_TN = 1024, 1024
DN_TS = 256
OUT_TM = 512
FFN_TM, FFN_BN = 1024, 512
PLE_TM = 512

DN_CHUNK = 128
INV_BASE = 8

F32 = jnp.float32
BF16 = jnp.bfloat16


def _mm(a, b):
    return jnp.dot(a.astype(BF16), b.astype(BF16), preferred_element_type=F32)


def _mm_nt(a, b):
    return lax.dot_general(a.astype(BF16), b.astype(BF16), (((1,), (1,)), ((), ())),
                           preferred_element_type=F32)


def _mm_exact(a, b):
    return jnp.dot(a, b, precision=lax.Precision.HIGHEST, preferred_element_type=F32)


def _rms_scale(x):
    return lax.rsqrt(jnp.mean(x * x, axis=-1, keepdims=True) + EPS)


def _silu(x):
    return x * jax.nn.sigmoid(x)


def _params(semantics):
    return pltpu.CompilerParams(dimension_semantics=semantics, vmem_limit_bytes=VMEM_LIMIT_BYTES)


def _in_proj_kernel(x_ref, g_ref, w_ref, wg_ref, alog_ref, dtb_ref, proj_ref, gates_ref, hs_ref):
    @pl.when(pl.program_id(1) == 0)
    def _():
        x = x_ref[...]
        hb = (x * _rms_scale(x) * g_ref[...]).astype(BF16)
        hs_ref[...] = hb
        raw = jnp.dot(hb, wg_ref[...], preferred_element_type=F32)
        a = raw + dtb_ref[...]
        softplus = jnp.maximum(a, 0.0) + jnp.log(1.0 + jnp.exp(-jnp.abs(a)))
        decay = -jnp.exp(alog_ref[...]) * softplus
        lane = lax.broadcasted_iota(jnp.int32, raw.shape, 1)
        gates_ref[...] = jnp.where(lane < DN_HEADS, decay, jax.nn.sigmoid(raw))

    tn = proj_ref.shape[1]
    for block in range(PROJ_COLS // tn):
        @pl.when(pl.program_id(1) == block)
        def _(block=block):
            proj_ref[...] = jnp.dot(hs_ref[...], w_ref[:, pl.ds(block * tn, tn)],
                                    preferred_element_type=F32)


def _in_proj(x2, g, w_main, w_gate, alog_pad, dtb_pad, *, tm, tn):
    m = x2.shape[0]
    return pl.pallas_call(
        _in_proj_kernel,
        grid=(m // tm, PROJ_COLS // tn),
        in_specs=[
            pl.BlockSpec((tm, D_MODEL), lambda i, j: (i, 0)),
            pl.BlockSpec((1, D_MODEL), lambda i, j: (0, 0)),
            pl.BlockSpec(w_main.shape, lambda i, j: (0, 0), pipeline_mode=pl.Buffered(1)),
            pl.BlockSpec((D_MODEL, LANES), lambda i, j: (0, 0)),
            pl.BlockSpec((1, LANES), lambda i, j: (0, 0)),
            pl.BlockSpec((1, LANES), lambda i, j: (0, 0)),
        ],
        out_specs=[
            pl.BlockSpec((tm, tn), lambda i, j: (i, j)),
            pl.BlockSpec((tm, LANES), lambda i, j: (i, 0)),
        ],
        out_shape=[
            jax.ShapeDtypeStruct((m, PROJ_COLS), F32),
            jax.ShapeDtypeStruct((m, LANES), F32),
        ],
        scratch_shapes=[pltpu.VMEM((tm, D_MODEL), BF16)],
        compiler_params=_params(("parallel", "arbitrary")),
        name="in_proj",
    )(x2, g, w_main, w_gate, alog_pad, dtb_pad)


def _each(fn, *lists):
    return [fn(*args) for args in zip(*lists)]


def _unit_lower_inverse(lows, row, col):
    c = lows[0].shape[0]
    eye = (row == col).astype(F32)
    base_shift = INV_BASE.bit_length() - 1
    base_mask = (row >> base_shift) == (col >> base_shift)
    l0 = [jnp.where(base_mask, low, 0.0) for low in lows]
    l2 = _each(_mm, l0, l0)
    l4 = _each(_mm, l2, l2)
    l3 = _each(_mm, l0, l2)
    p1 = _each(lambda a, b, d: eye - a + b - d, l0, l2, l3)
    inv = _each(lambda p, pl4: p + pl4, p1, _each(_mm, p1, l4))
    shift = base_shift
    while (1 << shift) < c:
        off_mask = (((row >> (shift + 1)) == (col >> (shift + 1)))
                    & ((row >> shift) != (col >> shift)))
        off_inv = _each(lambda low, t: _mm(jnp.where(off_mask, low, 0.0), t), lows, inv)
        inv = _each(lambda t, x: t - _mm(t, x), inv, off_inv)
        shift += 1
    return inv


def _deltanet_kernel(q_ref, k_ref, v_ref, z_ref, gates_ref, cw_ref, ng_ref, *refs,
                     ts, chunk, n_cast):
    cast_src = refs[:n_cast]
    out_ref = refs[n_cast]
    cast_dst = refs[n_cast + 1:2 * n_cast + 1]
    qbuf, kbuf, vbuf, state_ref, u_s, wq_s, qk_s, kdt_s = refs[2 * n_cast + 1:]

    halo = SUBLANES_F32
    first = pl.program_id(1) == 0

    @pl.when(first)
    def _():
        state_ref[...] = jnp.zeros_like(state_ref)
        for buf in (qbuf, kbuf, vbuf):
            buf[pl.ds(0, halo), :] = jnp.zeros((halo, DN_WIDTH), F32)

    @pl.when(jnp.logical_not(first))
    def _():
        for buf in (qbuf, kbuf, vbuf):
            buf[pl.ds(0, halo), :] = buf[pl.ds(ts, halo), :]

    for src, buf in ((q_ref, qbuf), (k_ref, kbuf), (v_ref, vbuf)):
        buf[pl.ds(halo, ts), :] = src[...]
    for src, dst in zip(cast_src, cast_dst):
        dst[...] = src[...].astype(dst.dtype)

    row = lax.broadcasted_iota(jnp.int32, (chunk, chunk), 0)
    col = lax.broadcasted_iota(jnp.int32, (chunk, chunk), 1)
    lower_incl = (row >= col).astype(F32)
    causal = row >= col
    strict = row > col
    scale = DN_HEAD_DIM ** -0.5
    heads = range(DN_HEADS)
    chunks = range(ts // chunk)
    head_cols = [slice(h * DN_HEAD_DIM, (h + 1) * DN_HEAD_DIM) for h in heads]
    pairs = [(c, h) for c in chunks for h in heads]
    slot = lambda c, h: c * DN_HEADS + h

    def l2_normalized(x, extra_scale):
        return x * (lax.rsqrt(jnp.sum(x * x, axis=-1, keepdims=True) + EPS) * extra_scale)

    def conv_silu(buf, c, which):
        w = [cw_ref[pl.ds(j, 1), pl.ds(which * DN_WIDTH, DN_WIDTH)] for j in range(DN_CONV_K)]
        cur = buf[pl.ds(c * chunk, chunk + halo), :]
        prev = pltpu.roll(cur, 1, axis=0)
        tail = pltpu.roll(w[1] * cur + w[0] * prev, 2, axis=0)
        return _silu((w[3] * cur + w[2] * prev + tail)[halo:, :])

    qa = [conv_silu(qbuf, c, 0) for c in chunks]
    ka = [conv_silu(kbuf, c, 1) for c in chunks]
    va = [conv_silu(vbuf, c, 2) for c in chunks]
    gts = [gates_ref[pl.ds(c * chunk, chunk), :] for c in chunks]
    gcum = [_mm_exact(lower_incl, g) for g in gts]
    gcum_t = [g.T for g in gcum]
    q = [l2_normalized(qa[c][:, head_cols[h]], scale) for c, h in pairs]
    k = [l2_normalized(ka[c][:, head_cols[h]], 1.0) for c, h in pairs]
    v = [va[c][:, head_cols[h]] for c, h in pairs]
    gc = [gcum[c][:, h:h + 1] for c, h in pairs]
    gr = [gcum_t[c][h:h + 1, :] for c, h in pairs]
    beta = [gts[c][:, DN_HEADS + h:DN_HEADS + h + 1] for c, h in pairs]
    g_last = [g[chunk - 1:chunk, :] for g in gc]
    decay = _each(lambda a, b: jnp.exp(jnp.where(causal, a - b, -1e30)), gc, gr)
    kk = _each(_mm_nt, k, k)
    low = _each(lambda m, d, b: jnp.where(strict, m * d * b, 0.0), kk, decay, beta)
    inv = _unit_lower_inverse(low, row, col)
    eg = _each(jnp.exp, gc)
    rhs = _each(lambda vv, kx, b, e: jnp.concatenate([vv * b, kx * (b * e)], axis=1),
                v, k, beta, eg)
    sol = _each(_mm, inv, rhs)
    qk = _each(lambda a, b, d: _mm_nt(a, b) * d, q, k, decay)
    for (c, h), s, qx, e, a, kx, gl, g in zip(pairs, sol, q, eg, qk, k, g_last, gc):
        i = slot(c, h)
        u_s[i] = s[:, :DN_HEAD_DIM]
        wq_s[i] = jnp.concatenate([s[:, DN_HEAD_DIM:], qx * e], axis=0).astype(BF16)
        qk_s[i] = a.astype(BF16)
        kdt_s[i] = (kx * jnp.exp(gl - g)).T.astype(BF16)
    carry_decay = _each(jnp.exp, g_last)

    state = [state_ref[h] for h in heads]
    for c in chunks:
        ids = [slot(c, h) for h in heads]
        ws_qs = [jnp.dot(wq_s[i], st.astype(BF16), preferred_element_type=F32)
                 for i, st in zip(ids, state)]
        v_new = [(u_s[i] - x[:chunk]).astype(BF16) for i, x in zip(ids, ws_qs)]
        o = [x[chunk:] + jnp.dot(qk_s[i], vn, preferred_element_type=F32)
             for i, x, vn in zip(ids, ws_qs, v_new)]
        state = [st * carry_decay[i] + jnp.dot(kdt_s[i], vn, preferred_element_type=F32)
                 for i, st, vn in zip(ids, state, v_new)]
        rows = pl.ds(c * chunk, chunk)
        for h, sl in enumerate(head_cols):
            y = o[h] * _rms_scale(o[h]) * ng_ref[...] * _silu(z_ref[rows, sl])
            out_ref[rows, sl] = y.astype(out_ref.dtype)
    for h in heads:
        state_ref[h] = state[h]


def _deltanet(proj, gates, conv_w, norm_g, cast_weights, *, seq, ts, chunk):
    m = proj.shape[0]
    tiles = seq // ts
    steps = (m // seq) * tiles
    pairs = (ts // chunk) * DN_HEADS
    col0 = 3 * CONV_WIDTH // DN_WIDTH
    row_map = lambda b, t: b * tiles + t
    qkvz = [pl.BlockSpec((ts, DN_WIDTH), functools.partial(lambda b, t, n: (b * tiles + t, col0 + n), n=n))
            for n in range(4)]
    cast_specs = []
    for w in cast_weights:
        share = 1 if w.shape[0] % (steps * SUBLANES_BF16) == 0 else 2
        assert w.shape[0] % (steps // share * SUBLANES_BF16) == 0, w.shape
        cast_specs.append(pl.BlockSpec(
            (w.shape[0] // (steps // share), w.shape[1]),
            functools.partial(lambda b, t, share: (row_map(b, t) // share, 0), share=share)))
    outs = pl.pallas_call(
        functools.partial(_deltanet_kernel, ts=ts, chunk=chunk, n_cast=len(cast_weights)),
        grid=(m // seq, tiles),
        in_specs=qkvz + [
            pl.BlockSpec((ts, LANES), lambda b, t: (row_map(b, t), 0)),
            pl.BlockSpec((DN_CONV_K, 3 * DN_WIDTH), lambda b, t: (0, 0)),
            pl.BlockSpec((1, DN_HEAD_DIM), lambda b, t: (0, 0)),
        ] + cast_specs,
        out_specs=[pl.BlockSpec((ts, DN_WIDTH), lambda b, t: (row_map(b, t), 0))] + cast_specs,
        out_shape=[jax.ShapeDtypeStruct((m, DN_WIDTH), BF16)]
        + [jax.ShapeDtypeStruct(w.shape, BF16) for w in cast_weights],
        scratch_shapes=[pltpu.VMEM((ts + SUBLANES_F32, DN_WIDTH), F32)] * 3 + [
            pltpu.VMEM((DN_HEADS, DN_HEAD_DIM, DN_HEAD_DIM), F32),
            pltpu.VMEM((pairs, chunk, DN_HEAD_DIM), F32),
            pltpu.VMEM((pairs, 2 * chunk, DN_HEAD_DIM), BF16),
            pltpu.VMEM((pairs, chunk, chunk), BF16),
            pltpu.VMEM((pairs, DN_HEAD_DIM, chunk), BF16),
        ],
        compiler_params=_params(("parallel", "arbitrary")),
        name="deltanet",
    )(proj, proj, proj, proj, gates, conv_w, norm_g, *cast_weights)
    return outs[0], outs[1:]


def _out_proj_kernel(ax_ref, ab_ref, ac_ref, hax_ref, hac_ref, yb_ref, x_ref, cw_ref, w_ref,
                     out_ref, pbuf, *, tm, seq):
    halo = SUBLANES_F32
    acc_b = jnp.dot(yb_ref[...], w_ref[pl.ds(CONV_WIDTH, DN_WIDTH), :], preferred_element_type=F32)
    seq_start = (pl.program_id(0) * tm) % seq == 0
    pbuf[pl.ds(0, halo), :] = jnp.where(seq_start, 0.0, hac_ref[...] * hax_ref[...])
    pbuf[pl.ds(halo, tm), :] = ac_ref[...] * ax_ref[...]
    acc = None
    for j in range(CONV_K):
        term = pbuf[pl.ds(halo - (CONV_K - 1) + j, tm), :] * cw_ref[pl.ds(j, 1), :]
        acc = term if acc is None else acc + term
    ya = (ab_ref[...] * acc).astype(BF16)
    out_ref[...] = (x_ref[...] + acc_b
                    + jnp.dot(ya, w_ref[pl.ds(0, CONV_WIDTH), :], preferred_element_type=F32))


def _out_proj(proj, yb, x2, conv_w, w_out, *, tm, seq):
    m = x2.shape[0]
    halo = SUBLANES_F32
    halo_row = lambda i: jnp.maximum(i * (tm // halo) - 1, 0)
    return pl.pallas_call(
        functools.partial(_out_proj_kernel, tm=tm, seq=seq),
        grid=(m // tm,),
        in_specs=[
            pl.BlockSpec((tm, CONV_WIDTH), lambda i: (i, 0)),
            pl.BlockSpec((tm, CONV_WIDTH), lambda i: (i, 1)),
            pl.BlockSpec((tm, CONV_WIDTH), lambda i: (i, 2)),
            pl.BlockSpec((halo, CONV_WIDTH), lambda i: (halo_row(i), 0)),
            pl.BlockSpec((halo, CONV_WIDTH), lambda i: (halo_row(i), 2)),
            pl.BlockSpec((tm, DN_WIDTH), lambda i: (i, 0)),
            pl.BlockSpec((tm, D_MODEL), lambda i: (i, 0)),
            pl.BlockSpec((CONV_K, CONV_WIDTH), lambda i: (0, 0)),
            pl.BlockSpec((D_MODEL, D_MODEL), lambda i: (0, 0)),
        ],
        out_specs=pl.BlockSpec((tm, D_MODEL), lambda i: (i, 0)),
        out_shape=jax.ShapeDtypeStruct((m, D_MODEL), F32),
        scratch_shapes=[pltpu.VMEM((tm + halo, CONV_WIDTH), F32)],
        compiler_params=_params(("parallel",)),
        name="out_proj",
    )(proj, proj, proj, proj, proj, yb, x2, conv_w, w_out)


def _ffn_kernel(x_ref, g_ref, wg_ref, wv_ref, cg_ref, cv_ref, wd_ref, out_ref,
                hs_ref, ubuf, carry, *, tm, seq):
    halo = SUBLANES_F32
    i = pl.program_id(0)
    j = pl.program_id(1)
    seq_start = (i * tm) % seq == 0

    @pl.when(jnp.logical_and(i == 0, j == 0))
    def _():
        carry[...] = jnp.zeros_like(carry)

    @pl.when(j == 0)
    def _():
        x = x_ref[...]
        hs_ref[...] = (x * _rms_scale(x) * g_ref[...]).astype(BF16)
        out_ref[...] = x

    def up_conv(w_ref, cw_ref, which, cs):
        up = jnp.dot(hs_ref[...], w_ref[:, cs], preferred_element_type=F32)
        ubuf[which, pl.ds(0, halo), cs] = jnp.where(seq_start, 0.0, carry[which, j, :, cs])
        ubuf[which, pl.ds(halo, tm), cs] = up
        carry[which, j, :, cs] = up[tm - halo:, :]
        acc = None
        for t in range(FFN_CONV_K):
            term = (ubuf[which, pl.ds(halo - (FFN_CONV_K - 1) + t, tm), cs]
                    * cw_ref[pl.ds(t, 1), cs])
            acc = term if acc is None else acc + term
        return acc

    bn = wd_ref.shape[0]
    halves = [pl.ds(h * (bn // 2), bn // 2) for h in range(2)]
    acts = []
    for cs in halves:
        gate = up_conv(wg_ref, cg_ref, 0, cs)
        val = up_conv(wv_ref, cv_ref, 1, cs)
        acts.append((_silu(gate) * val).astype(BF16))
    for cs, act in zip(halves, acts):
        out_ref[...] += jnp.dot(act, wd_ref[cs, :], preferred_element_type=F32)


def _ffn(x1, g, w_up, conv_w, w_down, *, tm, bn, seq):
    m = x1.shape[0]
    halo = SUBLANES_F32
    nj = D_FF // bn
    return pl.pallas_call(
        functools.partial(_ffn_kernel, tm=tm, seq=seq),
        grid=(m // tm, nj),
        in_specs=[
            pl.BlockSpec((tm, D_MODEL), lambda i, j: (i, 0)),
            pl.BlockSpec((1, D_MODEL), lambda i, j: (0, 0)),
            pl.BlockSpec((D_MODEL, bn), lambda i, j: (0, j)),
            pl.BlockSpec((D_MODEL, bn), lambda i, j: (0, j + nj)),
            pl.BlockSpec((FFN_CONV_K, bn), lambda i, j: (0, j)),
            pl.BlockSpec((FFN_CONV_K, bn), lambda i, j: (0, j + nj)),
            pl.BlockSpec((bn, D_MODEL), lambda i, j: (j, 0)),
        ],
        out_specs=pl.BlockSpec((tm, D_MODEL), lambda i, j: (i, 0)),
        out_shape=jax.ShapeDtypeStruct((m, D_MODEL), F32),
        scratch_shapes=[pltpu.VMEM((tm, D_MODEL), BF16),
                        pltpu.VMEM((2, tm + halo, bn), F32),
                        pltpu.VMEM((2, nj, halo, bn), F32)],
        compiler_params=_params(("arbitrary", "arbitrary")),
        name="conv_ffn",
    )(x1, g, w_up, w_up, conv_w, conv_w, w_down)


def _ple_kernel(x_ref, p_ref, g_ref, wpg_ref, wpp_ref, fg_ref, out_ref):
    x = x_ref[...]
    hb = (x * _rms_scale(x) * g_ref[...]).astype(BF16)
    gate = jax.nn.sigmoid(jnp.dot(hb, wpg_ref[...], preferred_element_type=F32))
    emb = jnp.dot(p_ref[...].astype(BF16), wpp_ref[...], preferred_element_type=F32)
    y = x + gate * emb
    out_ref[...] = y * _rms_scale(y) * fg_ref[...]


def _ple(x2, p2, g, w_pg, w_pp, final_g, *, tm):
    m = x2.shape[0]
    return pl.pallas_call(
        _ple_kernel,
        grid=(m // tm,),
        in_specs=[
            pl.BlockSpec((tm, D_MODEL), lambda i: (i, 0)),
            pl.BlockSpec((tm, PLE_DIM), lambda i: (i, 0)),
            pl.BlockSpec((1, D_MODEL), lambda i: (0, 0)),
            pl.BlockSpec((D_MODEL, D_MODEL), lambda i: (0, 0)),
            pl.BlockSpec((PLE_DIM, D_MODEL), lambda i: (0, 0)),
            pl.BlockSpec((1, D_MODEL), lambda i: (0, 0)),
        ],
        out_specs=pl.BlockSpec((tm, D_MODEL), lambda i: (i, 0)),
        out_shape=jax.ShapeDtypeStruct((m, D_MODEL), F32),
        compiler_params=_params(("parallel",)),
        name="ple",
    )(x2, p2, g, w_pg, w_pp, final_g)


def _layer(x2, p2, seq, norm_mix_g, w_in, conv_a_w, conv_qkv_w, a_log, dt_bias, dn_norm_g,
           w_out, norm_ffn_g, w_up, conv_ffn_w, w_down, norm_ple_g, w_ple_gate, w_ple_proj,
           out_norm_g):
    row = lambda v: v.reshape(1, -1).astype(F32)
    lane_pad = lambda v: jnp.pad(row(v), ((0, 0), (0, LANES - v.shape[-1])))
    w_main = w_in.astype(BF16)
    w_gate = jnp.pad(w_in[:, PROJ_COLS:], ((0, 0), (0, LANES - 2 * DN_HEADS))).astype(BF16)

    proj, gates = _in_proj(x2, row(norm_mix_g), w_main, w_gate, lane_pad(a_log), lane_pad(dt_bias),
                           tm=IN_TM, tn=IN_TN)
    yb, (w_out_b, w_up_b, w_down_b, w_pg_b) = _deltanet(
        proj, gates, conv_qkv_w.astype(F32), row(dn_norm_g), (w_out, w_up, w_down, w_ple_gate),
        seq=seq, ts=DN_TS, chunk=DN_CHUNK)
    x2 = _out_proj(proj, yb, x2, conv_a_w.astype(F32), w_out_b, tm=OUT_TM, seq=seq)
    x2 = _ffn(x2, row(norm_ffn_g), w_up_b, conv_ffn_w.astype(F32), w_down_b,
              tm=FFN_TM, bn=FFN_BN, seq=seq)
    return _ple(x2, p2, row(norm_ple_g), w_pg_b, w_ple_proj.astype(BF16),
                row(out_norm_g), tm=PLE_TM)


def kernel(x, p, norm_mix_g, w_in, conv_a_w, conv_qkv_w, a_log, dt_bias, dn_norm_g, w_out,
           norm_ffn_g, w_up, conv_ffn_w, w_down, norm_ple_g, w_ple_gate, w_ple_proj, final_norm_g):
    batch, seq, d_model = x.shape
    depth = p.shape[0]
    assert depth == 1 and d_model == D_MODEL
    x2 = x.reshape(batch * seq, d_model)
    p2 = p[0].reshape(batch * seq, PLE_DIM)
    out = _layer(x2, p2, seq, norm_mix_g[0], w_in[0], conv_a_w[0], conv_qkv_w[0], a_log[0],
                 dt_bias[0], dn_norm_g[0], w_out[0], norm_ffn_g[0], w_up[0], conv_ffn_w[0],
                 w_down[0], norm_ple_g[0], w_ple_gate[0], w_ple_proj[0], final_norm_g)
    return out.reshape(batch, seq, d_model)
```

```python
import functools

import jax
import jax.numpy as jnp
from jax import lax
from jax.experimental import pallas as pl
from jax.experimental.pallas import tpu as pltpu

D_MODEL = 2048
CONV_WIDTH = 1024
CONV_K = 3
DN_HEADS = 8
DN_HEAD_DIM = 128
DN_WIDTH = DN_HEADS * DN_HEAD_DIM
DN_CONV_K = 4
D_FF = 5632
FFN_CONV_K = 3
PLE_DIM = 256
EPS = 1e-6
PROJ_COLS = 3 * CONV_WIDTH + 4 * DN_WIDTH

LANES = 128
SUBLANES_F32 = 8
SUBLANES_BF16 = 16
V7X_VMEM_BYTES = 64 * 1024 * 1024
VMEM_LIMIT_BYTES = V7X_VMEM_BYTES - 2 * 1024 * 1024

IN_TM, IN_TN = 1024, 1024
DN_TS = 256
OUT_TM = 512
FFN_TM, FFN_BN = 1024, 512
PLE_TM = 512

DN_CHUNK = 128
INV_BASE = 8

F32 = jnp.float32
BF16 = jnp.bfloat16


def _mm(a, b):
    return jnp.dot(a.astype(BF16), b.astype(BF16), preferred_element_type=F32)


def _mm_nt(a, b):
    return lax.dot_general(a.astype(BF16), b.astype(BF16), (((1,), (1,)), ((), ())),
                           preferred_element_type=F32)


def _mm_exact(a, b):
    return jnp.dot(a, b, precision=lax.Precision.HIGHEST, preferred_element_type=F32)


def _rms_scale(x):
    return lax.rsqrt(jnp.mean(x * x, axis=-1, keepdims=True) + EPS)


def _silu(x):
    return x * jax.nn.sigmoid(x)


def _params(semantics):
    return pltpu.CompilerParams(dimension_semantics=semantics, vmem_limit_bytes=VMEM_LIMIT_BYTES)


def _norm_and_gates(x_ref, g_ref, wg_ref, alog_ref, dtb_ref, gates_ref, hs_ref):
    x = x_ref[...]
    hb = (x * _rms_scale(x) * g_ref[...]).astype(BF16)
    hs_ref[...] = hb
    raw = jnp.dot(hb, wg_ref[...], preferred_element_type=F32)
    a = raw + dtb_ref[...]
    softplus = jnp.maximum(a, 0.0) + jnp.log(1.0 + jnp.exp(-jnp.abs(a)))
    decay = -jnp.exp(alog_ref[...]) * softplus
    lane = lax.broadcasted_iota(jnp.int32, raw.shape, 1)
    gates_ref[...] = jnp.where(lane < DN_HEADS, decay, jax.nn.sigmoid(raw))


def _in_proj_head_kernel(x_ref, g_ref, w_ref, wg_ref, alog_ref, dtb_ref,
                         proj_ref, gates_ref, wb_ref, hs_ref):
    @pl.when(pl.program_id(0) == 0)
    def _():
        _norm_and_gates(x_ref, g_ref, wg_ref, alog_ref, dtb_ref, gates_ref, hs_ref)

    wb = w_ref[...].astype(BF16)
    wb_ref[...] = wb
    proj_ref[...] = jnp.dot(hs_ref[...], wb, preferred_element_type=F32)


def _in_proj_kernel(x_ref, g_ref, w_ref, wg_ref, alog_ref, dtb_ref, proj_in, gates_in,
                    proj_ref, gates_ref, hs_ref):
    del proj_in, gates_in
    @pl.when(pl.program_id(1) == 0)
    def _():
        _norm_and_gates(x_ref, g_ref, wg_ref, alog_ref, dtb_ref, gates_ref, hs_ref)

    tn = proj_ref.shape[1]
    for block in range(PROJ_COLS // tn):
        @pl.when(pl.program_id(1) == block)
        def _(block=block):
            proj_ref[...] = jnp.dot(hs_ref[...], w_ref[:, pl.ds(block * tn, tn)],
                                    preferred_element_type=F32)


def _in_proj(x2, g, w_in, w_gate, alog_pad, dtb_pad, *, tm, tn):
    m = x2.shape[0]
    n_blocks = PROJ_COLS // tn
    small = [
        pl.BlockSpec((D_MODEL, LANES), lambda *_: (0, 0)),
        pl.BlockSpec((1, LANES), lambda *_: (0, 0)),
        pl.BlockSpec((1, LANES), lambda *_: (0, 0)),
    ]
    out_shape = [jax.ShapeDtypeStruct((m, PROJ_COLS), F32), jax.ShapeDtypeStruct((m, LANES), F32)]
    proj, gates, w_bf16 = pl.pallas_call(
        _in_proj_head_kernel,
        grid=(n_blocks,),
        in_specs=[
            pl.BlockSpec((tm, D_MODEL), lambda j: (0, 0)),
            pl.BlockSpec((1, D_MODEL), lambda j: (0, 0)),
            pl.BlockSpec((D_MODEL, tn), lambda j: (0, j)),
        ] + small,
        out_specs=[
            pl.BlockSpec((tm, tn), lambda j: (0, j)),
            pl.BlockSpec((tm, LANES), lambda j: (0, 0)),
            pl.BlockSpec((D_MODEL, tn), lambda j: (0, j)),
        ],
        out_shape=out_shape + [jax.ShapeDtypeStruct((D_MODEL, PROJ_COLS), BF16)],
        scratch_shapes=[pltpu.VMEM((tm, D_MODEL), BF16)],
        compiler_params=_params(("arbitrary",)),
        name="in_proj_head",
    )(x2, g, w_in, w_gate, alog_pad, dtb_pad)
    return pl.pallas_call(
        _in_proj_kernel,
        grid=(m // tm - 1, n_blocks),
        in_specs=[
            pl.BlockSpec((tm, D_MODEL), lambda i, j: (i + 1, 0)),
            pl.BlockSpec((1, D_MODEL), lambda i, j: (0, 0)),
            pl.BlockSpec(w_bf16.shape, lambda i, j: (0, 0), pipeline_mode=pl.Buffered(1)),
        ] + small + [
            pl.BlockSpec(memory_space=pl.ANY),
            pl.BlockSpec(memory_space=pl.ANY),
        ],
        out_specs=[
            pl.BlockSpec((tm, tn), lambda i, j: (i + 1, j)),
            pl.BlockSpec((tm, LANES), lambda i, j: (i + 1, 0)),
        ],
        out_shape=out_shape,
        input_output_aliases={6: 0, 7: 1},
        scratch_shapes=[pltpu.VMEM((tm, D_MODEL), BF16)],
        compiler_params=_params(("parallel", "arbitrary")),
        name="in_proj",
    )(x2, g, w_bf16, w_gate, alog_pad, dtb_pad, proj, gates)


def _each(fn, *lists):
    return [fn(*args) for args in zip(*lists)]


def _unit_lower_inverse(lows, row, col):
    c = lows[0].shape[0]
    eye = (row == col).astype(F32)
    base_shift = INV_BASE.bit_length() - 1
    base_mask = (row >> base_shift) == (col >> base_shift)
    l0 = [jnp.where(base_mask, low, 0.0) for low in lows]
    l2 = _each(_mm, l0, l0)
    l4 = _each(_mm, l2, l2)
    l3 = _each(_mm, l0, l2)
    p1 = _each(lambda a, b, d: eye - a + b - d, l0, l2, l3)
    inv = _each(lambda p, pl4: p + pl4, p1, _each(_mm, p1, l4))
    shift = base_shift
    while (1 << shift) < c:
        off_mask = (((row >> (shift + 1)) == (col >> (shift + 1)))
                    & ((row >> shift) != (col >> shift)))
        off_inv = _each(lambda low, t: _mm(jnp.where(off_mask, low, 0.0), t), lows, inv)
        inv = _each(lambda t, x: t - _mm(t, x), inv, off_inv)
        shift += 1
    return inv


def _deltanet_kernel(q_ref, k_ref, v_ref, z_ref, gates_ref, cw_ref, ng_ref, *refs,
                     ts, chunk, n_cast):
    cast_src = refs[:n_cast]
    out_ref = refs[n_cast]
    cast_dst = refs[n_cast + 1:2 * n_cast + 1]
    qbuf, kbuf, vbuf, state_ref, u_s, wq_s, qk_s, kdt_s = refs[2 * n_cast + 1:]

    halo = SUBLANES_F32
    first = pl.program_id(1) == 0

    @pl.when(first)
    def _():
        state_ref[...] = jnp.zeros_like(state_ref)
        for buf in (qbuf, kbuf, vbuf):
            buf[pl.ds(0, halo), :] = jnp.zeros((halo, DN_WIDTH), F32)

    @pl.when(jnp.logical_not(first))
    def _():
        for buf in (qbuf, kbuf, vbuf):
            buf[pl.ds(0, halo), :] = buf[pl.ds(ts, halo), :]

    for src, buf in ((q_ref, qbuf), (k_ref, kbuf), (v_ref, vbuf)):
        buf[pl.ds(halo, ts), :] = src[...]
    for src, dst in zip(cast_src, cast_dst):
        dst[...] = src[...].astype(dst.dtype)

    row = lax.broadcasted_iota(jnp.int32, (chunk, chunk), 0)
    col = lax.broadcasted_iota(jnp.int32, (chunk, chunk), 1)
    lower_incl = (row >= col).astype(F32)
    causal = row >= col
    strict = row > col
    scale = DN_HEAD_DIM ** -0.5
    heads = range(DN_HEADS)
    chunks = range(ts // chunk)
    head_cols = [slice(h * DN_HEAD_DIM, (h + 1) * DN_HEAD_DIM) for h in heads]
    pairs = [(c, h) for c in chunks for h in heads]
    slot = lambda c, h: c * DN_HEADS + h

    def l2_normalized(x, extra_scale):
        return x * (lax.rsqrt(jnp.sum(x * x, axis=-1, keepdims=True) + EPS) * extra_scale)

    def conv_silu(buf, c, which):
        w = [cw_ref[pl.ds(j, 1), pl.ds(which * DN_WIDTH, DN_WIDTH)] for j in range(DN_CONV_K)]
        cur = buf[pl.ds(c * chunk, chunk + halo), :]
        prev = pltpu.roll(cur, 1, axis=0)
        tail = pltpu.roll(w[1] * cur + w[0] * prev, 2, axis=0)
        return _silu((w[3] * cur + w[2] * prev + tail)[halo:, :])

    qa = [conv_silu(qbuf, c, 0) for c in chunks]
    ka = [conv_silu(kbuf, c, 1) for c in chunks]
    va = [conv_silu(vbuf, c, 2) for c in chunks]
    gts = [gates_ref[pl.ds(c * chunk, chunk), :] for c in chunks]
    gcum = [_mm_exact(lower_incl, g) for g in gts]
    gcum_t = [g.T for g in gcum]
    q = [l2_normalized(qa[c][:, head_cols[h]], scale) for c, h in pairs]
    k = [l2_normalized(ka[c][:, head_cols[h]], 1.0) for c, h in pairs]
    v = [va[c][:, head_cols[h]] for c, h in pairs]
    gc = [gcum[c][:, h:h + 1] for c, h in pairs]
    gr = [gcum_t[c][h:h + 1, :] for c, h in pairs]
    beta = [gts[c][:, DN_HEADS + h:DN_HEADS + h + 1] for c, h in pairs]
    g_last = [g[chunk - 1:chunk, :] for g in gc]
    decay = _each(lambda a, b: jnp.exp(jnp.where(causal, a - b, -1e30)), gc, gr)
    kk = _each(_mm_nt, k, k)
    low = _each(lambda m, d, b: jnp.where(strict, m * d * b, 0.0), kk, decay, beta)
    inv = _unit_lower_inverse(low, row, col)
    eg = _each(jnp.exp, gc)
    rhs = _each(lambda vv, kx, b, e: jnp.concatenate([vv * b, kx * (b * e)], axis=1),
                v, k, beta, eg)
    sol = _each(_mm, inv, rhs)
    qk = _each(lambda a, b, d: _mm_nt(a, b) * d, q, k, decay)
    for (c, h), s, qx, e, a, kx, gl, g in zip(pairs, sol, q, eg, qk, k, g_last, gc):
        i = slot(c, h)
        u_s[i] = s[:, :DN_HEAD_DIM]
        wq_s[i] = jnp.concatenate([s[:, DN_HEAD_DIM:], qx * e], axis=0).astype(BF16)
        qk_s[i] = a.astype(BF16)
        kdt_s[i] = (kx * jnp.exp(gl - g)).T.astype(BF16)
    carry_decay = _each(jnp.exp, g_last)

    state = [state_ref[h] for h in heads]
    for c in chunks:
        ids = [slot(c, h) for h in heads]
        ws_qs = [jnp.dot(wq_s[i], st.astype(BF16), preferred_element_type=F32)
                 for i, st in zip(ids, state)]
        v_new = [(u_s[i] - x[:chunk]).astype(BF16) for i, x in zip(ids, ws_qs)]
        o = [x[chunk:] + jnp.dot(qk_s[i], vn, preferred_element_type=F32)
             for i, x, vn in zip(ids, ws_qs, v_new)]
        state = [st * carry_decay[i] + jnp.dot(kdt_s[i], vn, preferred_element_type=F32)
                 for i, st, vn in zip(ids, state, v_new)]
        rows = pl.ds(c * chunk, chunk)
        for h, sl in enumerate(head_cols):
            y = o[h] * _rms_scale(o[h]) * ng_ref[...] * _silu(z_ref[rows, sl])
            out_ref[rows, sl] = y.astype(out_ref.dtype)
    for h in heads:
        state_ref[h] = state[h]


def _deltanet(proj, gates, conv_w, norm_g, cast_weights, *, seq, ts, chunk):
    m = proj.shape[0]
    tiles = seq // ts
    steps = (m // seq) * tiles
    pairs = (ts // chunk) * DN_HEADS
    col0 = 3 * CONV_WIDTH // DN_WIDTH
    row_map = lambda b, t: b * tiles + t
    qkvz = [pl.BlockSpec((ts, DN_WIDTH), functools.partial(lambda b, t, n: (b * tiles + t, col0 + n), n=n))
            for n in range(4)]
    cast_specs = []
    for w in cast_weights:
        share = 1 if w.shape[0] % (steps * SUBLANES_BF16) == 0 else 2
        assert w.shape[0] % (steps // share * SUBLANES_BF16) == 0, w.shape
        cast_specs.append(pl.BlockSpec(
            (w.shape[0] // (steps // share), w.shape[1]),
            functools.partial(lambda b, t, share: (row_map(b, t) // share, 0), share=share)))
    outs = pl.pallas_call(
        functools.partial(_deltanet_kernel, ts=ts, chunk=chunk, n_cast=len(cast_weights)),
        grid=(m // seq, tiles),
        in_specs=qkvz + [
            pl.BlockSpec((ts, LANES), lambda b, t: (row_map(b, t), 0)),
            pl.BlockSpec((DN_CONV_K, 3 * DN_WIDTH), lambda b, t: (0, 0)),
            pl.BlockSpec((1, DN_HEAD_DIM), lambda b, t: (0, 0)),
        ] + cast_specs,
        out_specs=[pl.BlockSpec((ts, DN_WIDTH), lambda b, t: (row_map(b, t), 0))] + cast_specs,
        out_shape=[jax.ShapeDtypeStruct((m, DN_WIDTH), BF16)]
        + [jax.ShapeDtypeStruct(w.shape, BF16) for w in cast_weights],
        scratch_shapes=[pltpu.VMEM((ts + SUBLANES_F32, DN_WIDTH), F32)] * 3 + [
            pltpu.VMEM((DN_HEADS, DN_HEAD_DIM, DN_HEAD_DIM), F32),
            pltpu.VMEM((pairs, chunk, DN_HEAD_DIM), F32),
            pltpu.VMEM((pairs, 2 * chunk, DN_HEAD_DIM), BF16),
            pltpu.VMEM((pairs, chunk, chunk), BF16),
            pltpu.VMEM((pairs, DN_HEAD_DIM, chunk), BF16),
        ],
        compiler_params=_params(("parallel", "arbitrary")),
        name="deltanet",
    )(proj, proj, proj, proj, gates, conv_w, norm_g, *cast_weights)
    return outs[0], outs[1:]


def _out_proj_kernel(ax_ref, ab_ref, ac_ref, hax_ref, hac_ref, yb_ref, x_ref, cw_ref, w_ref,
                     out_ref, pbuf, *, tm, seq):
    halo = SUBLANES_F32
    acc_b = jnp.dot(yb_ref[...], w_ref[pl.ds(CONV_WIDTH, DN_WIDTH), :], preferred_element_type=F32)
    seq_start = (pl.program_id(0) * tm) % seq == 0
    pbuf[pl.ds(0, halo), :] = jnp.where(seq_start, 0.0, hac_ref[...] * hax_ref[...])
    pbuf[pl.ds(halo, tm), :] = ac_ref[...] * ax_ref[...]
    acc = None
    for j in range(CONV_K):
        term = pbuf[pl.ds(halo - (CONV_K - 1) + j, tm), :] * cw_ref[pl.ds(j, 1), :]
        acc = term if acc is None else acc + term
    ya = (ab_ref[...] * acc).astype(BF16)
    out_ref[...] = (x_ref[...] + acc_b
                    + jnp.dot(ya, w_ref[pl.ds(0, CONV_WIDTH), :], preferred_element_type=F32))


def _out_proj(proj, yb, x2, conv_w, w_out, *, tm, seq):
    m = x2.shape[0]
    halo = SUBLANES_F32
    halo_row = lambda i: jnp.maximum(i * (tm // halo) - 1, 0)
    return pl.pallas_call(
        functools.partial(_out_proj_kernel, tm=tm, seq=seq),
        grid=(m // tm,),
        in_specs=[
            pl.BlockSpec((tm, CONV_WIDTH), lambda i: (i, 0)),
            pl.BlockSpec((tm, CONV_WIDTH), lambda i: (i, 1)),
            pl.BlockSpec((tm, CONV_WIDTH), lambda i: (i, 2)),
            pl.BlockSpec((halo, CONV_WIDTH), lambda i: (halo_row(i), 0)),
            pl.BlockSpec((halo, CONV_WIDTH), lambda i: (halo_row(i), 2)),
            pl.BlockSpec((tm, DN_WIDTH), lambda i: (i, 0)),
            pl.BlockSpec((tm, D_MODEL), lambda i: (i, 0)),
            pl.BlockSpec((CONV_K, CONV_WIDTH), lambda i: (0, 0)),
            pl.BlockSpec((D_MODEL, D_MODEL), lambda i: (0, 0)),
        ],
        out_specs=pl.BlockSpec((tm, D_MODEL), lambda i: (i, 0)),
        out_shape=jax.ShapeDtypeStruct((m, D_MODEL), F32),
        scratch_shapes=[pltpu.VMEM((tm + halo, CONV_WIDTH), F32)],
        compiler_params=_params(("parallel",)),
        name="out_proj",
    )(proj, proj, proj, proj, proj, yb, x2, conv_w, w_out)


def _ffn_kernel(x_ref, g_ref, wg_ref, wv_ref, cg_ref, cv_ref, wd_ref, out_ref,
                hs_ref, ubuf, carry, *, tm, seq):
    halo = SUBLANES_F32
    i = pl.program_id(0)
    j = pl.program_id(1)
    seq_start = (i * tm) % seq == 0

    @pl.when(jnp.logical_and(i == 0, j == 0))
    def _():
        carry[...] = jnp.zeros_like(carry)

    @pl.when(j == 0)
    def _():
        x = x_ref[...]
        hs_ref[...] = (x * _rms_scale(x) * g_ref[...]).astype(BF16)
        out_ref[...] = x

    def up_conv(w_ref, cw_ref, which, cs):
        up = jnp.dot(hs_ref[...], w_ref[:, cs], preferred_element_type=F32)
        ubuf[which, pl.ds(0, halo), cs] = jnp.where(seq_start, 0.0, carry[which, j, :, cs])
        ubuf[which, pl.ds(halo, tm), cs] = up
        carry[which, j, :, cs] = up[tm - halo:, :]
        acc = None
        for t in range(FFN_CONV_K):
            term = (ubuf[which, pl.ds(halo - (FFN_CONV_K - 1) + t, tm), cs]
                    * cw_ref[pl.ds(t, 1), cs])
            acc = term if acc is None else acc + term
        return acc

    bn = wd_ref.shape[0]
    halves = [pl.ds(h * (bn // 2), bn // 2) for h in range(2)]
    acts = []
    for cs in halves:
        gate = up_conv(wg_ref, cg_ref, 0, cs)
        val = up_conv(wv_ref, cv_ref, 1, cs)
        acts.append((_silu(gate) * val).astype(BF16))
    for cs, act in zip(halves, acts):
        out_ref[...] += jnp.dot(act, wd_ref[cs, :], preferred_element_type=F32)


def _ffn(x1, g, w_up, conv_w, w_down, *, tm, bn, seq):
    m = x1.shape[0]
    halo = SUBLANES_F32
    nj = D_FF // bn
    return pl.pallas_call(
        functools.partial(_ffn_kernel, tm=tm, seq=seq),
        grid=(m // tm, nj),
        in_specs=[
            pl.BlockSpec((tm, D_MODEL), lambda i, j: (i, 0)),
            pl.BlockSpec((1, D_MODEL), lambda i, j: (0, 0)),
            pl.BlockSpec((D_MODEL, bn), lambda i, j: (0, j)),
            pl.BlockSpec((D_MODEL, bn), lambda i, j: (0, j + nj)),
            pl.BlockSpec((FFN_CONV_K, bn), lambda i, j: (0, j)),
            pl.BlockSpec((FFN_CONV_K, bn), lambda i, j: (0, j + nj)),
            pl.BlockSpec((bn, D_MODEL), lambda i, j: (j, 0)),
        ],
        out_specs=pl.BlockSpec((tm, D_MODEL), lambda i, j: (i, 0)),
        out_shape=jax.ShapeDtypeStruct((m, D_MODEL), F32),
        scratch_shapes=[pltpu.VMEM((tm, D_MODEL), BF16),
                        pltpu.VMEM((2, tm + halo, bn), F32),
                        pltpu.VMEM((2, nj, halo, bn), F32)],
        compiler_params=_params(("arbitrary", "arbitrary")),
        name="conv_ffn",
    )(x1, g, w_up, w_up, conv_w, conv_w, w_down)


def _ple_kernel(x_ref, p_ref, g_ref, wpg_ref, wpp_ref, fg_ref, out_ref):
    x = x_ref[...]
    hb = (x * _rms_scale(x) * g_ref[...]).astype(BF16)
    gate = jax.nn.sigmoid(jnp.dot(hb, wpg_ref[...], preferred_element_type=F32))
    emb = jnp.dot(p_ref[...].astype(BF16), wpp_ref[...], preferred_element_type=F32)
    y = x + gate * emb
    out_ref[...] = y * _rms_scale(y) * fg_ref[...]


def _ple(x2, p2, g, w_pg, w_pp, final_g, *, tm):
    m = x2.shape[0]
    return pl.pallas_call(
        _ple_kernel,
        grid=(m // tm,),
        in_specs=[
            pl.BlockSpec((tm, D_MODEL), lambda i: (i, 0)),
            pl.BlockSpec((tm, PLE_DIM), lambda i: (i, 0)),
            pl.BlockSpec((1, D_MODEL), lambda i: (0, 0)),
            pl.BlockSpec((D_MODEL, D_MODEL), lambda i: (0, 0)),
            pl.BlockSpec((PLE_DIM, D_MODEL), lambda i: (0, 0)),
            pl.BlockSpec((1, D_MODEL), lambda i: (0, 0)),
        ],
        out_specs=pl.BlockSpec((tm, D_MODEL), lambda i: (i, 0)),
        out_shape=jax.ShapeDtypeStruct((m, D_MODEL), F32),
        compiler_params=_params(("parallel",)),
        name="ple",
    )(x2, p2, g, w_pg, w_pp, final_g)


def _layer(x2, p2, seq, norm_mix_g, w_in, conv_a_w, conv_qkv_w, a_log, dt_bias, dn_norm_g,
           w_out, norm_ffn_g, w_up, conv_ffn_w, w_down, norm_ple_g, w_ple_gate, w_ple_proj,
           out_norm_g):
    row = lambda v: v.reshape(1, -1).astype(F32)
    lane_pad = lambda v: jnp.pad(row(v), ((0, 0), (0, LANES - v.shape[-1])))
    w_gate = jnp.pad(w_in[:, PROJ_COLS:], ((0, 0), (0, LANES - 2 * DN_HEADS))).astype(BF16)

    proj, gates = _in_proj(x2, row(norm_mix_g), w_in, w_gate, lane_pad(a_log), lane_pad(dt_bias),
                           tm=IN_TM, tn=IN_TN)
    yb, (w_out_b, w_up_b, w_down_b, w_pg_b) = _deltanet(
        proj, gates, conv_qkv_w.astype(F32), row(dn_norm_g), (w_out, w_up, w_down, w_ple_gate),
        seq=seq, ts=DN_TS, chunk=DN_CHUNK)
    x2 = _out_proj(proj, yb, x2, conv_a_w.astype(F32), w_out_b, tm=OUT_TM, seq=seq)
    x2 = _ffn(x2, row(norm_ffn_g), w_up_b, conv_ffn_w.astype(F32), w_down_b,
              tm=FFN_TM, bn=FFN_BN, seq=seq)
    return _ple(x2, p2, row(norm_ple_g), w_pg_b, w_ple_proj.astype(BF16),
                row(out_norm_g), tm=PLE_TM)


def kernel(x, p, norm_mix_g, w_in, conv_a_w, conv_qkv_w, a_log, dt_bias, dn_norm_g, w_out,
           norm_ffn_g, w_up, conv_ffn_w, w_down, norm_ple_g, w_ple_gate, w_ple_proj, final_norm_g):
    batch, seq, d_model = x.shape
    depth = p.shape[0]
    assert depth == 1 and d_model == D_MODEL
    x2 = x.reshape(batch * seq, d_model)
    p2 = p[0].reshape(batch * seq, PLE_DIM)
    out = _layer(x2, p2, seq, norm_mix_g[0], w_in[0], conv_a_w[0], conv_qkv_w[0], a_log[0],
                 dt_bias[0], dn_norm_g[0], w_out[0], norm_ffn_g[0], w_up[0], conv_ffn_w[0],
                 w_down[0], norm_ple_g[0], w_ple_gate[0], w_ple_proj[0], final_norm_g)
    return out.reshape(batch, seq, d_model)
```

```python
import functools

import jax
import jax.numpy as jnp
from jax import lax
from jax.experimental import pallas as pl
from jax.experimental.pallas import tpu as pltpu

D_MODEL = 2048
CONV_WIDTH = 1024
CONV_K = 3
DN_HEADS = 8
DN_HEAD_DIM = 128
DN_WIDTH = DN_HEADS * DN_HEAD_DIM
DN_CONV_K = 4
D_FF = 5632
FFN_CONV_K = 3
PLE_DIM = 256
EPS = 1e-6
PROJ_COLS = 3 * CONV_WIDTH + 4 * DN_WIDTH

LANES = 128
SUBLANES_F32 = 8
SUBLANES_BF16 = 16
V7X_VMEM_BYTES = 64 * 1024 * 1024
VMEM_LIMIT_BYTES = V7X_VMEM_BYTES - 2 * 1024 * 1024

IN_TM, IN_TN = 1024, 1024
DN_TS = 256
OUT_TM = 512
FFN_TM, FFN_BN = 1024, 512
PLE_TM = 512

DN_CHUNK = 128
INV_BASE = 8

F32 = jnp.float32
BF16 = jnp.bfloat16


def _mm(a, b):
    return jnp.dot(a.astype(BF16), b.astype(BF16), preferred_element_type=F32)


def _mm_nt(a, b):
    return lax.dot_general(a.astype(BF16), b.astype(BF16), (((1,), (1,)), ((), ())),
                           preferred_element_type=F32)


def _mm_exact(a, b):
    return jnp.dot(a, b, precision=lax.Precision.HIGHEST, preferred_element_type=F32)


def _rms_scale(x):
    return lax.rsqrt(jnp.mean(x * x, axis=-1, keepdims=True) + EPS)


def _silu(x):
    return x * jax.nn.sigmoid(x)


def _params(semantics):
    return pltpu.CompilerParams(dimension_semantics=semantics, vmem_limit_bytes=VMEM_LIMIT_BYTES)


def _norm_and_gates(x_ref, g_ref, wg_ref, alog_ref, dtb_ref, gates_ref, hs_ref):
    x = x_ref[...]
    hb = (x * _rms_scale(x) * g_ref[...]).astype(BF16)
    hs_ref[...] = hb
    raw = lax.dot_general(hb, wg_ref[...], (((1,), (1,)), ((), ())),
                          preferred_element_type=F32)
    a = raw + dtb_ref[...]
    softplus = jnp.maximum(a, 0.0) + jnp.log(1.0 + jnp.exp(-jnp.abs(a)))
    decay = -jnp.exp(alog_ref[...]) * softplus
    lane = lax.broadcasted_iota(jnp.int32, raw.shape, 1)
    gates_ref[...] = jnp.where(lane < DN_HEADS, decay, jax.nn.sigmoid(raw))


def _in_proj_head_kernel(x_ref, g_ref, w_ref, wg_ref, alog_ref, dtb_ref,
                         proj_ref, gates_ref, wb_ref, hs_ref):
    @pl.when(pl.program_id(0) == 0)
    def _():
        _norm_and_gates(x_ref, g_ref, wg_ref, alog_ref, dtb_ref, gates_ref, hs_ref)

    wb = w_ref[...].astype(BF16)
    wb_ref[...] = wb
    proj_ref[...] = lax.dot_general(hs_ref[...], wb, (((1,), (1,)), ((), ())),
                                    preferred_element_type=F32)


def _in_proj_kernel(x_ref, g_ref, w_ref, wg_ref, alog_ref, dtb_ref, proj_in, gates_in,
                    proj_ref, gates_ref, hs_ref):
    del proj_in, gates_in
    @pl.when(pl.program_id(1) == 0)
    def _():
        _norm_and_gates(x_ref, g_ref, wg_ref, alog_ref, dtb_ref, gates_ref, hs_ref)

    tn = proj_ref.shape[1]
    for block in range(PROJ_COLS // tn):
        @pl.when(pl.program_id(1) == block)
        def _(block=block):
            proj_ref[...] = lax.dot_general(hs_ref[...], w_ref[pl.ds(block * tn, tn), :],
                                            (((1,), (1,)), ((), ())),
                                            preferred_element_type=F32)


def _in_proj(x2, g, w_in_t, w_gate, alog_pad, dtb_pad, *, tm, tn):
    m = x2.shape[0]
    n_blocks = PROJ_COLS // tn
    small = [
        pl.BlockSpec((LANES, D_MODEL), lambda *_: (0, 0)),
        pl.BlockSpec((1, LANES), lambda *_: (0, 0)),
        pl.BlockSpec((1, LANES), lambda *_: (0, 0)),
    ]
    out_shape = [jax.ShapeDtypeStruct((m, PROJ_COLS), F32), jax.ShapeDtypeStruct((m, LANES), F32)]
    proj, gates, w_bf16 = pl.pallas_call(
        _in_proj_head_kernel,
        grid=(n_blocks,),
        in_specs=[
            pl.BlockSpec((tm, D_MODEL), lambda j: (0, 0)),
            pl.BlockSpec((1, D_MODEL), lambda j: (0, 0)),
            pl.BlockSpec((tn, D_MODEL), lambda j: (j, 0)),
        ] + small,
        out_specs=[
            pl.BlockSpec((tm, tn), lambda j: (0, j)),
            pl.BlockSpec((tm, LANES), lambda j: (0, 0)),
            pl.BlockSpec((tn, D_MODEL), lambda j: (j, 0)),
        ],
        out_shape=out_shape + [jax.ShapeDtypeStruct((PROJ_COLS, D_MODEL), BF16)],
        scratch_shapes=[pltpu.VMEM((tm, D_MODEL), BF16)],
        compiler_params=_params(("arbitrary",)),
        name="in_proj_head",
    )(x2, g, w_in_t, w_gate, alog_pad, dtb_pad)
    return pl.pallas_call(
        _in_proj_kernel,
        grid=(m // tm - 1, n_blocks),
        in_specs=[
            pl.BlockSpec((tm, D_MODEL), lambda i, j: (i + 1, 0)),
            pl.BlockSpec((1, D_MODEL), lambda i, j: (0, 0)),
            pl.BlockSpec(w_bf16.shape, lambda i, j: (0, 0), pipeline_mode=pl.Buffered(1)),
        ] + small + [
            pl.BlockSpec(memory_space=pl.ANY),
            pl.BlockSpec(memory_space=pl.ANY),
        ],
        out_specs=[
            pl.BlockSpec((tm, tn), lambda i, j: (i + 1, j)),
            pl.BlockSpec((tm, LANES), lambda i, j: (i + 1, 0)),
        ],
        out_shape=out_shape,
        input_output_aliases={6: 0, 7: 1},
        scratch_shapes=[pltpu.VMEM((tm, D_MODEL), BF16)],
        compiler_params=_params(("parallel", "arbitrary")),
        name="in_proj",
    )(x2, g, w_bf16, w_gate, alog_pad, dtb_pad, proj, gates)


def _each(fn, *lists):
    return [fn(*args) for args in zip(*lists)]


def _unit_lower_inverse(lows, row, col):
    c = lows[0].shape[0]
    eye = (row == col).astype(F32)
    base_shift = INV_BASE.bit_length() - 1
    base_mask = (row >> base_shift) == (col >> base_shift)
    l0 = [jnp.where(base_mask, low, 0.0) for low in lows]
    l2 = _each(_mm, l0, l0)
    l4 = _each(_mm, l2, l2)
    l3 = _each(_mm, l0, l2)
    p1 = _each(lambda a, b, d: eye - a + b - d, l0, l2, l3)
    inv = _each(lambda p, pl4: p + pl4, p1, _each(_mm, p1, l4))
    shift = base_shift
    while (1 << shift) < c:
        off_mask = (((row >> (shift + 1)) == (col >> (shift + 1)))
                    & ((row >> shift) != (col >> shift)))
        off_inv = _each(lambda low, t: _mm(jnp.where(off_mask, low, 0.0), t), lows, inv)
        inv = _each(lambda t, x: t - _mm(t, x), inv, off_inv)
        shift += 1
    return inv


def _deltanet_kernel(q_ref, k_ref, v_ref, z_ref, gates_ref, cw_ref, ng_ref, *refs,
                     ts, chunk, n_cast):
    cast_src = refs[:n_cast]
    out_ref = refs[n_cast]
    cast_dst = refs[n_cast + 1:2 * n_cast + 1]
    qbuf, kbuf, vbuf, state_ref, u_s, wq_s, qk_s, kdt_s = refs[2 * n_cast + 1:]

    halo = SUBLANES_F32
    first = pl.program_id(1) == 0

    @pl.when(first)
    def _():
        state_ref[...] = jnp.zeros_like(state_ref)
        for buf in (qbuf, kbuf, vbuf):
            buf[pl.ds(0, halo), :] = jnp.zeros((halo, DN_WIDTH), F32)

    @pl.when(jnp.logical_not(first))
    def _():
        for buf in (qbuf, kbuf, vbuf):
            buf[pl.ds(0, halo), :] = buf[pl.ds(ts, halo), :]

    for src, buf in ((q_ref, qbuf), (k_ref, kbuf), (v_ref, vbuf)):
        buf[pl.ds(halo, ts), :] = src[...]
    for src, dst in zip(cast_src, cast_dst):
        dst[...] = src[...].astype(dst.dtype)

    row = lax.broadcasted_iota(jnp.int32, (chunk, chunk), 0)
    col = lax.broadcasted_iota(jnp.int32, (chunk, chunk), 1)
    lower_incl = (row >= col).astype(F32)
    causal = row >= col
    strict = row > col
    scale = DN_HEAD_DIM ** -0.5
    heads = range(DN_HEADS)
    chunks = range(ts // chunk)
    head_cols = [slice(h * DN_HEAD_DIM, (h + 1) * DN_HEAD_DIM) for h in heads]
    pairs = [(c, h) for c in chunks for h in heads]
    slot = lambda c, h: c * DN_HEADS + h

    def l2_normalized(x, extra_scale):
        return x * (lax.rsqrt(jnp.sum(x * x, axis=-1, keepdims=True) + EPS) * extra_scale)

    def conv_silu(buf, c, which):
        w = [cw_ref[pl.ds(j, 1), pl.ds(which * DN_WIDTH, DN_WIDTH)] for j in range(DN_CONV_K)]
        cur = buf[pl.ds(c * chunk, chunk + halo), :]
        prev = pltpu.roll(cur, 1, axis=0)
        tail = pltpu.roll(w[1] * cur + w[0] * prev, 2, axis=0)
        return _silu((w[3] * cur + w[2] * prev + tail)[halo:, :])

    qa = [conv_silu(qbuf, c, 0) for c in chunks]
    ka = [conv_silu(kbuf, c, 1) for c in chunks]
    va = [conv_silu(vbuf, c, 2) for c in chunks]
    gts = [gates_ref[pl.ds(c * chunk, chunk), :] for c in chunks]
    gcum = [_mm_exact(lower_incl, g) for g in gts]
    gcum_t = [g.T for g in gcum]
    q = [l2_normalized(qa[c][:, head_cols[h]], scale) for c, h in pairs]
    k = [l2_normalized(ka[c][:, head_cols[h]], 1.0) for c, h in pairs]
    v = [va[c][:, head_cols[h]] for c, h in pairs]
    gc = [gcum[c][:, h:h + 1] for c, h in pairs]
    gr = [gcum_t[c][h:h + 1, :] for c, h in pairs]
    beta = [gts[c][:, DN_HEADS + h:DN_HEADS + h + 1] for c, h in pairs]
    g_last = [g[chunk - 1:chunk, :] for g in gc]
    decay = _each(lambda a, b: jnp.exp(jnp.where(causal, a - b, -1e30)), gc, gr)
    kk = _each(_mm_nt, k, k)
    low = _each(lambda m, d, b: jnp.where(strict, m * d * b, 0.0), kk, decay, beta)
    inv = _unit_lower_inverse(low, row, col)
    eg = _each(jnp.exp, gc)
    rhs = _each(lambda vv, kx, b, e: jnp.concatenate([vv * b, kx * (b * e)], axis=1),
                v, k, beta, eg)
    sol = _each(_mm, inv, rhs)
    qk = _each(lambda a, b, d: _mm_nt(a, b) * d, q, k, decay)
    for (c, h), s, qx, e, a, kx, gl, g in zip(pairs, sol, q, eg, qk, k, g_last, gc):
        i = slot(c, h)
        u_s[i] = s[:, :DN_HEAD_DIM]
        wq_s[i] = jnp.concatenate([s[:, DN_HEAD_DIM:], qx * e], axis=0).astype(BF16)
        qk_s[i] = a.astype(BF16)
        kdt_s[i] = (kx * jnp.exp(gl - g)).T.astype(BF16)
    carry_decay = _each(jnp.exp, g_last)

    state = [state_ref[h] for h in heads]
    for c in chunks:
        ids = [slot(c, h) for h in heads]
        ws_qs = [jnp.dot(wq_s[i], st.astype(BF16), preferred_element_type=F32)
                 for i, st in zip(ids, state)]
        v_new = [(u_s[i] - x[:chunk]).astype(BF16) for i, x in zip(ids, ws_qs)]
        o = [x[chunk:] + jnp.dot(qk_s[i], vn, preferred_element_type=F32)
             for i, x, vn in zip(ids, ws_qs, v_new)]
        state = [st * carry_decay[i] + jnp.dot(kdt_s[i], vn, preferred_element_type=F32)
                 for i, st, vn in zip(ids, state, v_new)]
        rows = pl.ds(c * chunk, chunk)
        for h, sl in enumerate(head_cols):
            y = o[h] * _rms_scale(o[h]) * ng_ref[...] * _silu(z_ref[rows, sl])
            out_ref[rows, sl] = y.astype(out_ref.dtype)
    for h in heads:
        state_ref[h] = state[h]


def _deltanet(proj, gates, conv_w, norm_g, cast_weights, *, seq, ts, chunk):
    m = proj.shape[0]
    tiles = seq // ts
    steps = (m // seq) * tiles
    pairs = (ts // chunk) * DN_HEADS
    col0 = 3 * CONV_WIDTH // DN_WIDTH
    row_map = lambda b, t: b * tiles + t
    qkvz = [pl.BlockSpec((ts, DN_WIDTH), functools.partial(lambda b, t, n: (b * tiles + t, col0 + n), n=n))
            for n in range(4)]
    cast_specs = []
    for w in cast_weights:
        share = 1 if w.shape[0] % (steps * SUBLANES_BF16) == 0 else 2
        assert w.shape[0] % (steps // share * SUBLANES_BF16) == 0, w.shape
        cast_specs.append(pl.BlockSpec(
            (w.shape[0] // (steps // share), w.shape[1]),
            functools.partial(lambda b, t, share: (row_map(b, t) // share, 0), share=share)))
    outs = pl.pallas_call(
        functools.partial(_deltanet_kernel, ts=ts, chunk=chunk, n_cast=len(cast_weights)),
        grid=(m // seq, tiles),
        in_specs=qkvz + [
            pl.BlockSpec((ts, LANES), lambda b, t: (row_map(b, t), 0)),
            pl.BlockSpec((DN_CONV_K, 3 * DN_WIDTH), lambda b, t: (0, 0)),
            pl.BlockSpec((1, DN_HEAD_DIM), lambda b, t: (0, 0)),
        ] + cast_specs,
        out_specs=[pl.BlockSpec((ts, DN_WIDTH), lambda b, t: (row_map(b, t), 0))] + cast_specs,
        out_shape=[jax.ShapeDtypeStruct((m, DN_WIDTH), BF16)]
        + [jax.ShapeDtypeStruct(w.shape, BF16) for w in cast_weights],
        scratch_shapes=[pltpu.VMEM((ts + SUBLANES_F32, DN_WIDTH), F32)] * 3 + [
            pltpu.VMEM((DN_HEADS, DN_HEAD_DIM, DN_HEAD_DIM), F32),
            pltpu.VMEM((pairs, chunk, DN_HEAD_DIM), F32),
            pltpu.VMEM((pairs, 2 * chunk, DN_HEAD_DIM), BF16),
            pltpu.VMEM((pairs, chunk, chunk), BF16),
            pltpu.VMEM((pairs, DN_HEAD_DIM, chunk), BF16),
        ],
        compiler_params=_params(("parallel", "arbitrary")),
        name="deltanet",
    )(proj, proj, proj, proj, gates, conv_w, norm_g, *cast_weights)
    return outs[0], outs[1:]


def _out_proj_kernel(ax_ref, ab_ref, ac_ref, hax_ref, hac_ref, yb_ref, x_ref, cw_ref, w_ref,
                     out_ref, pbuf, *, tm, seq):
    halo = SUBLANES_F32
    acc_b = jnp.dot(yb_ref[...], w_ref[pl.ds(CONV_WIDTH, DN_WIDTH), :], preferred_element_type=F32)
    seq_start = (pl.program_id(0) * tm) % seq == 0
    pbuf[pl.ds(0, halo), :] = jnp.where(seq_start, 0.0, hac_ref[...] * hax_ref[...])
    pbuf[pl.ds(halo, tm), :] = ac_ref[...] * ax_ref[...]
    acc = None
    for j in range(CONV_K):
        term = pbuf[pl.ds(halo - (CONV_K - 1) + j, tm), :] * cw_ref[pl.ds(j, 1), :]
        acc = term if acc is None else acc + term
    ya = (ab_ref[...] * acc).astype(BF16)
    out_ref[...] = (x_ref[...] + acc_b
                    + jnp.dot(ya, w_ref[pl.ds(0, CONV_WIDTH), :], preferred_element_type=F32))


def _out_proj(proj, yb, x2, conv_w, w_out, *, tm, seq):
    m = x2.shape[0]
    halo = SUBLANES_F32
    halo_row = lambda i: jnp.maximum(i * (tm // halo) - 1, 0)
    return pl.pallas_call(
        functools.partial(_out_proj_kernel, tm=tm, seq=seq),
        grid=(m // tm,),
        in_specs=[
            pl.BlockSpec((tm, CONV_WIDTH), lambda i: (i, 0)),
            pl.BlockSpec((tm, CONV_WIDTH), lambda i: (i, 1)),
            pl.BlockSpec((tm, CONV_WIDTH), lambda i: (i, 2)),
            pl.BlockSpec((halo, CONV_WIDTH), lambda i: (halo_row(i), 0)),
            pl.BlockSpec((halo, CONV_WIDTH), lambda i: (halo_row(i), 2)),
            pl.BlockSpec((tm, DN_WIDTH), lambda i: (i, 0)),
            pl.BlockSpec((tm, D_MODEL), lambda i: (i, 0)),
            pl.BlockSpec((CONV_K, CONV_WIDTH), lambda i: (0, 0)),
            pl.BlockSpec((D_MODEL, D_MODEL), lambda i: (0, 0)),
        ],
        out_specs=pl.BlockSpec((tm, D_MODEL), lambda i: (i, 0)),
        out_shape=jax.ShapeDtypeStruct((m, D_MODEL), F32),
        scratch_shapes=[pltpu.VMEM((tm + halo, CONV_WIDTH), F32)],
        compiler_params=_params(("parallel",)),
        name="out_proj",
    )(proj, proj, proj, proj, proj, yb, x2, conv_w, w_out)


def _ffn_kernel(x_ref, g_ref, wg_ref, wv_ref, cg_ref, cv_ref, wd_ref, out_ref,
                hs_ref, ubuf, carry, *, tm, seq):
    halo = SUBLANES_F32
    i = pl.program_id(0)
    j = pl.program_id(1)
    seq_start = (i * tm) % seq == 0

    @pl.when(jnp.logical_and(i == 0, j == 0))
    def _():
        carry[...] = jnp.zeros_like(carry)

    @pl.when(j == 0)
    def _():
        x = x_ref[...]
        hs_ref[...] = (x * _rms_scale(x) * g_ref[...]).astype(BF16)
        out_ref[...] = x

    def up_conv(w_ref, cw_ref, which, cs):
        up = jnp.dot(hs_ref[...], w_ref[:, cs], preferred_element_type=F32)
        ubuf[which, pl.ds(0, halo), cs] = jnp.where(seq_start, 0.0, carry[which, j, :, cs])
        ubuf[which, pl.ds(halo, tm), cs] = up
        carry[which, j, :, cs] = up[tm - halo:, :]
        acc = None
        for t in range(FFN_CONV_K):
            term = (ubuf[which, pl.ds(halo - (FFN_CONV_K - 1) + t, tm), cs]
                    * cw_ref[pl.ds(t, 1), cs])
            acc = term if acc is None else acc + term
        return acc

    bn = wd_ref.shape[0]
    halves = [pl.ds(h * (bn // 2), bn // 2) for h in range(2)]
    acts = []
    for cs in halves:
        gate = up_conv(wg_ref, cg_ref, 0, cs)
        val = up_conv(wv_ref, cv_ref, 1, cs)
        acts.append((_silu(gate) * val).astype(BF16))
    for cs, act in zip(halves, acts):
        out_ref[...] += jnp.dot(act, wd_ref[cs, :], preferred_element_type=F32)


def _ffn(x1, g, w_up, conv_w, w_down, *, tm, bn, seq):
    m = x1.shape[0]
    halo = SUBLANES_F32
    nj = D_FF // bn
    return pl.pallas_call(
        functools.partial(_ffn_kernel, tm=tm, seq=seq),
        grid=(m // tm, nj),
        in_specs=[
            pl.BlockSpec((tm, D_MODEL), lambda i, j: (i, 0)),
            pl.BlockSpec((1, D_MODEL), lambda i, j: (0, 0)),
            pl.BlockSpec((D_MODEL, bn), lambda i, j: (0, j)),
            pl.BlockSpec((D_MODEL, bn), lambda i, j: (0, j + nj)),
            pl.BlockSpec((FFN_CONV_K, bn), lambda i, j: (0, j)),
            pl.BlockSpec((FFN_CONV_K, bn), lambda i, j: (0, j + nj)),
            pl.BlockSpec((bn, D_MODEL), lambda i, j: (j, 0)),
        ],
        out_specs=pl.BlockSpec((tm, D_MODEL), lambda i, j: (i, 0)),
        out_shape=jax.ShapeDtypeStruct((m, D_MODEL), F32),
        scratch_shapes=[pltpu.VMEM((tm, D_MODEL), BF16),
                        pltpu.VMEM((2, tm + halo, bn), F32),
                        pltpu.VMEM((2, nj, halo, bn), F32)],
        compiler_params=_params(("arbitrary", "arbitrary")),
        name="conv_ffn",
    )(x1, g, w_up, w_up, conv_w, conv_w, w_down)


def _ple_kernel(x_ref, p_ref, g_ref, wpg_ref, wpp_ref, fg_ref, out_ref):
    x = x_ref[...]
    hb = (x * _rms_scale(x) * g_ref[...]).astype(BF16)
    gate = jax.nn.sigmoid(jnp.dot(hb, wpg_ref[...], preferred_element_type=F32))
    emb = jnp.dot(p_ref[...].astype(BF16), wpp_ref[...], preferred_element_type=F32)
    y = x + gate * emb
    out_ref[...] = y * _rms_scale(y) * fg_ref[...]


def _ple(x2, p2, g, w_pg, w_pp, final_g, *, tm):
    m = x2.shape[0]
    return pl.pallas_call(
        _ple_kernel,
        grid=(m // tm,),
        in_specs=[
            pl.BlockSpec((tm, D_MODEL), lambda i: (i, 0)),
            pl.BlockSpec((tm, PLE_DIM), lambda i: (i, 0)),
            pl.BlockSpec((1, D_MODEL), lambda i: (0, 0)),
            pl.BlockSpec((D_MODEL, D_MODEL), lambda i: (0, 0)),
            pl.BlockSpec((PLE_DIM, D_MODEL), lambda i: (0, 0)),
            pl.BlockSpec((1, D_MODEL), lambda i: (0, 0)),
        ],
        out_specs=pl.BlockSpec((tm, D_MODEL), lambda i: (i, 0)),
        out_shape=jax.ShapeDtypeStruct((m, D_MODEL), F32),
        compiler_params=_params(("parallel",)),
        name="ple",
    )(x2, p2, g, w_pg, w_pp, final_g)


def _layer(x2, p2, seq, norm_mix_g, w_in, conv_a_w, conv_qkv_w, a_log, dt_bias, dn_norm_g,
           w_out, norm_ffn_g, w_up, conv_ffn_w, w_down, norm_ple_g, w_ple_gate, w_ple_proj,
           out_norm_g):
    row = lambda v: v.reshape(1, -1).astype(F32)
    lane_pad = lambda v: jnp.pad(row(v), ((0, 0), (0, LANES - v.shape[-1])))
    w_in_t = w_in.T
    w_gate = jnp.pad(w_in_t[PROJ_COLS:, :], ((0, LANES - 2 * DN_HEADS), (0, 0))).astype(BF16)

    proj, gates = _in_proj(x2, row(norm_mix_g), w_in_t, w_gate, lane_pad(a_log), lane_pad(dt_bias),
                           tm=IN_TM, tn=IN_TN)
    yb, (w_out_b, w_up_b, w_down_b, w_pg_b) = _deltanet(
        proj, gates, conv_qkv_w.astype(F32), row(dn_norm_g), (w_out, w_up, w_down, w_ple_gate),
        seq=seq, ts=DN_TS, chunk=DN_CHUNK)
    x2 = _out_proj(proj, yb, x2, conv_a_w.astype(F32), w_out_b, tm=OUT_TM, seq=seq)
    x2 = _ffn(x2, row(norm_ffn_g), w_up_b, conv_ffn_w.astype(F32), w_down_b,
              tm=FFN_TM, bn=FFN_BN, seq=seq)
    return _ple(x2, p2, row(norm_ple_g), w_pg_b, w_ple_proj.astype(BF16),
                row(out_norm_g), tm=PLE_TM)


def kernel(x, p, norm_mix_g, w_in, conv_a_w, conv_qkv_w, a_log, dt_bias, dn_norm_g, w_out,
           norm_ffn_g, w_up, conv_ffn_w, w_down, norm_ple_g, w_ple_gate, w_ple_proj, final_norm_g):
    batch, seq, d_model = x.shape
    depth = p.shape[0]
    assert depth == 1 and d_model == D_MODEL
    x2 = x.reshape(batch * seq, d_model)
    p2 = p[0].reshape(batch * seq, PLE_DIM)
    out = _layer(x2, p2, seq, norm_mix_g[0], w_in[0], conv_a_w[0], conv_qkv_w[0], a_log[0],
                 dt_bias[0], dn_norm_g[0], w_out[0], norm_ffn_g[0], w_up[0], conv_ffn_w[0],
                 w_down[0], norm_ple_g[0], w_ple_gate[0], w_ple_proj[0], final_norm_g)
    return out.reshape(batch, seq, d_model)
```

```python
import functools

import jax
import jax.numpy as jnp
from jax import lax
from jax.experimental import pallas as pl
from jax.experimental.pallas import tpu as pltpu

D_MODEL = 2048
CONV_WIDTH = 1024
CONV_K = 3
DN_HEADS = 8
DN_HEAD_DIM = 128
DN_WIDTH = DN_HEADS * DN_HEAD_DIM
DN_CONV_K = 4
D_FF = 5632
FFN_CONV_K = 3
PLE_DIM = 256
EPS = 1e-6
PROJ_COLS = 3 * CONV_WIDTH + 4 * DN_WIDTH

LANES = 128
SUBLANES_F32 = 8
SUBLANES_BF16 = 16
V7X_VMEM_BYTES = 64 * 1024 * 1024
VMEM_LIMIT_BYTES = V7X_VMEM_BYTES - 2 * 1024 * 1024

IN_TM, IN_TN = 1024, 1024
DN_TS = 256
OUT_TM = 512
FFN_TM, FFN_BN = 1024, 512
PLE_TM = 1024

DN_CHUNK = 128
INV_BASE = 8

F32 = jnp.float32
BF16 = jnp.bfloat16


def _mm(a, b):
    return jnp.dot(a.astype(BF16), b.astype(BF16), preferred_element_type=F32)


def _mm_nt(a, b):
    return lax.dot_general(a.astype(BF16), b.astype(BF16), (((1,), (1,)), ((), ())),
                           preferred_element_type=F32)


def _mm_exact(a, b):
    return jnp.dot(a, b, precision=lax.Precision.HIGHEST, preferred_element_type=F32)


def _rms_scale(x):
    return lax.rsqrt(jnp.mean(x * x, axis=-1, keepdims=True) + EPS)


def _silu(x):
    return x * jax.nn.sigmoid(x)


def _params(semantics):
    return pltpu.CompilerParams(dimension_semantics=semantics, vmem_limit_bytes=VMEM_LIMIT_BYTES)


def _norm_and_gates(x_ref, g_ref, wg_ref, alog_ref, dtb_ref, gates_ref, hs_ref):
    x = x_ref[...]
    hb = (x * _rms_scale(x) * g_ref[...]).astype(BF16)
    hs_ref[...] = hb
    raw = lax.dot_general(hb, wg_ref[...], (((1,), (1,)), ((), ())),
                          preferred_element_type=F32)
    a = raw + dtb_ref[...]
    softplus = jnp.maximum(a, 0.0) + jnp.log(1.0 + jnp.exp(-jnp.abs(a)))
    decay = -jnp.exp(alog_ref[...]) * softplus
    lane = lax.broadcasted_iota(jnp.int32, raw.shape, 1)
    gates_ref[...] = jnp.where(lane < DN_HEADS, decay, jax.nn.sigmoid(raw))


def _in_proj_head_kernel(x_ref, g_ref, w_ref, wg_ref, alog_ref, dtb_ref,
                         proj_ref, gates_ref, wb_ref, hs_ref):
    @pl.when(pl.program_id(0) == 0)
    def _():
        _norm_and_gates(x_ref, g_ref, wg_ref, alog_ref, dtb_ref, gates_ref, hs_ref)

    wb = w_ref[...].T.astype(BF16)
    wb_ref[...] = wb
    proj_ref[...] = jnp.dot(hs_ref[...], wb, preferred_element_type=F32)


def _in_proj_kernel(x_ref, g_ref, w_ref, wg_ref, alog_ref, dtb_ref, proj_in, gates_in,
                    proj_ref, gates_ref, hs_ref):
    del proj_in, gates_in
    @pl.when(pl.program_id(1) == 0)
    def _():
        _norm_and_gates(x_ref, g_ref, wg_ref, alog_ref, dtb_ref, gates_ref, hs_ref)

    tn = proj_ref.shape[1]
    for block in range(PROJ_COLS // tn):
        @pl.when(pl.program_id(1) == block)
        def _(block=block):
            proj_ref[...] = jnp.dot(hs_ref[...], w_ref[:, pl.ds(block * tn, tn)],
                                    preferred_element_type=F32)


def _in_proj(x2, g, w_in_t, w_gate, alog_pad, dtb_pad, *, tm, tn):
    m = x2.shape[0]
    n_blocks = PROJ_COLS // tn
    small = [
        pl.BlockSpec((LANES, D_MODEL), lambda *_: (0, 0)),
        pl.BlockSpec((1, LANES), lambda *_: (0, 0)),
        pl.BlockSpec((1, LANES), lambda *_: (0, 0)),
    ]
    out_shape = [jax.ShapeDtypeStruct((m, PROJ_COLS), F32), jax.ShapeDtypeStruct((m, LANES), F32)]
    proj, gates, w_bf16 = pl.pallas_call(
        _in_proj_head_kernel,
        grid=(n_blocks,),
        in_specs=[
            pl.BlockSpec((tm, D_MODEL), lambda j: (0, 0)),
            pl.BlockSpec((1, D_MODEL), lambda j: (0, 0)),
            pl.BlockSpec((tn, D_MODEL), lambda j: (j, 0)),
        ] + small,
        out_specs=[
            pl.BlockSpec((tm, tn), lambda j: (0, j)),
            pl.BlockSpec((tm, LANES), lambda j: (0, 0)),
            pl.BlockSpec((D_MODEL, tn), lambda j: (0, j)),
        ],
        out_shape=out_shape + [jax.ShapeDtypeStruct((D_MODEL, PROJ_COLS), BF16)],
        scratch_shapes=[pltpu.VMEM((tm, D_MODEL), BF16)],
        compiler_params=_params(("arbitrary",)),
        name="in_proj_head",
    )(x2, g, w_in_t, w_gate, alog_pad, dtb_pad)
    return pl.pallas_call(
        _in_proj_kernel,
        grid=(m // tm - 1, n_blocks),
        in_specs=[
            pl.BlockSpec((tm, D_MODEL), lambda i, j: (i + 1, 0)),
            pl.BlockSpec((1, D_MODEL), lambda i, j: (0, 0)),
            pl.BlockSpec(w_bf16.shape, lambda i, j: (0, 0), pipeline_mode=pl.Buffered(1)),
        ] + small + [
            pl.BlockSpec(memory_space=pl.ANY),
            pl.BlockSpec(memory_space=pl.ANY),
        ],
        out_specs=[
            pl.BlockSpec((tm, tn), lambda i, j: (i + 1, j)),
            pl.BlockSpec((tm, LANES), lambda i, j: (i + 1, 0)),
        ],
        out_shape=out_shape,
        input_output_aliases={6: 0, 7: 1},
        scratch_shapes=[pltpu.VMEM((tm, D_MODEL), BF16)],
        compiler_params=_params(("parallel", "arbitrary")),
        name="in_proj",
    )(x2, g, w_bf16, w_gate, alog_pad, dtb_pad, proj, gates)


def _each(fn, *lists):
    return [fn(*args) for args in zip(*lists)]


def _unit_lower_inverse(lows, row, col):
    c = lows[0].shape[0]
    eye = (row == col).astype(F32)
    base_shift = INV_BASE.bit_length() - 1
    base_mask = (row >> base_shift) == (col >> base_shift)
    l0 = [jnp.where(base_mask, low, 0.0) for low in lows]
    l2 = _each(_mm, l0, l0)
    l4 = _each(_mm, l2, l2)
    l3 = _each(_mm, l0, l2)
    p1 = _each(lambda a, b, d: eye - a + b - d, l0, l2, l3)
    inv = _each(lambda p, pl4: p + pl4, p1, _each(_mm, p1, l4))
    shift = base_shift
    while (1 << shift) < c:
        off_mask = (((row >> (shift + 1)) == (col >> (shift + 1)))
                    & ((row >> shift) != (col >> shift)))
        off_inv = _each(lambda low, t: _mm(jnp.where(off_mask, low, 0.0), t), lows, inv)
        inv = _each(lambda t, x: t - _mm(t, x), inv, off_inv)
        shift += 1
    return inv


def _deltanet_kernel(q_ref, k_ref, v_ref, z_ref, gates_ref, cw_ref, ng_ref, *refs,
                     ts, chunk, n_cast):
    cast_src = refs[:n_cast]
    out_ref = refs[n_cast]
    cast_dst = refs[n_cast + 1:2 * n_cast + 1]
    qbuf, kbuf, vbuf, state_ref, u_s, wq_s, qk_s, kdt_s = refs[2 * n_cast + 1:]

    halo = SUBLANES_F32
    first = pl.program_id(1) == 0

    @pl.when(first)
    def _():
        state_ref[...] = jnp.zeros_like(state_ref)
        for buf in (qbuf, kbuf, vbuf):
            buf[pl.ds(0, halo), :] = jnp.zeros((halo, DN_WIDTH), F32)

    @pl.when(jnp.logical_not(first))
    def _():
        for buf in (qbuf, kbuf, vbuf):
            buf[pl.ds(0, halo), :] = buf[pl.ds(ts, halo), :]

    for src, buf in ((q_ref, qbuf), (k_ref, kbuf), (v_ref, vbuf)):
        buf[pl.ds(halo, ts), :] = src[...]
    for src, dst in zip(cast_src, cast_dst):
        dst[...] = src[...].astype(dst.dtype)

    row = lax.broadcasted_iota(jnp.int32, (chunk, chunk), 0)
    col = lax.broadcasted_iota(jnp.int32, (chunk, chunk), 1)
    lower_incl = (row >= col).astype(F32)
    causal = row >= col
    strict = row > col
    scale = DN_HEAD_DIM ** -0.5
    heads = range(DN_HEADS)
    chunks = range(ts // chunk)
    head_cols = [slice(h * DN_HEAD_DIM, (h + 1) * DN_HEAD_DIM) for h in heads]
    pairs = [(c, h) for c in chunks for h in heads]
    slot = lambda c, h: c * DN_HEADS + h

    def l2_normalized(x, extra_scale):
        return x * (lax.rsqrt(jnp.sum(x * x, axis=-1, keepdims=True) + EPS) * extra_scale)

    def conv_silu(buf, c, which):
        w = [cw_ref[pl.ds(j, 1), pl.ds(which * DN_WIDTH, DN_WIDTH)] for j in range(DN_CONV_K)]
        cur = buf[pl.ds(c * chunk, chunk + halo), :]
        prev = pltpu.roll(cur, 1, axis=0)
        tail = pltpu.roll(w[1] * cur + w[0] * prev, 2, axis=0)
        return _silu((w[3] * cur + w[2] * prev + tail)[halo:, :])

    qa = [conv_silu(qbuf, c, 0) for c in chunks]
    ka = [conv_silu(kbuf, c, 1) for c in chunks]
    va = [conv_silu(vbuf, c, 2) for c in chunks]
    gts = [gates_ref[pl.ds(c * chunk, chunk), :] for c in chunks]
    gcum = [_mm_exact(lower_incl, g) for g in gts]
    gcum_t = [g.T for g in gcum]
    q = [l2_normalized(qa[c][:, head_cols[h]], scale) for c, h in pairs]
    k = [l2_normalized(ka[c][:, head_cols[h]], 1.0) for c, h in pairs]
    v = [va[c][:, head_cols[h]] for c, h in pairs]
    gc = [gcum[c][:, h:h + 1] for c, h in pairs]
    gr = [gcum_t[c][h:h + 1, :] for c, h in pairs]
    beta = [gts[c][:, DN_HEADS + h:DN_HEADS + h + 1] for c, h in pairs]
    g_last = [g[chunk - 1:chunk, :] for g in gc]
    decay = _each(lambda a, b: jnp.exp(jnp.where(causal, a - b, -1e30)), gc, gr)
    kk = _each(_mm_nt, k, k)
    low = _each(lambda m, d, b: jnp.where(strict, m * d * b, 0.0), kk, decay, beta)
    inv = _unit_lower_inverse(low, row, col)
    eg = _each(jnp.exp, gc)
    rhs = _each(lambda vv, kx, b, e: jnp.concatenate([vv * b, kx * (b * e)], axis=1),
                v, k, beta, eg)
    sol = _each(_mm, inv, rhs)
    qk = _each(lambda a, b, d: _mm_nt(a, b) * d, q, k, decay)
    for (c, h), s, qx, e, a, kx, gl, g in zip(pairs, sol, q, eg, qk, k, g_last, gc):
        i = slot(c, h)
        u_s[i] = s[:, :DN_HEAD_DIM]
        wq_s[i] = jnp.concatenate([s[:, DN_HEAD_DIM:], qx * e], axis=0).astype(BF16)
        qk_s[i] = a.astype(BF16)
        kdt_s[i] = (kx * jnp.exp(gl - g)).T.astype(BF16)
    carry_decay = _each(jnp.exp, g_last)

    state = [state_ref[h] for h in heads]
    for c in chunks:
        ids = [slot(c, h) for h in heads]
        ws_qs = [jnp.dot(wq_s[i], st.astype(BF16), preferred_element_type=F32)
                 for i, st in zip(ids, state)]
        v_new = [(u_s[i] - x[:chunk]).astype(BF16) for i, x in zip(ids, ws_qs)]
        o = [x[chunk:] + jnp.dot(qk_s[i], vn, preferred_element_type=F32)
             for i, x, vn in zip(ids, ws_qs, v_new)]
        state = [st * carry_decay[i] + jnp.dot(kdt_s[i], vn, preferred_element_type=F32)
                 for i, st, vn in zip(ids, state, v_new)]
        rows = pl.ds(c * chunk, chunk)
        for h, sl in enumerate(head_cols):
            y = o[h] * _rms_scale(o[h]) * ng_ref[...] * _silu(z_ref[rows, sl])
            out_ref[rows, sl] = y.astype(out_ref.dtype)
    for h in heads:
        state_ref[h] = state[h]


def _deltanet(proj, gates, conv_w, norm_g, cast_weights, *, seq, ts, chunk):
    m = proj.shape[0]
    tiles = seq // ts
    steps = (m // seq) * tiles
    pairs = (ts // chunk) * DN_HEADS
    col0 = 3 * CONV_WIDTH // DN_WIDTH
    row_map = lambda b, t: b * tiles + t
    qkvz = [pl.BlockSpec((ts, DN_WIDTH), functools.partial(lambda b, t, n: (b * tiles + t, col0 + n), n=n))
            for n in range(4)]
    cast_specs = []
    for w in cast_weights:
        share = 1 if w.shape[0] % (steps * SUBLANES_BF16) == 0 else 2
        assert w.shape[0] % (steps // share * SUBLANES_BF16) == 0, w.shape
        cast_specs.append(pl.BlockSpec(
            (w.shape[0] // (steps // share), w.shape[1]),
            functools.partial(lambda b, t, share: (row_map(b, t) // share, 0), share=share)))
    outs = pl.pallas_call(
        functools.partial(_deltanet_kernel, ts=ts, chunk=chunk, n_cast=len(cast_weights)),
        grid=(m // seq, tiles),
        in_specs=qkvz + [
            pl.BlockSpec((ts, LANES), lambda b, t: (row_map(b, t), 0)),
            pl.BlockSpec((DN_CONV_K, 3 * DN_WIDTH), lambda b, t: (0, 0)),
            pl.BlockSpec((1, DN_HEAD_DIM), lambda b, t: (0, 0)),
        ] + cast_specs,
        out_specs=[pl.BlockSpec((ts, DN_WIDTH), lambda b, t: (row_map(b, t), 0))] + cast_specs,
        out_shape=[jax.ShapeDtypeStruct((m, DN_WIDTH), BF16)]
        + [jax.ShapeDtypeStruct(w.shape, BF16) for w in cast_weights],
        scratch_shapes=[pltpu.VMEM((ts + SUBLANES_F32, DN_WIDTH), F32)] * 3 + [
            pltpu.VMEM((DN_HEADS, DN_HEAD_DIM, DN_HEAD_DIM), F32),
            pltpu.VMEM((pairs, chunk, DN_HEAD_DIM), F32),
            pltpu.VMEM((pairs, 2 * chunk, DN_HEAD_DIM), BF16),
            pltpu.VMEM((pairs, chunk, chunk), BF16),
            pltpu.VMEM((pairs, DN_HEAD_DIM, chunk), BF16),
        ],
        compiler_params=_params(("parallel", "arbitrary")),
        name="deltanet",
    )(proj, proj, proj, proj, gates, conv_w, norm_g, *cast_weights)
    return outs[0], outs[1:]


def _out_proj_kernel(ax_ref, ab_ref, ac_ref, hax_ref, hac_ref, yb_ref, x_ref, cw_ref, w_ref,
                     out_ref, pbuf, *, tm, seq):
    halo = SUBLANES_F32
    acc_b = jnp.dot(yb_ref[...], w_ref[pl.ds(CONV_WIDTH, DN_WIDTH), :], preferred_element_type=F32)
    seq_start = (pl.program_id(0) * tm) % seq == 0
    pbuf[pl.ds(0, halo), :] = jnp.where(seq_start, 0.0, hac_ref[...] * hax_ref[...])
    pbuf[pl.ds(halo, tm), :] = ac_ref[...] * ax_ref[...]
    acc = None
    for j in range(CONV_K):
        term = pbuf[pl.ds(halo - (CONV_K - 1) + j, tm), :] * cw_ref[pl.ds(j, 1), :]
        acc = term if acc is None else acc + term
    ya = (ab_ref[...] * acc).astype(BF16)
    out_ref[...] = (x_ref[...] + acc_b
                    + jnp.dot(ya, w_ref[pl.ds(0, CONV_WIDTH), :], preferred_element_type=F32))


def _out_proj(proj, yb, x2, conv_w, w_out, *, tm, seq):
    m = x2.shape[0]
    halo = SUBLANES_F32
    halo_row = lambda i: jnp.maximum(i * (tm // halo) - 1, 0)
    return pl.pallas_call(
        functools.partial(_out_proj_kernel, tm=tm, seq=seq),
        grid=(m // tm,),
        in_specs=[
            pl.BlockSpec((tm, CONV_WIDTH), lambda i: (i, 0)),
            pl.BlockSpec((tm, CONV_WIDTH), lambda i: (i, 1)),
            pl.BlockSpec((tm, CONV_WIDTH), lambda i: (i, 2)),
            pl.BlockSpec((halo, CONV_WIDTH), lambda i: (halo_row(i), 0)),
            pl.BlockSpec((halo, CONV_WIDTH), lambda i: (halo_row(i), 2)),
            pl.BlockSpec((tm, DN_WIDTH), lambda i: (i, 0)),
            pl.BlockSpec((tm, D_MODEL), lambda i: (i, 0)),
            pl.BlockSpec((CONV_K, CONV_WIDTH), lambda i: (0, 0)),
            pl.BlockSpec((D_MODEL, D_MODEL), lambda i: (0, 0)),
        ],
        out_specs=pl.BlockSpec((tm, D_MODEL), lambda i: (i, 0)),
        out_shape=jax.ShapeDtypeStruct((m, D_MODEL), F32),
        scratch_shapes=[pltpu.VMEM((tm + halo, CONV_WIDTH), F32)],
        compiler_params=_params(("parallel",)),
        name="out_proj",
    )(proj, proj, proj, proj, proj, yb, x2, conv_w, w_out)


def _ffn_kernel(x_ref, g_ref, wg_ref, wv_ref, cg_ref, cv_ref, wd_ref, out_ref,
                hs_ref, ubuf, carry, *, tm, seq):
    halo = SUBLANES_F32
    i = pl.program_id(0)
    j = pl.program_id(1)
    seq_start = (i * tm) % seq == 0

    @pl.when(jnp.logical_and(i == 0, j == 0))
    def _():
        carry[...] = jnp.zeros_like(carry)

    @pl.when(j == 0)
    def _():
        x = x_ref[...]
        hs_ref[...] = (x * _rms_scale(x) * g_ref[...]).astype(BF16)
        out_ref[...] = x

    def up_conv(w_ref, cw_ref, which, cs):
        up = jnp.dot(hs_ref[...], w_ref[:, cs], preferred_element_type=F32)
        ubuf[which, pl.ds(0, halo), cs] = jnp.where(seq_start, 0.0, carry[which, j, :, cs])
        ubuf[which, pl.ds(halo, tm), cs] = up
        carry[which, j, :, cs] = up[tm - halo:, :]
        acc = None
        for t in range(FFN_CONV_K):
            term = (ubuf[which, pl.ds(halo - (FFN_CONV_K - 1) + t, tm), cs]
                    * cw_ref[pl.ds(t, 1), cs])
            acc = term if acc is None else acc + term
        return acc

    bn = wd_ref.shape[0]
    halves = [pl.ds(h * (bn // 2), bn // 2) for h in range(2)]
    acts = []
    for cs in halves:
        gate = up_conv(wg_ref, cg_ref, 0, cs)
        val = up_conv(wv_ref, cv_ref, 1, cs)
        acts.append((_silu(gate) * val).astype(BF16))
    for cs, act in zip(halves, acts):
        out_ref[...] += jnp.dot(act, wd_ref[cs, :], preferred_element_type=F32)


def _ffn(x1, g, w_up, conv_w, w_down, *, tm, bn, seq):
    m = x1.shape[0]
    halo = SUBLANES_F32
    nj = D_FF // bn
    return pl.pallas_call(
        functools.partial(_ffn_kernel, tm=tm, seq=seq),
        grid=(m // tm, nj),
        in_specs=[
            pl.BlockSpec((tm, D_MODEL), lambda i, j: (i, 0)),
            pl.BlockSpec((1, D_MODEL), lambda i, j: (0, 0)),
            pl.BlockSpec((D_MODEL, bn), lambda i, j: (0, j)),
            pl.BlockSpec((D_MODEL, bn), lambda i, j: (0, j + nj)),
            pl.BlockSpec((FFN_CONV_K, bn), lambda i, j: (0, j)),
            pl.BlockSpec((FFN_CONV_K, bn), lambda i, j: (0, j + nj)),
            pl.BlockSpec((bn, D_MODEL), lambda i, j: (j, 0)),
        ],
        out_specs=pl.BlockSpec((tm, D_MODEL), lambda i, j: (i, 0)),
        out_shape=jax.ShapeDtypeStruct((m, D_MODEL), F32),
        scratch_shapes=[pltpu.VMEM((tm, D_MODEL), BF16),
                        pltpu.VMEM((2, tm + halo, bn), F32),
                        pltpu.VMEM((2, nj, halo, bn), F32)],
        compiler_params=_params(("arbitrary", "arbitrary")),
        name="conv_ffn",
    )(x1, g, w_up, w_up, conv_w, conv_w, w_down)


def _ple_kernel(x_ref, p_ref, g_ref, wpg_ref, wpp_ref, fg_ref, out_ref):
    x = x_ref[...]
    hb = (x * _rms_scale(x) * g_ref[...]).astype(BF16)
    gate = jax.nn.sigmoid(jnp.dot(hb, wpg_ref[...], preferred_element_type=F32))
    emb = jnp.dot(p_ref[...].astype(BF16), wpp_ref[...], preferred_element_type=F32)
    y = x + gate * emb
    out_ref[...] = y * _rms_scale(y) * fg_ref[...]


def _ple(x2, p2, g, w_pg, w_pp, final_g, *, tm):
    m = x2.shape[0]
    return pl.pallas_call(
        _ple_kernel,
        grid=(m // tm,),
        in_specs=[
            pl.BlockSpec((tm, D_MODEL), lambda i: (i, 0)),
            pl.BlockSpec((tm, PLE_DIM), lambda i: (i, 0)),
            pl.BlockSpec((1, D_MODEL), lambda i: (0, 0)),
            pl.BlockSpec((D_MODEL, D_MODEL), lambda i: (0, 0), pipeline_mode=pl.Buffered(1)),
            pl.BlockSpec((PLE_DIM, D_MODEL), lambda i: (0, 0), pipeline_mode=pl.Buffered(1)),
            pl.BlockSpec((1, D_MODEL), lambda i: (0, 0)),
        ],
        out_specs=pl.BlockSpec((tm, D_MODEL), lambda i: (i, 0)),
        out_shape=jax.ShapeDtypeStruct((m, D_MODEL), F32),
        compiler_params=_params(("parallel",)),
        name="ple",
    )(x2, p2, g, w_pg, w_pp, final_g)


def _layer(x2, p2, seq, norm_mix_g, w_in, conv_a_w, conv_qkv_w, a_log, dt_bias, dn_norm_g,
           w_out, norm_ffn_g, w_up, conv_ffn_w, w_down, norm_ple_g, w_ple_gate, w_ple_proj,
           out_norm_g):
    row = lambda v: v.reshape(1, -1).astype(F32)
    lane_pad = lambda v: jnp.pad(row(v), ((0, 0), (0, LANES - v.shape[-1])))
    w_in_t = w_in.T
    w_gate = jnp.pad(w_in_t[PROJ_COLS:, :], ((0, LANES - 2 * DN_HEADS), (0, 0))).astype(BF16)

    proj, gates = _in_proj(x2, row(norm_mix_g), w_in_t, w_gate, lane_pad(a_log), lane_pad(dt_bias),
                           tm=IN_TM, tn=IN_TN)
    yb, (w_out_b, w_up_b, w_down_b, w_pg_b) = _deltanet(
        proj, gates, conv_qkv_w.astype(F32), row(dn_norm_g), (w_out, w_up, w_down, w_ple_gate),
        seq=seq, ts=DN_TS, chunk=DN_CHUNK)
    x2 = _out_proj(proj, yb, x2, conv_a_w.astype(F32), w_out_b, tm=OUT_TM, seq=seq)
    x2 = _ffn(x2, row(norm_ffn_g), w_up_b, conv_ffn_w.astype(F32), w_down_b,
              tm=FFN_TM, bn=FFN_BN, seq=seq)
    return _ple(x2, p2, row(norm_ple_g), w_pg_b, w_ple_proj.astype(BF16),
                row(out_norm_g), tm=PLE_TM)


def kernel(x, p, norm_mix_g, w_in, conv_a_w, conv_qkv_w, a_log, dt_bias, dn_norm_g, w_out,
           norm_ffn_g, w_up, conv_ffn_w, w_down, norm_ple_g, w_ple_gate, w_ple_proj, final_norm_g):
    batch, seq, d_model = x.shape
    depth = p.shape[0]
    assert depth == 1 and d_model == D_MODEL
    x2 = x.reshape(batch * seq, d_model)
    p2 = p[0].reshape(batch * seq, PLE_DIM)
    out = _layer(x2, p2, seq, norm_mix_g[0], w_in[0], conv_a_w[0], conv_qkv_w[0], a_log[0],
                 dt_bias[0], dn_norm_g[0], w_out[0], norm_ffn_g[0], w_up[0], conv_ffn_w[0],
                 w_down[0], norm_ple_g[0], w_ple_gate[0], w_ple_proj[0], final_norm_g)
    return out.reshape(batch, seq, d_model)
```

```python
import functools

import jax
import jax.numpy as jnp
from jax import lax
from jax.experimental import pallas as pl
from jax.experimental.pallas import tpu as pltpu

D_MODEL = 2048
CONV_WIDTH = 1024
CONV_K = 3
DN_HEADS = 8
DN_HEAD_DIM = 128
DN_WIDTH = DN_HEADS * DN_HEAD_DIM
DN_CONV_K = 4
D_FF = 5632
FFN_CONV_K = 3
PLE_DIM = 256
EPS = 1e-6
PROJ_COLS = 3 * CONV_WIDTH + 4 * DN_WIDTH

LANES = 128
SUBLANES_F32 = 8
SUBLANES_BF16 = 16
V7X_VMEM_BYTES = 64 * 1024 * 1024
VMEM_LIMIT_BYTES = V7X_VMEM_BYTES - 2 * 1024 * 1024

IN_TM, IN_TN = 1024, 1024
DN_TS = 256
OUT_TM = 512
FFN_TM, FFN_BN = 1024, 512
PLE_TM = 1024

DN_CHUNK = 128
INV_BASE = 8

F32 = jnp.float32
BF16 = jnp.bfloat16


def _mm(a, b):
    return jnp.dot(a.astype(BF16), b.astype(BF16), preferred_element_type=F32)


def _mm_nt(a, b):
    return lax.dot_general(a.astype(BF16), b.astype(BF16), (((1,), (1,)), ((), ())),
                           preferred_element_type=F32)


def _mm_exact(a, b):
    return jnp.dot(a, b, precision=lax.Precision.HIGHEST, preferred_element_type=F32)


def _rms_scale(x):
    return lax.rsqrt(jnp.mean(x * x, axis=-1, keepdims=True) + EPS)


def _silu(x):
    return x * jax.nn.sigmoid(x)


def _params(semantics):
    return pltpu.CompilerParams(dimension_semantics=semantics, vmem_limit_bytes=VMEM_LIMIT_BYTES)


def _norm_and_gates(x_ref, g_ref, wg_ref, alog_ref, dtb_ref, gates_ref, hs_ref):
    x = x_ref[...]
    hb = (x * _rms_scale(x) * g_ref[...]).astype(BF16)
    hs_ref[...] = hb
    raw = lax.dot_general(hb, wg_ref[...], (((1,), (1,)), ((), ())),
                          preferred_element_type=F32)
    a = raw + dtb_ref[...]
    softplus = jnp.maximum(a, 0.0) + jnp.log(1.0 + jnp.exp(-jnp.abs(a)))
    decay = -jnp.exp(alog_ref[...]) * softplus
    lane = lax.broadcasted_iota(jnp.int32, raw.shape, 1)
    gates_ref[...] = jnp.where(lane < DN_HEADS, decay, jax.nn.sigmoid(raw))


def _in_proj_head_kernel(x_ref, g_ref, w_ref, wg_ref, alog_ref, dtb_ref,
                         proj_ref, gates_ref, wb_ref, hs_ref):
    @pl.when(pl.program_id(0) == 0)
    def _():
        _norm_and_gates(x_ref, g_ref, wg_ref, alog_ref, dtb_ref, gates_ref, hs_ref)

    wb = w_ref[...].T.astype(BF16)
    wb_ref[...] = wb
    proj_ref[...] = jnp.dot(hs_ref[...], wb, preferred_element_type=F32)


def _in_proj_kernel(x_ref, g_ref, w_ref, wg_ref, alog_ref, dtb_ref, proj_in, gates_in,
                    proj_ref, gates_ref, hs_ref):
    del proj_in, gates_in
    @pl.when(pl.program_id(1) == 0)
    def _():
        _norm_and_gates(x_ref, g_ref, wg_ref, alog_ref, dtb_ref, gates_ref, hs_ref)

    tn = proj_ref.shape[1]
    for block in range(PROJ_COLS // tn):
        @pl.when(pl.program_id(1) == block)
        def _(block=block):
            proj_ref[...] = jnp.dot(hs_ref[...], w_ref[:, pl.ds(block * tn, tn)],
                                    preferred_element_type=F32)


def _in_proj(x2, g, w_in_t, w_gate, alog_pad, dtb_pad, *, tm, tn):
    m = x2.shape[0]
    n_blocks = PROJ_COLS // tn
    small = [
        pl.BlockSpec((LANES, D_MODEL), lambda *_: (0, 0)),
        pl.BlockSpec((1, LANES), lambda *_: (0, 0)),
        pl.BlockSpec((1, LANES), lambda *_: (0, 0)),
    ]
    out_shape = [jax.ShapeDtypeStruct((m, PROJ_COLS), F32), jax.ShapeDtypeStruct((m, LANES), F32)]
    proj, gates, w_bf16 = pl.pallas_call(
        _in_proj_head_kernel,
        grid=(n_blocks,),
        in_specs=[
            pl.BlockSpec((tm, D_MODEL), lambda j: (0, 0)),
            pl.BlockSpec((1, D_MODEL), lambda j: (0, 0)),
            pl.BlockSpec((tn, D_MODEL), lambda j: (j, 0)),
        ] + small,
        out_specs=[
            pl.BlockSpec((tm, tn), lambda j: (0, j)),
            pl.BlockSpec((tm, LANES), lambda j: (0, 0)),
            pl.BlockSpec((D_MODEL, tn), lambda j: (0, j)),
        ],
        out_shape=out_shape + [jax.ShapeDtypeStruct((D_MODEL, PROJ_COLS), BF16)],
        scratch_shapes=[pltpu.VMEM((tm, D_MODEL), BF16)],
        compiler_params=_params(("arbitrary",)),
        name="in_proj_head",
    )(x2, g, w_in_t, w_gate, alog_pad, dtb_pad)
    return pl.pallas_call(
        _in_proj_kernel,
        grid=(m // tm - 1, n_blocks),
        in_specs=[
            pl.BlockSpec((tm, D_MODEL), lambda i, j: (i + 1, 0)),
            pl.BlockSpec((1, D_MODEL), lambda i, j: (0, 0)),
            pl.BlockSpec(w_bf16.shape, lambda i, j: (0, 0), pipeline_mode=pl.Buffered(1)),
        ] + small + [
            pl.BlockSpec(memory_space=pl.ANY),
            pl.BlockSpec(memory_space=pl.ANY),
        ],
        out_specs=[
            pl.BlockSpec((tm, tn), lambda i, j: (i + 1, j)),
            pl.BlockSpec((tm, LANES), lambda i, j: (i + 1, 0)),
        ],
        out_shape=out_shape,
        input_output_aliases={6: 0, 7: 1},
        scratch_shapes=[pltpu.VMEM((tm, D_MODEL), BF16)],
        compiler_params=_params(("parallel", "arbitrary")),
        name="in_proj",
    )(x2, g, w_bf16, w_gate, alog_pad, dtb_pad, proj, gates)


def _each(fn, *lists):
    return [fn(*args) for args in zip(*lists)]


def _unit_lower_inverse(lows, row, col):
    c = lows[0].shape[0]
    eye = (row == col).astype(F32)
    base_shift = INV_BASE.bit_length() - 1
    base_mask = (row >> base_shift) == (col >> base_shift)
    l0 = [jnp.where(base_mask, low, 0.0) for low in lows]
    l2 = _each(_mm, l0, l0)
    l4 = _each(_mm, l2, l2)
    l3 = _each(_mm, l0, l2)
    p1 = _each(lambda a, b, d: eye - a + b - d, l0, l2, l3)
    inv = _each(lambda p, pl4: p + pl4, p1, _each(_mm, p1, l4))
    shift = base_shift
    while (1 << shift) < c:
        off_mask = (((row >> (shift + 1)) == (col >> (shift + 1)))
                    & ((row >> shift) != (col >> shift)))
        off_inv = _each(lambda low, t: _mm(jnp.where(off_mask, low, 0.0), t), lows, inv)
        inv = _each(lambda t, x: t - _mm(t, x), inv, off_inv)
        shift += 1
    return inv


def _deltanet_kernel(q_ref, k_ref, v_ref, z_ref, gates_ref, cw_ref, ng_ref, *refs,
                     ts, chunk, n_cast):
    cast_src = refs[:n_cast]
    out_ref = refs[n_cast]
    cast_dst = refs[n_cast + 1:2 * n_cast + 1]
    qbuf, kbuf, vbuf, state_ref, u_s, wq_s, qk_s, kdt_s = refs[2 * n_cast + 1:]

    halo = SUBLANES_F32
    first = pl.program_id(1) == 0

    @pl.when(first)
    def _():
        state_ref[...] = jnp.zeros_like(state_ref)
        for buf in (qbuf, kbuf, vbuf):
            buf[pl.ds(0, halo), :] = jnp.zeros((halo, DN_WIDTH), F32)

    @pl.when(jnp.logical_not(first))
    def _():
        for buf in (qbuf, kbuf, vbuf):
            buf[pl.ds(0, halo), :] = buf[pl.ds(ts, halo), :]

    for src, buf in ((q_ref, qbuf), (k_ref, kbuf), (v_ref, vbuf)):
        buf[pl.ds(halo, ts), :] = src[...]
    for src, dst in zip(cast_src, cast_dst):
        dst[...] = src[...].astype(dst.dtype)

    row = lax.broadcasted_iota(jnp.int32, (chunk, chunk), 0)
    col = lax.broadcasted_iota(jnp.int32, (chunk, chunk), 1)
    lower_incl = (row >= col).astype(F32)
    causal = row >= col
    strict = row > col
    scale = DN_HEAD_DIM ** -0.5
    heads = range(DN_HEADS)
    chunks = range(ts // chunk)
    head_cols = [slice(h * DN_HEAD_DIM, (h + 1) * DN_HEAD_DIM) for h in heads]
    pairs = [(c, h) for c in chunks for h in heads]
    slot = lambda c, h: c * DN_HEADS + h

    def l2_normalized(x, extra_scale):
        return x * (lax.rsqrt(jnp.sum(x * x, axis=-1, keepdims=True) + EPS) * extra_scale)

    def conv_silu(buf, c, which):
        w = [cw_ref[pl.ds(j, 1), pl.ds(which * DN_WIDTH, DN_WIDTH)] for j in range(DN_CONV_K)]
        cur = buf[pl.ds(c * chunk, chunk + halo), :]
        prev = pltpu.roll(cur, 1, axis=0)
        tail = pltpu.roll(w[1] * cur + w[0] * prev, 2, axis=0)
        return _silu((w[3] * cur + w[2] * prev + tail)[halo:, :])

    qa = [conv_silu(qbuf, c, 0) for c in chunks]
    ka = [conv_silu(kbuf, c, 1) for c in chunks]
    va = [conv_silu(vbuf, c, 2) for c in chunks]
    gts = [gates_ref[pl.ds(c * chunk, chunk), :] for c in chunks]
    gcum = [_mm_exact(lower_incl, g) for g in gts]
    gcum_t = [g.T for g in gcum]
    q = [l2_normalized(qa[c][:, head_cols[h]], scale) for c, h in pairs]
    k = [l2_normalized(ka[c][:, head_cols[h]], 1.0) for c, h in pairs]
    v = [va[c][:, head_cols[h]] for c, h in pairs]
    gc = [gcum[c][:, h:h + 1] for c, h in pairs]
    gr = [gcum_t[c][h:h + 1, :] for c, h in pairs]
    beta = [gts[c][:, DN_HEADS + h:DN_HEADS + h + 1] for c, h in pairs]
    g_last = [g[chunk - 1:chunk, :] for g in gc]
    decay = _each(lambda a, b: jnp.exp(jnp.where(causal, a - b, -1e30)), gc, gr)
    kk = _each(_mm_nt, k, k)
    low = _each(lambda m, d, b: jnp.where(strict, m * d * b, 0.0), kk, decay, beta)
    inv = _unit_lower_inverse(low, row, col)
    eg = _each(jnp.exp, gc)
    rhs = _each(lambda vv, kx, b, e: jnp.concatenate([vv * b, kx * (b * e)], axis=1),
                v, k, beta, eg)
    sol = _each(_mm, inv, rhs)
    qk = _each(lambda a, b, d: _mm_nt(a, b) * d, q, k, decay)
    for (c, h), s, qx, e, a, kx, gl, g in zip(pairs, sol, q, eg, qk, k, g_last, gc):
        i = slot(c, h)
        u_s[i] = s[:, :DN_HEAD_DIM]
        wq_s[i] = jnp.concatenate([s[:, DN_HEAD_DIM:], qx * e], axis=0).astype(BF16)
        qk_s[i] = a.astype(BF16)
        kdt_s[i] = (kx * jnp.exp(gl - g)).T.astype(BF16)
    carry_decay = _each(jnp.exp, g_last)

    state = [state_ref[h] for h in heads]
    for c in chunks:
        ids = [slot(c, h) for h in heads]
        ws_qs = [jnp.dot(wq_s[i], st.astype(BF16), preferred_element_type=F32)
                 for i, st in zip(ids, state)]
        v_new = [(u_s[i] - x[:chunk]).astype(BF16) for i, x in zip(ids, ws_qs)]
        o = [x[chunk:] + jnp.dot(qk_s[i], vn, preferred_element_type=F32)
             for i, x, vn in zip(ids, ws_qs, v_new)]
        state = [st * carry_decay[i] + jnp.dot(kdt_s[i], vn, preferred_element_type=F32)
                 for i, st, vn in zip(ids, state, v_new)]
        rows = pl.ds(c * chunk, chunk)
        for h, sl in enumerate(head_cols):
            y = o[h] * _rms_scale(o[h]) * ng_ref[...] * _silu(z_ref[rows, sl])
            out_ref[rows, sl] = y.astype(out_ref.dtype)
    for h in heads:
        state_ref[h] = state[h]


def _deltanet(proj, gates, conv_w, norm_g, cast_weights, *, seq, ts, chunk):
    m = proj.shape[0]
    tiles = seq // ts
    steps = (m // seq) * tiles
    pairs = (ts // chunk) * DN_HEADS
    col0 = 3 * CONV_WIDTH // DN_WIDTH
    row_map = lambda b, t: b * tiles + t
    qkvz = [pl.BlockSpec((ts, DN_WIDTH), functools.partial(lambda b, t, n: (b * tiles + t, col0 + n), n=n))
            for n in range(4)]
    cast_specs = []
    for w in cast_weights:
        share = 1 if w.shape[0] % (steps * SUBLANES_BF16) == 0 else 2
        assert w.shape[0] % (steps // share * SUBLANES_BF16) == 0, w.shape
        cast_specs.append(pl.BlockSpec(
            (w.shape[0] // (steps // share), w.shape[1]),
            functools.partial(lambda b, t, share: (row_map(b, t) // share, 0), share=share)))
    outs = pl.pallas_call(
        functools.partial(_deltanet_kernel, ts=ts, chunk=chunk, n_cast=len(cast_weights)),
        grid=(m // seq, tiles),
        in_specs=qkvz + [
            pl.BlockSpec((ts, LANES), lambda b, t: (row_map(b, t), 0)),
            pl.BlockSpec((DN_CONV_K, 3 * DN_WIDTH), lambda b, t: (0, 0)),
            pl.BlockSpec((1, DN_HEAD_DIM), lambda b, t: (0, 0)),
        ] + cast_specs,
        out_specs=[pl.BlockSpec((ts, DN_WIDTH), lambda b, t: (row_map(b, t), 0))] + cast_specs,
        out_shape=[jax.ShapeDtypeStruct((m, DN_WIDTH), BF16)]
        + [jax.ShapeDtypeStruct(w.shape, BF16) for w in cast_weights],
        scratch_shapes=[pltpu.VMEM((ts + SUBLANES_F32, DN_WIDTH), F32)] * 3 + [
            pltpu.VMEM((DN_HEADS, DN_HEAD_DIM, DN_HEAD_DIM), F32),
            pltpu.VMEM((pairs, chunk, DN_HEAD_DIM), F32),
            pltpu.VMEM((pairs, 2 * chunk, DN_HEAD_DIM), BF16),
            pltpu.VMEM((pairs, chunk, chunk), BF16),
            pltpu.VMEM((pairs, DN_HEAD_DIM, chunk), BF16),
        ],
        compiler_params=_params(("parallel", "arbitrary")),
        name="deltanet",
    )(proj, proj, proj, proj, gates, conv_w, norm_g, *cast_weights)
    return outs[0], outs[1:]


def _out_proj_kernel(ax_ref, ab_ref, ac_ref, hax_ref, hac_ref, yb_ref, x_ref, cw_ref, w_ref,
                     out_ref, pbuf, *, tm, seq):
    halo = SUBLANES_F32
    acc_b = jnp.dot(yb_ref[...], w_ref[pl.ds(CONV_WIDTH, DN_WIDTH), :], preferred_element_type=F32)
    seq_start = (pl.program_id(0) * tm) % seq == 0
    pbuf[pl.ds(0, halo), :] = jnp.where(seq_start, 0.0, hac_ref[...] * hax_ref[...])
    pbuf[pl.ds(halo, tm), :] = ac_ref[...] * ax_ref[...]
    acc = None
    for j in range(CONV_K):
        term = pbuf[pl.ds(halo - (CONV_K - 1) + j, tm), :] * cw_ref[pl.ds(j, 1), :]
        acc = term if acc is None else acc + term
    ya = (ab_ref[...] * acc).astype(BF16)
    out_ref[...] = (x_ref[...] + acc_b
                    + jnp.dot(ya, w_ref[pl.ds(0, CONV_WIDTH), :], preferred_element_type=F32))


def _out_proj(proj, yb, x2, conv_w, w_out, *, tm, seq):
    m = x2.shape[0]
    halo = SUBLANES_F32
    halo_row = lambda i: jnp.maximum(i * (tm // halo) - 1, 0)
    return pl.pallas_call(
        functools.partial(_out_proj_kernel, tm=tm, seq=seq),
        grid=(m // tm,),
        in_specs=[
            pl.BlockSpec((tm, CONV_WIDTH), lambda i: (i, 0)),
            pl.BlockSpec((tm, CONV_WIDTH), lambda i: (i, 1)),
            pl.BlockSpec((tm, CONV_WIDTH), lambda i: (i, 2)),
            pl.BlockSpec((halo, CONV_WIDTH), lambda i: (halo_row(i), 0)),
            pl.BlockSpec((halo, CONV_WIDTH), lambda i: (halo_row(i), 2)),
            pl.BlockSpec((tm, DN_WIDTH), lambda i: (i, 0)),
            pl.BlockSpec((tm, D_MODEL), lambda i: (i, 0)),
            pl.BlockSpec((CONV_K, CONV_WIDTH), lambda i: (0, 0)),
            pl.BlockSpec((D_MODEL, D_MODEL), lambda i: (0, 0)),
        ],
        out_specs=pl.BlockSpec((tm, D_MODEL), lambda i: (i, 0)),
        out_shape=jax.ShapeDtypeStruct((m, D_MODEL), F32),
        scratch_shapes=[pltpu.VMEM((tm + halo, CONV_WIDTH), F32)],
        compiler_params=_params(("parallel",)),
        name="out_proj",
    )(proj, proj, proj, proj, proj, yb, x2, conv_w, w_out)


def _ffn_kernel(x_ref, g_ref, wg_ref, wv_ref, cg_ref, cv_ref, wd_ref, out_ref,
                hs_ref, ubuf, carry, *, tm, seq):
    halo = SUBLANES_F32
    i = pl.program_id(0)
    j = pl.program_id(1)
    seq_start = (i * tm) % seq == 0

    @pl.when(jnp.logical_and(i == 0, j == 0))
    def _():
        carry[...] = jnp.zeros_like(carry)

    @pl.when(j == 0)
    def _():
        x = x_ref[...]
        hs_ref[...] = (x * _rms_scale(x) * g_ref[...]).astype(BF16)
        out_ref[...] = x

    def up_conv(w_ref, cw_ref, which, cs):
        up = jnp.dot(hs_ref[...], w_ref[:, cs], preferred_element_type=F32)
        ubuf[which, pl.ds(0, halo), cs] = jnp.where(seq_start, 0.0, carry[which, j, :, cs])
        ubuf[which, pl.ds(halo, tm), cs] = up
        carry[which, j, :, cs] = up[tm - halo:, :]
        acc = None
        for t in range(FFN_CONV_K):
            term = (ubuf[which, pl.ds(halo - (FFN_CONV_K - 1) + t, tm), cs]
                    * cw_ref[pl.ds(t, 1), cs])
            acc = term if acc is None else acc + term
        return acc

    bn = wd_ref.shape[0]
    halves = [pl.ds(h * (bn // 2), bn // 2) for h in range(2)]
    gates = [_silu(up_conv(wg_ref, cg_ref, 0, cs)) for cs in halves]
    acts = [(g * up_conv(wv_ref, cv_ref, 1, cs)).astype(BF16) for g, cs in zip(gates, halves)]
    for cs, act in zip(halves, acts):
        out_ref[...] += jnp.dot(act, wd_ref[cs, :], preferred_element_type=F32)


def _ffn(x1, g, w_up, conv_w, w_down, *, tm, bn, seq):
    m = x1.shape[0]
    halo = SUBLANES_F32
    nj = D_FF // bn
    return pl.pallas_call(
        functools.partial(_ffn_kernel, tm=tm, seq=seq),
        grid=(m // tm, nj),
        in_specs=[
            pl.BlockSpec((tm, D_MODEL), lambda i, j: (i, 0)),
            pl.BlockSpec((1, D_MODEL), lambda i, j: (0, 0)),
            pl.BlockSpec((D_MODEL, bn), lambda i, j: (0, j)),
            pl.BlockSpec((D_MODEL, bn), lambda i, j: (0, j + nj)),
            pl.BlockSpec((FFN_CONV_K, bn), lambda i, j: (0, j)),
            pl.BlockSpec((FFN_CONV_K, bn), lambda i, j: (0, j + nj)),
            pl.BlockSpec((bn, D_MODEL), lambda i, j: (j, 0)),
        ],
        out_specs=pl.BlockSpec((tm, D_MODEL), lambda i, j: (i, 0)),
        out_shape=jax.ShapeDtypeStruct((m, D_MODEL), F32),
        scratch_shapes=[pltpu.VMEM((tm, D_MODEL), BF16),
                        pltpu.VMEM((2, tm + halo, bn), F32),
                        pltpu.VMEM((2, nj, halo, bn), F32)],
        compiler_params=_params(("arbitrary", "arbitrary")),
        name="conv_ffn",
    )(x1, g, w_up, w_up, conv_w, conv_w, w_down)


def _ple_kernel(x_ref, p_ref, g_ref, wpg_ref, wpp_ref, fg_ref, out_ref):
    x = x_ref[...]
    hb = (x * _rms_scale(x) * g_ref[...]).astype(BF16)
    gate = jax.nn.sigmoid(jnp.dot(hb, wpg_ref[...], preferred_element_type=F32))
    emb = jnp.dot(p_ref[...].astype(BF16), wpp_ref[...], preferred_element_type=F32)
    y = x + gate * emb
    out_ref[...] = y * _rms_scale(y) * fg_ref[...]


def _ple(x2, p2, g, w_pg, w_pp, final_g, *, tm):
    m = x2.shape[0]
    return pl.pallas_call(
        _ple_kernel,
        grid=(m // tm,),
        in_specs=[
            pl.BlockSpec((tm, D_MODEL), lambda i: (i, 0)),
            pl.BlockSpec((tm, PLE_DIM), lambda i: (i, 0)),
            pl.BlockSpec((1, D_MODEL), lambda i: (0, 0)),
            pl.BlockSpec((D_MODEL, D_MODEL), lambda i: (0, 0), pipeline_mode=pl.Buffered(1)),
            pl.BlockSpec((PLE_DIM, D_MODEL), lambda i: (0, 0), pipeline_mode=pl.Buffered(1)),
            pl.BlockSpec((1, D_MODEL), lambda i: (0, 0)),
        ],
        out_specs=pl.BlockSpec((tm, D_MODEL), lambda i: (i, 0)),
        out_shape=jax.ShapeDtypeStruct((m, D_MODEL), F32),
        compiler_params=_params(("parallel",)),
        name="ple",
    )(x2, p2, g, w_pg, w_pp, final_g)


def _layer(x2, p2, seq, norm_mix_g, w_in, conv_a_w, conv_qkv_w, a_log, dt_bias, dn_norm_g,
           w_out, norm_ffn_g, w_up, conv_ffn_w, w_down, norm_ple_g, w_ple_gate, w_ple_proj,
           out_norm_g):
    row = lambda v: v.reshape(1, -1).astype(F32)
    lane_pad = lambda v: jnp.pad(row(v), ((0, 0), (0, LANES - v.shape[-1])))
    w_in_t = w_in.T
    w_gate = jnp.pad(w_in_t[PROJ_COLS:, :], ((0, LANES - 2 * DN_HEADS), (0, 0))).astype(BF16)

    proj, gates = _in_proj(x2, row(norm_mix_g), w_in_t, w_gate, lane_pad(a_log), lane_pad(dt_bias),
                           tm=IN_TM, tn=IN_TN)
    yb, (w_out_b, w_up_b, w_down_b, w_pg_b) = _deltanet(
        proj, gates, conv_qkv_w.astype(F32), row(dn_norm_g), (w_out, w_up, w_down, w_ple_gate),
        seq=seq, ts=DN_TS, chunk=DN_CHUNK)
    x2 = _out_proj(proj, yb, x2, conv_a_w.astype(F32), w_out_b, tm=OUT_TM, seq=seq)
    x2 = _ffn(x2, row(norm_ffn_g), w_up_b, conv_ffn_w.astype(F32), w_down_b,
              tm=FFN_TM, bn=FFN_BN, seq=seq)
    return _ple(x2, p2, row(norm_ple_g), w_pg_b, w_ple_proj.astype(BF16),
                row(out_norm_g), tm=PLE_TM)


def kernel(x, p, norm_mix_g, w_in, conv_a_w, conv_qkv_w, a_log, dt_bias, dn_norm_g, w_out,
           norm_ffn_g, w_up, conv_ffn_w, w_down, norm_ple_g, w_ple_gate, w_ple_proj, final_norm_g):
    batch, seq, d_model = x.shape
    depth = p.shape[0]
    assert depth == 1 and d_model == D_MODEL
    x2 = x.reshape(batch * seq, d_model)
    p2 = p[0].reshape(batch * seq, PLE_DIM)
    out = _layer(x2, p2, seq, norm_mix_g[0], w_in[0], conv_a_w[0], conv_qkv_w[0], a_log[0],
                 dt_bias[0], dn_norm_g[0], w_out[0], norm_ffn_g[0], w_up[0], conv_ffn_w[0],
                 w_down[0], norm_ple_g[0], w_ple_gate[0], w_ple_proj[0], final_norm_g)
    return out.reshape(batch, seq, d_model)
```

```python
import functools

import jax
import jax.numpy as jnp
from jax import lax
from jax.experimental import pallas as pl
from jax.experimental.pallas import tpu as pltpu

D_MODEL = 2048
CONV_WIDTH = 1024
CONV_K = 3
DN_HEADS = 8
DN_HEAD_DIM = 128
DN_WIDTH = DN_HEADS * DN_HEAD_DIM
DN_CONV_K = 4
D_FF = 5632
FFN_CONV_K = 3
PLE_DIM = 256
EPS = 1e-6
PROJ_COLS = 3 * CONV_WIDTH + 4 * DN_WIDTH

LANES = 128
SUBLANES_F32 = 8
SUBLANES_BF16 = 16
V7X_VMEM_BYTES = 64 * 1024 * 1024
VMEM_LIMIT_BYTES = V7X_VMEM_BYTES - 2 * 1024 * 1024

IN_TM, IN_TN = 1024, 1024
DN_TS = 256
OUT_TM = 512
FFN_TM, FFN_BN = 1024, 512
PLE_TM = 512

DN_CHUNK = 128
INV_BASE = 8

F32 = jnp.float32
BF16 = jnp.bfloat16


def _mm(a, b):
    return jnp.dot(a.astype(BF16), b.astype(BF16), preferred_element_type=F32)


def _mm_nt(a, b):
    return lax.dot_general(a.astype(BF16), b.astype(BF16), (((1,), (1,)), ((), ())),
                           preferred_element_type=F32)


def _mm_exact(a, b):
    return jnp.dot(a, b, precision=lax.Precision.HIGHEST, preferred_element_type=F32)


def _rms_scale(x):
    return lax.rsqrt(jnp.mean(x * x, axis=-1, keepdims=True) + EPS)


def _silu(x):
    return x * jax.nn.sigmoid(x)


def _params(semantics):
    return pltpu.CompilerParams(dimension_semantics=semantics, vmem_limit_bytes=VMEM_LIMIT_BYTES)


def _in_proj_kernel(x_ref, g_ref, w_ref, wg_ref, alog_ref, dtb_ref, proj_ref, gates_ref, hs_ref):
    @pl.when(pl.program_id(1) == 0)
    def _():
        x = x_ref[...]
        hb = (x * _rms_scale(x) * g_ref[...]).astype(BF16)
        hs_ref[...] = hb
        raw = jnp.dot(hb, wg_ref[...], preferred_element_type=F32)
        a = raw + dtb_ref[...]
        softplus = jnp.maximum(a, 0.0) + jnp.log(1.0 + jnp.exp(-jnp.abs(a)))
        decay = -jnp.exp(alog_ref[...]) * softplus
        lane = lax.broadcasted_iota(jnp.int32, raw.shape, 1)
        gates_ref[...] = jnp.where(lane < DN_HEADS, decay, jax.nn.sigmoid(raw))

    tn = proj_ref.shape[1]
    for block in range(PROJ_COLS // tn):
        @pl.when(pl.program_id(1) == block)
        def _(block=block):
            proj_ref[...] = jnp.dot(hs_ref[...], w_ref[:, pl.ds(block * tn, tn)],
                                    preferred_element_type=F32)


def _in_proj(x2, g, w_main, w_gate, alog_pad, dtb_pad, *, tm, tn):
    m = x2.shape[0]
    return pl.pallas_call(
        _in_proj_kernel,
        grid=(m // tm, PROJ_COLS // tn),
        in_specs=[
            pl.BlockSpec((tm, D_MODEL), lambda i, j: (i, 0)),
            pl.BlockSpec((1, D_MODEL), lambda i, j: (0, 0)),
            pl.BlockSpec(w_main.shape, lambda i, j: (0, 0), pipeline_mode=pl.Buffered(1)),
            pl.BlockSpec((D_MODEL, LANES), lambda i, j: (0, 0)),
            pl.BlockSpec((1, LANES), lambda i, j: (0, 0)),
            pl.BlockSpec((1, LANES), lambda i, j: (0, 0)),
        ],
        out_specs=[
            pl.BlockSpec((tm, tn), lambda i, j: (i, j)),
            pl.BlockSpec((tm, LANES), lambda i, j: (i, 0)),
        ],
        out_shape=[
            jax.ShapeDtypeStruct((m, PROJ_COLS), F32),
            jax.ShapeDtypeStruct((m, LANES), F32),
        ],
        scratch_shapes=[pltpu.VMEM((tm, D_MODEL), BF16)],
        compiler_params=_params(("parallel", "arbitrary")),
        name="in_proj",
    )(x2, g, w_main, w_gate, alog_pad, dtb_pad)


def _each(fn, *lists):
    return [fn(*args) for args in zip(*lists)]


def _unit_lower_inverse(lows, row, col):
    c = lows[0].shape[0]
    eye = (row == col).astype(F32)
    base_shift = INV_BASE.bit_length() - 1
    base_mask = (row >> base_shift) == (col >> base_shift)
    l0 = [jnp.where(base_mask, low, 0.0) for low in lows]
    l2 = _each(_mm, l0, l0)
    l4 = _each(_mm, l2, l2)
    l3 = _each(_mm, l0, l2)
    p1 = _each(lambda a, b, d: eye - a + b - d, l0, l2, l3)
    inv = _each(lambda p, pl4: p + pl4, p1, _each(_mm, p1, l4))
    shift = base_shift
    while (1 << shift) < c:
        off_mask = (((row >> (shift + 1)) == (col >> (shift + 1)))
                    & ((row >> shift) != (col >> shift)))
        off_inv = _each(lambda low, t: _mm(jnp.where(off_mask, low, 0.0), t), lows, inv)
        inv = _each(lambda t, x: t - _mm(t, x), inv, off_inv)
        shift += 1
    return inv


def _deltanet_kernel(q_ref, k_ref, v_ref, z_ref, gates_ref, cw_ref, ng_ref, *refs,
                     ts, chunk, n_cast):
    cast_src = refs[:n_cast]
    out_ref = refs[n_cast]
    cast_dst = refs[n_cast + 1:2 * n_cast + 1]
    qbuf, kbuf, vbuf, state_ref, u_s, wq_s, qk_s, kdt_s = refs[2 * n_cast + 1:]

    halo = SUBLANES_F32
    first = pl.program_id(1) == 0

    @pl.when(first)
    def _():
        state_ref[...] = jnp.zeros_like(state_ref)
        for buf in (qbuf, kbuf, vbuf):
            buf[pl.ds(0, halo), :] = jnp.zeros((halo, DN_WIDTH), F32)

    @pl.when(jnp.logical_not(first))
    def _():
        for buf in (qbuf, kbuf, vbuf):
            buf[pl.ds(0, halo), :] = buf[pl.ds(ts, halo), :]

    for src, buf in ((q_ref, qbuf), (k_ref, kbuf), (v_ref, vbuf)):
        buf[pl.ds(halo, ts), :] = src[...]
    for src, dst in zip(cast_src, cast_dst):
        dst[...] = src[...].astype(dst.dtype)

    row = lax.broadcasted_iota(jnp.int32, (chunk, chunk), 0)
    col = lax.broadcasted_iota(jnp.int32, (chunk, chunk), 1)
    lower_incl = (row >= col).astype(F32)
    causal = row >= col
    strict = row > col
    scale = DN_HEAD_DIM ** -0.5
    heads = range(DN_HEADS)
    chunks = range(ts // chunk)
    head_cols = [slice(h * DN_HEAD_DIM, (h + 1) * DN_HEAD_DIM) for h in heads]
    pairs = [(c, h) for c in chunks for h in heads]
    slot = lambda c, h: c * DN_HEADS + h

    def l2_normalized(x, extra_scale):
        return x * (lax.rsqrt(jnp.sum(x * x, axis=-1, keepdims=True) + EPS) * extra_scale)

    def conv_silu(buf, c, which):
        w = [cw_ref[pl.ds(j, 1), pl.ds(which * DN_WIDTH, DN_WIDTH)] for j in range(DN_CONV_K)]
        cur = buf[pl.ds(c * chunk, chunk + halo), :]
        prev = pltpu.roll(cur, 1, axis=0)
        tail = pltpu.roll(w[1] * cur + w[0] * prev, 2, axis=0)
        return _silu((w[3] * cur + w[2] * prev + tail)[halo:, :])

    qa = [conv_silu(qbuf, c, 0) for c in chunks]
    ka = [conv_silu(kbuf, c, 1) for c in chunks]
    va = [conv_silu(vbuf, c, 2) for c in chunks]
    gts = [gates_ref[pl.ds(c * chunk, chunk), :] for c in chunks]
    gcum = [_mm_exact(lower_incl, g) for g in gts]
    gcum_t = [g.T for g in gcum]
    q = [l2_normalized(qa[c][:, head_cols[h]], scale) for c, h in pairs]
    k = [l2_normalized(ka[c][:, head_cols[h]], 1.0) for c, h in pairs]
    v = [va[c][:, head_cols[h]] for c, h in pairs]
    gc = [gcum[c][:, h:h + 1] for c, h in pairs]
    gr = [gcum_t[c][h:h + 1, :] for c, h in pairs]
    beta = [gts[c][:, DN_HEADS + h:DN_HEADS + h + 1] for c, h in pairs]
    g_last = [g[chunk - 1:chunk, :] for g in gc]
    decay = _each(lambda a, b: jnp.exp(jnp.where(causal, a - b, -1e30)), gc, gr)
    kk = _each(_mm_nt, k, k)
    low = _each(lambda m, d, b: jnp.where(strict, m * d * b, 0.0), kk, decay, beta)
    inv = _unit_lower_inverse(low, row, col)
    eg = _each(jnp.exp, gc)
    rhs = _each(lambda vv, kx, b, e: jnp.concatenate([vv * b, kx * (b * e)], axis=1),
                v, k, beta, eg)
    sol = _each(_mm, inv, rhs)
    qk = _each(lambda a, b, d: _mm_nt(a, b) * d, q, k, decay)
    for (c, h), s, qx, e, a, kx, gl, g in zip(pairs, sol, q, eg, qk, k, g_last, gc):
        i = slot(c, h)
        u_s[i] = s[:, :DN_HEAD_DIM]
        wq_s[i] = jnp.concatenate([s[:, DN_HEAD_DIM:], qx * e], axis=0).astype(BF16)
        qk_s[i] = a.astype(BF16)
        kdt_s[i] = (kx * jnp.exp(gl - g)).T.astype(BF16)
    carry_decay = _each(jnp.exp, g_last)

    state = [state_ref[h] for h in heads]
    for c in chunks:
        ids = [slot(c, h) for h in heads]
        ws_qs = [jnp.dot(wq_s[i], st.astype(BF16), preferred_element_type=F32)
                 for i, st in zip(ids, state)]
        v_new = [(u_s[i] - x[:chunk]).astype(BF16) for i, x in zip(ids, ws_qs)]
        o = [x[chunk:] + jnp.dot(qk_s[i], vn, preferred_element_type=F32)
             for i, x, vn in zip(ids, ws_qs, v_new)]
        state = [st * carry_decay[i] + jnp.dot(kdt_s[i], vn, preferred_element_type=F32)
                 for i, st, vn in zip(ids, state, v_new)]
        rows = pl.ds(c * chunk, chunk)
        for h, sl in enumerate(head_cols):
            y = o[h] * _rms_scale(o[h]) * ng_ref[...] * _silu(z_ref[rows, sl])
            out_ref[rows, sl] = y.astype(out_ref.dtype)
    for h in heads:
        state_ref[h] = state[h]


def _deltanet(proj, gates, conv_w, norm_g, cast_weights, *, seq, ts, chunk):
    m = proj.shape[0]
    tiles = seq // ts
    steps = (m // seq) * tiles
    pairs = (ts // chunk) * DN_HEADS
    col0 = 3 * CONV_WIDTH // DN_WIDTH
    row_map = lambda b, t: b * tiles + t
    qkvz = [pl.BlockSpec((ts, DN_WIDTH), functools.partial(lambda b, t, n: (b * tiles + t, col0 + n), n=n))
            for n in range(4)]
    cast_specs = []
    for w in cast_weights:
        share = 1 if w.shape[0] % (steps * SUBLANES_BF16) == 0 else 2
        assert w.shape[0] % (steps // share * SUBLANES_BF16) == 0, w.shape
        cast_specs.append(pl.BlockSpec(
            (w.shape[0] // (steps // share), w.shape[1]),
            functools.partial(lambda b, t, share: (row_map(b, t) // share, 0), share=share)))
    outs = pl.pallas_call(
        functools.partial(_deltanet_kernel, ts=ts, chunk=chunk, n_cast=len(cast_weights)),
        grid=(m // seq, tiles),
        in_specs=qkvz + [
            pl.BlockSpec((ts, LANES), lambda b, t: (row_map(b, t), 0)),
            pl.BlockSpec((DN_CONV_K, 3 * DN_WIDTH), lambda b, t: (0, 0)),
            pl.BlockSpec((1, DN_HEAD_DIM), lambda b, t: (0, 0)),
        ] + cast_specs,
        out_specs=[pl.BlockSpec((ts, DN_WIDTH), lambda b, t: (row_map(b, t), 0))] + cast_specs,
        out_shape=[jax.ShapeDtypeStruct((m, DN_WIDTH), BF16)]
        + [jax.ShapeDtypeStruct(w.shape, BF16) for w in cast_weights],
        scratch_shapes=[pltpu.VMEM((ts + SUBLANES_F32, DN_WIDTH), F32)] * 3 + [
            pltpu.VMEM((DN_HEADS, DN_HEAD_DIM, DN_HEAD_DIM), F32),
            pltpu.VMEM((pairs, chunk, DN_HEAD_DIM), F32),
            pltpu.VMEM((pairs, 2 * chunk, DN_HEAD_DIM), BF16),
            pltpu.VMEM((pairs, chunk, chunk), BF16),
            pltpu.VMEM((pairs, DN_HEAD_DIM, chunk), BF16),
        ],
        compiler_params=_params(("parallel", "arbitrary")),
        name="deltanet",
    )(proj, proj, proj, proj, gates, conv_w, norm_g, *cast_weights)
    return outs[0], outs[1:]


def _out_proj_kernel(ax_ref, ab_ref, ac_ref, hax_ref, hac_ref, yb_ref, x_ref, cw_ref, w_ref,
                     out_ref, pbuf, *, tm, seq):
    halo = SUBLANES_F32
    acc_b = jnp.dot(yb_ref[...], w_ref[pl.ds(CONV_WIDTH, DN_WIDTH), :], preferred_element_type=F32)
    seq_start = (pl.program_id(0) * tm) % seq == 0
    pbuf[pl.ds(0, halo), :] = jnp.where(seq_start, 0.0, hac_ref[...] * hax_ref[...])
    pbuf[pl.ds(halo, tm), :] = ac_ref[...] * ax_ref[...]
    acc = None
    for j in range(CONV_K):
        term = pbuf[pl.ds(halo - (CONV_K - 1) + j, tm), :] * cw_ref[pl.ds(j, 1), :]
        acc = term if acc is None else acc + term
    ya = (ab_ref[...] * acc).astype(BF16)
    out_ref[...] = (x_ref[...] + acc_b
                    + jnp.dot(ya, w_ref[pl.ds(0, CONV_WIDTH), :], preferred_element_type=F32))


def _out_proj(proj, yb, x2, conv_w, w_out, *, tm, seq):
    m = x2.shape[0]
    halo = SUBLANES_F32
    halo_row = lambda i: jnp.maximum(i * (tm // halo) - 1, 0)
    return pl.pallas_call(
        functools.partial(_out_proj_kernel, tm=tm, seq=seq),
        grid=(m // tm,),
        in_specs=[
            pl.BlockSpec((tm, CONV_WIDTH), lambda i: (i, 0)),
            pl.BlockSpec((tm, CONV_WIDTH), lambda i: (i, 1)),
            pl.BlockSpec((tm, CONV_WIDTH), lambda i: (i, 2)),
            pl.BlockSpec((halo, CONV_WIDTH), lambda i: (halo_row(i), 0)),
            pl.BlockSpec((halo, CONV_WIDTH), lambda i: (halo_row(i), 2)),
            pl.BlockSpec((tm, DN_WIDTH), lambda i: (i, 0)),
            pl.BlockSpec((tm, D_MODEL), lambda i: (i, 0)),
            pl.BlockSpec((CONV_K, CONV_WIDTH), lambda i: (0, 0)),
            pl.BlockSpec((D_MODEL, D_MODEL), lambda i: (0, 0)),
        ],
        out_specs=pl.BlockSpec((tm, D_MODEL), lambda i: (i, 0)),
        out_shape=jax.ShapeDtypeStruct((m, D_MODEL), F32),
        scratch_shapes=[pltpu.VMEM((tm + halo, CONV_WIDTH), F32)],
        compiler_params=_params(("parallel",)),
        name="out_proj",
    )(proj, proj, proj, proj, proj, yb, x2, conv_w, w_out)


def _ffn_kernel(x_ref, g_ref, wg_ref, wv_ref, cg_ref, cv_ref, wd_ref, out_ref,
                hs_ref, ubuf, carry, *, tm, seq):
    halo = SUBLANES_F32
    i = pl.program_id(0)
    j = pl.program_id(1)
    seq_start = (i * tm) % seq == 0

    @pl.when(jnp.logical_and(i == 0, j == 0))
    def _():
        carry[...] = jnp.zeros_like(carry)

    @pl.when(j == 0)
    def _():
        x = x_ref[...]
        hs_ref[...] = (x * _rms_scale(x) * g_ref[...]).astype(BF16)
        out_ref[...] = x

    def up_conv(w_ref, cw_ref, which, cs):
        up = jnp.dot(hs_ref[...], w_ref[:, cs], preferred_element_type=F32)
        ubuf[which, pl.ds(0, halo), cs] = jnp.where(seq_start, 0.0, carry[which, j, :, cs])
        ubuf[which, pl.ds(halo, tm), cs] = up
        carry[which, j, :, cs] = up[tm - halo:, :]
        acc = None
        for t in range(FFN_CONV_K):
            term = (ubuf[which, pl.ds(halo - (FFN_CONV_K - 1) + t, tm), cs]
                    * cw_ref[pl.ds(t, 1), cs])
            acc = term if acc is None else acc + term
        return acc

    bn = wd_ref.shape[0]
    halves = [pl.ds(h * (bn // 2), bn // 2) for h in range(2)]
    gates = [_silu(up_conv(wg_ref, cg_ref, 0, cs)) for cs in halves]
    acts = [(g * up_conv(wv_ref, cv_ref, 1, cs)).astype(BF16) for g, cs in zip(gates, halves)]
    for cs, act in zip(halves, acts):
        out_ref[...] += jnp.dot(act, wd_ref[cs, :], preferred_element_type=F32)


def _ffn(x1, g, w_up, conv_w, w_down, *, tm, bn, seq):
    m = x1.shape[0]
    halo = SUBLANES_F32
    nj = D_FF // bn
    return pl.pallas_call(
        functools.partial(_ffn_kernel, tm=tm, seq=seq),
        grid=(m // tm, nj),
        in_specs=[
            pl.BlockSpec((tm, D_MODEL), lambda i, j: (i, 0)),
            pl.BlockSpec((1, D_MODEL), lambda i, j: (0, 0)),
            pl.BlockSpec((D_MODEL, bn), lambda i, j: (0, j)),
            pl.BlockSpec((D_MODEL, bn), lambda i, j: (0, j + nj)),
            pl.BlockSpec((FFN_CONV_K, bn), lambda i, j: (0, j)),
            pl.BlockSpec((FFN_CONV_K, bn), lambda i, j: (0, j + nj)),
            pl.BlockSpec((bn, D_MODEL), lambda i, j: (j, 0)),
        ],
        out_specs=pl.BlockSpec((tm, D_MODEL), lambda i, j: (i, 0)),
        out_shape=jax.ShapeDtypeStruct((m, D_MODEL), F32),
        scratch_shapes=[pltpu.VMEM((tm, D_MODEL), BF16),
                        pltpu.VMEM((2, tm + halo, bn), F32),
                        pltpu.VMEM((2, nj, halo, bn), F32)],
        compiler_params=_params(("arbitrary", "arbitrary")),
        name="conv_ffn",
    )(x1, g, w_up, w_up, conv_w, conv_w, w_down)


def _ple_kernel(x_ref, p_ref, g_ref, wpg_ref, wpp_ref, fg_ref, out_ref):
    x = x_ref[...]
    hb = (x * _rms_scale(x) * g_ref[...]).astype(BF16)
    gate = jax.nn.sigmoid(jnp.dot(hb, wpg_ref[...], preferred_element_type=F32))
    emb = jnp.dot(p_ref[...].astype(BF16), wpp_ref[...], preferred_element_type=F32)
    y = x + gate * emb
    out_ref[...] = y * _rms_scale(y) * fg_ref[...]


def _ple(x2, p2, g, w_pg, w_pp, final_g, *, tm):
    m = x2.shape[0]
    return pl.pallas_call(
        _ple_kernel,
        grid=(m // tm,),
        in_specs=[
            pl.BlockSpec((tm, D_MODEL), lambda i: (i, 0)),
            pl.BlockSpec((tm, PLE_DIM), lambda i: (i, 0)),
            pl.BlockSpec((1, D_MODEL), lambda i: (0, 0)),
            pl.BlockSpec((D_MODEL, D_MODEL), lambda i: (0, 0)),
            pl.BlockSpec((PLE_DIM, D_MODEL), lambda i: (0, 0)),
            pl.BlockSpec((1, D_MODEL), lambda i: (0, 0)),
        ],
        out_specs=pl.BlockSpec((tm, D_MODEL), lambda i: (i, 0)),
        out_shape=jax.ShapeDtypeStruct((m, D_MODEL), F32),
        compiler_params=_params(("parallel",)),
        name="ple",
    )(x2, p2, g, w_pg, w_pp, final_g)


def _layer(x2, p2, seq, norm_mix_g, w_in, conv_a_w, conv_qkv_w, a_log, dt_bias, dn_norm_g,
           w_out, norm_ffn_g, w_up, conv_ffn_w, w_down, norm_ple_g, w_ple_gate, w_ple_proj,
           out_norm_g):
    row = lambda v: v.reshape(1, -1).astype(F32)
    lane_pad = lambda v: jnp.pad(row(v), ((0, 0), (0, LANES - v.shape[-1])))
    w_main = w_in.astype(BF16)
    w_gate = jnp.pad(w_in[:, PROJ_COLS:], ((0, 0), (0, LANES - 2 * DN_HEADS))).astype(BF16)

    proj, gates = _in_proj(x2, row(norm_mix_g), w_main, w_gate, lane_pad(a_log), lane_pad(dt_bias),
                           tm=IN_TM, tn=IN_TN)
    yb, (w_out_b, w_up_b, w_down_b, w_pg_b) = _deltanet(
        proj, gates, conv_qkv_w.astype(F32), row(dn_norm_g), (w_out, w_up, w_down, w_ple_gate),
        seq=seq, ts=DN_TS, chunk=DN_CHUNK)
    x2 = _out_proj(proj, yb, x2, conv_a_w.astype(F32), w_out_b, tm=OUT_TM, seq=seq)
    x2 = _ffn(x2, row(norm_ffn_g), w_up_b, conv_ffn_w.astype(F32), w_down_b,
              tm=FFN_TM, bn=FFN_BN, seq=seq)
    return _ple(x2, p2, row(norm_ple_g), w_pg_b, w_ple_proj.astype(BF16),
                row(out_norm_g), tm=PLE_TM)


def kernel(x, p, norm_mix_g, w_in, conv_a_w, conv_qkv_w, a_log, dt_bias, dn_norm_g, w_out,
           norm_ffn_g, w_up, conv_ffn_w, w_down, norm_ple_g, w_ple_gate, w_ple_proj, final_norm_g):
    batch, seq, d_model = x.shape
    depth = p.shape[0]
    assert depth == 1 and d_model == D_MODEL
    x2 = x.reshape(batch * seq, d_model)
    p2 = p[0].reshape(batch * seq, PLE_DIM)
    out = _layer(x2, p2, seq, norm_mix_g[0], w_in[0], conv_a_w[0], conv_qkv_w[0], a_log[0],
                 dt_bias[0], dn_norm_g[0], w_out[0], norm_ffn_g[0], w_up[0], conv_ffn_w[0],
                 w_down[0], norm_ple_g[0], w_ple_gate[0], w_ple_proj[0], final_norm_g)
    return out.reshape(batch, seq, d_model)
```

```python
import functools

import jax
import jax.numpy as jnp
from jax import lax
from jax.experimental import pallas as pl
from jax.experimental.pallas import tpu as pltpu

D_MODEL = 2048
CONV_WIDTH = 1024
CONV_K = 3
DN_HEADS = 8
DN_HEAD_DIM = 128
DN_WIDTH = DN_HEADS * DN_HEAD_DIM
DN_CONV_K = 4
D_FF = 5632
FFN_CONV_K = 3
PLE_DIM = 256
EPS = 1e-6
PROJ_COLS = 3 * CONV_WIDTH + 4 * DN_WIDTH

LANES = 128
SUBLANES_F32 = 8
SUBLANES_BF16 = 16
V7X_VMEM_BYTES = 64 * 1024 * 1024
VMEM_LIMIT_BYTES = V7X_VMEM_BYTES - 2 * 1024 * 1024

IN_TM, IN_TN = 1024, 1024
DN_TS = 256
OUT_TM = 512
FFN_TM, FFN_BN = 1024, 512
PLE_TM = 512

DN_CHUNK = 128
INV_BASE = 8

F32 = jnp.float32
BF16 = jnp.bfloat16


def _mm(a, b):
    return jnp.dot(a.astype(BF16), b.astype(BF16), preferred_element_type=F32)


def _mm_nt(a, b):
    return lax.dot_general(a.astype(BF16), b.astype(BF16), (((1,), (1,)), ((), ())),
                           preferred_element_type=F32)


def _mm_exact(a, b):
    return jnp.dot(a, b, precision=lax.Precision.HIGHEST, preferred_element_type=F32)


def _rms_scale(x):
    return lax.rsqrt(jnp.mean(x * x, axis=-1, keepdims=True) + EPS)


def _silu(x):
    return x * jax.nn.sigmoid(x)


def _params(semantics):
    return pltpu.CompilerParams(dimension_semantics=semantics, vmem_limit_bytes=VMEM_LIMIT_BYTES)


def _in_proj_kernel(x_ref, g_ref, w_ref, wg_ref, alog_ref, dtb_ref, proj_ref, gates_ref, hs_ref):
    @pl.when(pl.program_id(1) == 0)
    def _():
        x = x_ref[...]
        hb = (x * _rms_scale(x) * g_ref[...]).astype(BF16)
        hs_ref[...] = hb
        raw = jnp.dot(hb, wg_ref[...], preferred_element_type=F32)
        a = raw + dtb_ref[...]
        softplus = jnp.maximum(a, 0.0) + jnp.log(1.0 + jnp.exp(-jnp.abs(a)))
        decay = -jnp.exp(alog_ref[...]) * softplus
        lane = lax.broadcasted_iota(jnp.int32, raw.shape, 1)
        gates_ref[...] = jnp.where(lane < DN_HEADS, decay, jax.nn.sigmoid(raw))

    tn = proj_ref.shape[1]
    for block in range(PROJ_COLS // tn):
        @pl.when(pl.program_id(1) == block)
        def _(block=block):
            proj_ref[...] = jnp.dot(hs_ref[...], w_ref[:, pl.ds(block * tn, tn)],
                                    preferred_element_type=F32)


def _in_proj(x2, g, w_main, w_gate, alog_pad, dtb_pad, *, tm, tn):
    m = x2.shape[0]
    return pl.pallas_call(
        _in_proj_kernel,
        grid=(m // tm, PROJ_COLS // tn),
        in_specs=[
            pl.BlockSpec((tm, D_MODEL), lambda i, j: (i, 0)),
            pl.BlockSpec((1, D_MODEL), lambda i, j: (0, 0)),
            pl.BlockSpec(w_main.shape, lambda i, j: (0, 0), pipeline_mode=pl.Buffered(1)),
            pl.BlockSpec((D_MODEL, LANES), lambda i, j: (0, 0)),
            pl.BlockSpec((1, LANES), lambda i, j: (0, 0)),
            pl.BlockSpec((1, LANES), lambda i, j: (0, 0)),
        ],
        out_specs=[
            pl.BlockSpec((tm, tn), lambda i, j: (i, j)),
            pl.BlockSpec((tm, LANES), lambda i, j: (i, 0)),
        ],
        out_shape=[
            jax.ShapeDtypeStruct((m, PROJ_COLS), F32),
            jax.ShapeDtypeStruct((m, LANES), F32),
        ],
        scratch_shapes=[pltpu.VMEM((tm, D_MODEL), BF16)],
        compiler_params=_params(("parallel", "arbitrary")),
        name="in_proj",
    )(x2, g, w_main, w_gate, alog_pad, dtb_pad)


def _each(fn, *lists):
    return [fn(*args) for args in zip(*lists)]


def _unit_lower_inverse(lows, row, col):
    c = lows[0].shape[0]
    eye = (row == col).astype(F32)
    base_shift = INV_BASE.bit_length() - 1
    base_mask = (row >> base_shift) == (col >> base_shift)
    l0 = [jnp.where(base_mask, low, 0.0) for low in lows]
    l2 = _each(_mm, l0, l0)
    l4 = _each(_mm, l2, l2)
    l3 = _each(_mm, l0, l2)
    p1 = _each(lambda a, b, d: eye - a + b - d, l0, l2, l3)
    inv = _each(lambda p, pl4: p + pl4, p1, _each(_mm, p1, l4))
    shift = base_shift
    while (1 << shift) < c:
        off_mask = (((row >> (shift + 1)) == (col >> (shift + 1)))
                    & ((row >> shift) != (col >> shift)))
        off_inv = _each(lambda low, t: _mm(jnp.where(off_mask, low, 0.0), t), lows, inv)
        inv = _each(lambda t, x: t - _mm(t, x), inv, off_inv)
        shift += 1
    return inv


def _deltanet_kernel(q_ref, k_ref, v_ref, z_ref, gates_ref, cw_ref, ng_ref, *refs,
                     ts, chunk, n_cast):
    cast_src = refs[:n_cast]
    out_ref = refs[n_cast]
    cast_dst = refs[n_cast + 1:2 * n_cast + 1]
    qbuf, kbuf, vbuf, state_ref, u_s, wq_s, qk_s, kdt_s = refs[2 * n_cast + 1:]

    halo = SUBLANES_F32
    first = pl.program_id(1) == 0

    @pl.when(first)
    def _():
        state_ref[...] = jnp.zeros_like(state_ref)
        for buf in (qbuf, kbuf, vbuf):
            buf[pl.ds(0, halo), :] = jnp.zeros((halo, DN_WIDTH), F32)

    @pl.when(jnp.logical_not(first))
    def _():
        for buf in (qbuf, kbuf, vbuf):
            buf[pl.ds(0, halo), :] = buf[pl.ds(ts, halo), :]

    for src, buf in ((q_ref, qbuf), (k_ref, kbuf), (v_ref, vbuf)):
        buf[pl.ds(halo, ts), :] = src[...]
    for src, dst in zip(cast_src, cast_dst):
        dst[...] = src[...].astype(dst.dtype)

    row = lax.broadcasted_iota(jnp.int32, (chunk, chunk), 0)
    col = lax.broadcasted_iota(jnp.int32, (chunk, chunk), 1)
    lower_incl = (row >= col).astype(F32)
    causal = row >= col
    strict = row > col
    scale = DN_HEAD_DIM ** -0.5
    heads = range(DN_HEADS)
    chunks = range(ts // chunk)
    head_cols = [slice(h * DN_HEAD_DIM, (h + 1) * DN_HEAD_DIM) for h in heads]
    pairs = [(c, h) for c in chunks for h in heads]
    slot = lambda c, h: c * DN_HEADS + h

    def l2_normalized(x, extra_scale):
        return x * (lax.rsqrt(jnp.sum(x * x, axis=-1, keepdims=True) + EPS) * extra_scale)

    def conv_silu(buf, c, which):
        w = [cw_ref[pl.ds(j, 1), pl.ds(which * DN_WIDTH, DN_WIDTH)] for j in range(DN_CONV_K)]
        cur = buf[pl.ds(c * chunk, chunk + halo), :]
        prev = pltpu.roll(cur, 1, axis=0)
        tail = pltpu.roll(w[1] * cur + w[0] * prev, 2, axis=0)
        return _silu((w[3] * cur + w[2] * prev + tail)[halo:, :])

    qa = [conv_silu(qbuf, c, 0) for c in chunks]
    ka = [conv_silu(kbuf, c, 1) for c in chunks]
    va = [conv_silu(vbuf, c, 2) for c in chunks]
    gts = [gates_ref[pl.ds(c * chunk, chunk), :] for c in chunks]
    gcum = [_mm_exact(lower_incl, g) for g in gts]
    gcum_t = [g.T for g in gcum]
    q = [l2_normalized(qa[c][:, head_cols[h]], scale) for c, h in pairs]
    k = [l2_normalized(ka[c][:, head_cols[h]], 1.0) for c, h in pairs]
    v = [va[c][:, head_cols[h]] for c, h in pairs]
    gc = [gcum[c][:, h:h + 1] for c, h in pairs]
    gr = [gcum_t[c][h:h + 1, :] for c, h in pairs]
    beta = [gts[c][:, DN_HEADS + h:DN_HEADS + h + 1] for c, h in pairs]
    g_last = [g[chunk - 1:chunk, :] for g in gc]
    decay = _each(lambda a, b: jnp.exp(jnp.where(causal, a - b, -1e30)), gc, gr)
    kk = _each(_mm_nt, k, k)
    qk = _each(lambda a, b, d: _mm_nt(a, b) * d, q, k, decay)
    low = _each(lambda m, d, b: jnp.where(strict, m * d * b, 0.0), kk, decay, beta)
    inv = _unit_lower_inverse(low, row, col)
    eg = _each(jnp.exp, gc)
    rhs = _each(lambda vv, kx, b, e: jnp.concatenate([vv * b, kx * (b * e)], axis=1),
                v, k, beta, eg)
    sol = _each(_mm, inv, rhs)
    for (c, h), s, qx, e, a, kx, gl, g in zip(pairs, sol, q, eg, qk, k, g_last, gc):
        i = slot(c, h)
        u_s[i] = s[:, :DN_HEAD_DIM]
        wq_s[i] = jnp.concatenate([s[:, DN_HEAD_DIM:], qx * e], axis=0).astype(BF16)
        qk_s[i] = a.astype(BF16)
        kdt_s[i] = (kx * jnp.exp(gl - g)).T.astype(BF16)
    carry_decay = _each(jnp.exp, g_last)

    state = [state_ref[h] for h in heads]
    for c in chunks:
        ids = [slot(c, h) for h in heads]
        ws_qs = [jnp.dot(wq_s[i], st.astype(BF16), preferred_element_type=F32)
                 for i, st in zip(ids, state)]
        v_new = [(u_s[i] - x[:chunk]).astype(BF16) for i, x in zip(ids, ws_qs)]
        state = [st * carry_decay[i] + jnp.dot(kdt_s[i], vn, preferred_element_type=F32)
                 for i, st, vn in zip(ids, state, v_new)]
        o = [x[chunk:] + jnp.dot(qk_s[i], vn, preferred_element_type=F32)
             for i, x, vn in zip(ids, ws_qs, v_new)]
        rows = pl.ds(c * chunk, chunk)
        for h, sl in enumerate(head_cols):
            y = o[h] * _rms_scale(o[h]) * ng_ref[...] * _silu(z_ref[rows, sl])
            out_ref[rows, sl] = y.astype(out_ref.dtype)
    for h in heads:
        state_ref[h] = state[h]


def _deltanet(proj, gates, conv_w, norm_g, cast_weights, *, seq, ts, chunk):
    m = proj.shape[0]
    tiles = seq // ts
    steps = (m // seq) * tiles
    pairs = (ts // chunk) * DN_HEADS
    col0 = 3 * CONV_WIDTH // DN_WIDTH
    row_map = lambda b, t: b * tiles + t
    qkvz = [pl.BlockSpec((ts, DN_WIDTH), functools.partial(lambda b, t, n: (b * tiles + t, col0 + n), n=n))
            for n in range(4)]
    cast_specs = []
    for w in cast_weights:
        share = 1 if w.shape[0] % (steps * SUBLANES_BF16) == 0 else 2
        assert w.shape[0] % (steps // share * SUBLANES_BF16) == 0, w.shape
        cast_specs.append(pl.BlockSpec(
            (w.shape[0] // (steps // share), w.shape[1]),
            functools.partial(lambda b, t, share: (row_map(b, t) // share, 0), share=share)))
    outs = pl.pallas_call(
        functools.partial(_deltanet_kernel, ts=ts, chunk=chunk, n_cast=len(cast_weights)),
        grid=(m // seq, tiles),
        in_specs=qkvz + [
            pl.BlockSpec((ts, LANES), lambda b, t: (row_map(b, t), 0)),
            pl.BlockSpec((DN_CONV_K, 3 * DN_WIDTH), lambda b, t: (0, 0)),
            pl.BlockSpec((1, DN_HEAD_DIM), lambda b, t: (0, 0)),
        ] + cast_specs,
        out_specs=[pl.BlockSpec((ts, DN_WIDTH), lambda b, t: (row_map(b, t), 0))] + cast_specs,
        out_shape=[jax.ShapeDtypeStruct((m, DN_WIDTH), BF16)]
        + [jax.ShapeDtypeStruct(w.shape, BF16) for w in cast_weights],
        scratch_shapes=[pltpu.VMEM((ts + SUBLANES_F32, DN_WIDTH), F32)] * 3 + [
            pltpu.VMEM((DN_HEADS, DN_HEAD_DIM, DN_HEAD_DIM), F32),
            pltpu.VMEM((pairs, chunk, DN_HEAD_DIM), F32),
            pltpu.VMEM((pairs, 2 * chunk, DN_HEAD_DIM), BF16),
            pltpu.VMEM((pairs, chunk, chunk), BF16),
            pltpu.VMEM((pairs, DN_HEAD_DIM, chunk), BF16),
        ],
        compiler_params=_params(("parallel", "arbitrary")),
        name="deltanet",
    )(proj, proj, proj, proj, gates, conv_w, norm_g, *cast_weights)
    return outs[0], outs[1:]


def _out_proj_kernel(ax_ref, ab_ref, ac_ref, hax_ref, hac_ref, yb_ref, x_ref, cw_ref, w_ref,
                     out_ref, pbuf, *, tm, seq):
    halo = SUBLANES_F32
    acc_b = jnp.dot(yb_ref[...], w_ref[pl.ds(CONV_WIDTH, DN_WIDTH), :], preferred_element_type=F32)
    seq_start = (pl.program_id(0) * tm) % seq == 0
    pbuf[pl.ds(0, halo), :] = jnp.where(seq_start, 0.0, hac_ref[...] * hax_ref[...])
    pbuf[pl.ds(halo, tm), :] = ac_ref[...] * ax_ref[...]
    acc = None
    for j in range(CONV_K):
        term = pbuf[pl.ds(halo - (CONV_K - 1) + j, tm), :] * cw_ref[pl.ds(j, 1), :]
        acc = term if acc is None else acc + term
    ya = (ab_ref[...] * acc).astype(BF16)
    out_ref[...] = (x_ref[...] + acc_b
                    + jnp.dot(ya, w_ref[pl.ds(0, CONV_WIDTH), :], preferred_element_type=F32))


def _out_proj(proj, yb, x2, conv_w, w_out, *, tm, seq):
    m = x2.shape[0]
    halo = SUBLANES_F32
    halo_row = lambda i: jnp.maximum(i * (tm // halo) - 1, 0)
    return pl.pallas_call(
        functools.partial(_out_proj_kernel, tm=tm, seq=seq),
        grid=(m // tm,),
        in_specs=[
            pl.BlockSpec((tm, CONV_WIDTH), lambda i: (i, 0)),
            pl.BlockSpec((tm, CONV_WIDTH), lambda i: (i, 1)),
            pl.BlockSpec((tm, CONV_WIDTH), lambda i: (i, 2)),
            pl.BlockSpec((halo, CONV_WIDTH), lambda i: (halo_row(i), 0)),
            pl.BlockSpec((halo, CONV_WIDTH), lambda i: (halo_row(i), 2)),
            pl.BlockSpec((tm, DN_WIDTH), lambda i: (i, 0)),
            pl.BlockSpec((tm, D_MODEL), lambda i: (i, 0)),
            pl.BlockSpec((CONV_K, CONV_WIDTH), lambda i: (0, 0)),
            pl.BlockSpec((D_MODEL, D_MODEL), lambda i: (0, 0)),
        ],
        out_specs=pl.BlockSpec((tm, D_MODEL), lambda i: (i, 0)),
        out_shape=jax.ShapeDtypeStruct((m, D_MODEL), F32),
        scratch_shapes=[pltpu.VMEM((tm + halo, CONV_WIDTH), F32)],
        compiler_params=_params(("parallel",)),
        name="out_proj",
    )(proj, proj, proj, proj, proj, yb, x2, conv_w, w_out)


def _ffn_kernel(x_ref, g_ref, wg_ref, wv_ref, cg_ref, cv_ref, wd_ref, out_ref,
                hs_ref, ubuf, carry, *, tm, seq):
    halo = SUBLANES_F32
    i = pl.program_id(0)
    j = pl.program_id(1)
    seq_start = (i * tm) % seq == 0

    @pl.when(jnp.logical_and(i == 0, j == 0))
    def _():
        carry[...] = jnp.zeros_like(carry)

    @pl.when(j == 0)
    def _():
        x = x_ref[...]
        hs_ref[...] = (x * _rms_scale(x) * g_ref[...]).astype(BF16)
        out_ref[...] = x

    def up_conv(w_ref, cw_ref, which, cs):
        up = jnp.dot(hs_ref[...], w_ref[:, cs], preferred_element_type=F32)
        ubuf[which, pl.ds(0, halo), cs] = jnp.where(seq_start, 0.0, carry[which, j, :, cs])
        ubuf[which, pl.ds(halo, tm), cs] = up
        carry[which, j, :, cs] = up[tm - halo:, :]
        acc = None
        for t in range(FFN_CONV_K):
            term = (ubuf[which, pl.ds(halo - (FFN_CONV_K - 1) + t, tm), cs]
                    * cw_ref[pl.ds(t, 1), cs])
            acc = term if acc is None else acc + term
        return acc

    bn = wd_ref.shape[0]
    halves = [pl.ds(h * (bn // 2), bn // 2) for h in range(2)]
    gates = [_silu(up_conv(wg_ref, cg_ref, 0, cs)) for cs in halves]
    acts = [(g * up_conv(wv_ref, cv_ref, 1, cs)).astype(BF16) for g, cs in zip(gates, halves)]
    for cs, act in zip(halves, acts):
        out_ref[...] += jnp.dot(act, wd_ref[cs, :], preferred_element_type=F32)


def _ffn(x1, g, w_up, conv_w, w_down, *, tm, bn, seq):
    m = x1.shape[0]
    halo = SUBLANES_F32
    nj = D_FF // bn
    return pl.pallas_call(
        functools.partial(_ffn_kernel, tm=tm, seq=seq),
        grid=(m // tm, nj),
        in_specs=[
            pl.BlockSpec((tm, D_MODEL), lambda i, j: (i, 0)),
            pl.BlockSpec((1, D_MODEL), lambda i, j: (0, 0)),
            pl.BlockSpec((D_MODEL, bn), lambda i, j: (0, j)),
            pl.BlockSpec((D_MODEL, bn), lambda i, j: (0, j + nj)),
            pl.BlockSpec((FFN_CONV_K, bn), lambda i, j: (0, j)),
            pl.BlockSpec((FFN_CONV_K, bn), lambda i, j: (0, j + nj)),
            pl.BlockSpec((bn, D_MODEL), lambda i, j: (j, 0)),
        ],
        out_specs=pl.BlockSpec((tm, D_MODEL), lambda i, j: (i, 0)),
        out_shape=jax.ShapeDtypeStruct((m, D_MODEL), F32),
        scratch_shapes=[pltpu.VMEM((tm, D_MODEL), BF16),
                        pltpu.VMEM((2, tm + halo, bn), F32),
                        pltpu.VMEM((2, nj, halo, bn), F32)],
        compiler_params=_params(("arbitrary", "arbitrary")),
        name="conv_ffn",
    )(x1, g, w_up, w_up, conv_w, conv_w, w_down)


def _ple_kernel(x_ref, p_ref, g_ref, wpg_ref, wpp_ref, fg_ref, out_ref):
    x = x_ref[...]
    hb = (x * _rms_scale(x) * g_ref[...]).astype(BF16)
    gate = jax.nn.sigmoid(jnp.dot(hb, wpg_ref[...], preferred_element_type=F32))
    emb = jnp.dot(p_ref[...].astype(BF16), wpp_ref[...], preferred_element_type=F32)
    y = x + gate * emb
    out_ref[...] = y * _rms_scale(y) * fg_ref[...]


def _ple(x2, p2, g, w_pg, w_pp, final_g, *, tm):
    m = x2.shape[0]
    return pl.pallas_call(
        _ple_kernel,
        grid=(m // tm,),
        in_specs=[
            pl.BlockSpec((tm, D_MODEL), lambda i: (i, 0)),
            pl.BlockSpec((tm, PLE_DIM), lambda i: (i, 0)),
            pl.BlockSpec((1, D_MODEL), lambda i: (0, 0)),
            pl.BlockSpec((D_MODEL, D_MODEL), lambda i: (0, 0)),
            pl.BlockSpec((PLE_DIM, D_MODEL), lambda i: (0, 0)),
            pl.BlockSpec((1, D_MODEL), lambda i: (0, 0)),
        ],
        out_specs=pl.BlockSpec((tm, D_MODEL), lambda i: (i, 0)),
        out_shape=jax.ShapeDtypeStruct((m, D_MODEL), F32),
        compiler_params=_params(("parallel",)),
        name="ple",
    )(x2, p2, g, w_pg, w_pp, final_g)


def _layer(x2, p2, seq, norm_mix_g, w_in, conv_a_w, conv_qkv_w, a_log, dt_bias, dn_norm_g,
           w_out, norm_ffn_g, w_up, conv_ffn_w, w_down, norm_ple_g, w_ple_gate, w_ple_proj,
           out_norm_g):
    row = lambda v: v.reshape(1, -1).astype(F32)
    lane_pad = lambda v: jnp.pad(row(v), ((0, 0), (0, LANES - v.shape[-1])))
    w_main = w_in.astype(BF16)
    w_gate = jnp.pad(w_in[:, PROJ_COLS:], ((0, 0), (0, LANES - 2 * DN_HEADS))).astype(BF16)

    proj, gates = _in_proj(x2, row(norm_mix_g), w_main, w_gate, lane_pad(a_log), lane_pad(dt_bias),
                           tm=IN_TM, tn=IN_TN)
    yb, (w_out_b, w_up_b, w_down_b, w_pg_b) = _deltanet(
        proj, gates, conv_qkv_w.astype(F32), row(dn_norm_g), (w_out, w_up, w_down, w_ple_gate),
        seq=seq, ts=DN_TS, chunk=DN_CHUNK)
    x2 = _out_proj(proj, yb, x2, conv_a_w.astype(F32), w_out_b, tm=OUT_TM, seq=seq)
    x2 = _ffn(x2, row(norm_ffn_g), w_up_b, conv_ffn_w.astype(F32), w_down_b,
              tm=FFN_TM, bn=FFN_BN, seq=seq)
    return _ple(x2, p2, row(norm_ple_g), w_pg_b, w_ple_proj.astype(BF16),
                row(out_norm_g), tm=PLE_TM)


def kernel(x, p, norm_mix_g, w_in, conv_a_w, conv_qkv_w, a_log, dt_bias, dn_norm_g, w_out,
           norm_ffn_g, w_up, conv_ffn_w, w_down, norm_ple_g, w_ple_gate, w_ple_proj, final_norm_g):
    batch, seq, d_model = x.shape
    depth = p.shape[0]
    assert depth == 1 and d_model == D_MODEL
    x2 = x.reshape(batch * seq, d_model)
    p2 = p[0].reshape(batch * seq, PLE_DIM)
    out = _layer(x2, p2, seq, norm_mix_g[0], w_in[0], conv_a_w[0], conv_qkv_w[0], a_log[0],
                 dt_bias[0], dn_norm_g[0], w_out[0], norm_ffn_g[0], w_up[0], conv_ffn_w[0],
                 w_down[0], norm_ple_g[0], w_ple_gate[0], w_ple_proj[0], final_norm_g)
    return out.reshape(batch, seq, d_model)
```

```python
import functools

import jax
import jax.numpy as jnp
from jax import lax
from jax.experimental import pallas as pl
from jax.experimental.pallas import tpu as pltpu

D_MODEL = 2048
CONV_WIDTH = 1024
CONV_K = 3
DN_HEADS = 8
DN_HEAD_DIM = 128
DN_WIDTH = DN_HEADS * DN_HEAD_DIM
DN_CONV_K = 4
D_FF = 5632
FFN_CONV_K = 3
PLE_DIM = 256
EPS = 1e-6
PROJ_COLS = 3 * CONV_WIDTH + 4 * DN_WIDTH

LANES = 128
SUBLANES_F32 = 8
SUBLANES_BF16 = 16
V7X_VMEM_BYTES = 64 * 1024 * 1024
VMEM_LIMIT_BYTES = V7X_VMEM_BYTES - 2 * 1024 * 1024

IN_TM, IN_TN = 1024, 1024
DN_TS = 256
OUT_TM = 512
FFN_TM, FFN_BN = 1024, 512
PLE_TM = 512

DN_CHUNK = 128
INV_BASE = 8

F32 = jnp.float32
BF16 = jnp.bfloat16


def _mm(a, b):
    return jnp.dot(a.astype(BF16), b.astype(BF16), preferred_element_type=F32)


def _mm_nt(a, b):
    return lax.dot_general(a.astype(BF16), b.astype(BF16), (((1,), (1,)), ((), ())),
                           preferred_element_type=F32)


def _mm_exact(a, b):
    return jnp.dot(a, b, precision=lax.Precision.HIGHEST, preferred_element_type=F32)


def _rms_scale(x):
    return lax.rsqrt(jnp.mean(x * x, axis=-1, keepdims=True) + EPS)


def _silu(x):
    return x * jax.nn.sigmoid(x)


def _params(semantics):
    return pltpu.CompilerParams(dimension_semantics=semantics, vmem_limit_bytes=VMEM_LIMIT_BYTES)


def _in_proj_kernel(x_ref, g_ref, w_ref, wg_ref, alog_ref, dtb_ref, proj_ref, gates_ref, hs_ref):
    @pl.when(pl.program_id(1) == 0)
    def _():
        x = x_ref[...]
        hb = (x * _rms_scale(x) * g_ref[...]).astype(BF16)
        hs_ref[...] = hb
        raw = jnp.dot(hb, wg_ref[...], preferred_element_type=F32)
        a = raw + dtb_ref[...]
        softplus = jnp.maximum(a, 0.0) + jnp.log(1.0 + jnp.exp(-jnp.abs(a)))
        decay = -jnp.exp(alog_ref[...]) * softplus
        lane = lax.broadcasted_iota(jnp.int32, raw.shape, 1)
        gates_ref[...] = jnp.where(lane < DN_HEADS, decay, jax.nn.sigmoid(raw))

    tn = proj_ref.shape[1]
    for block in range(PROJ_COLS // tn):
        @pl.when(pl.program_id(1) == block)
        def _(block=block):
            proj_ref[...] = jnp.dot(hs_ref[...], w_ref[:, pl.ds(block * tn, tn)],
                                    preferred_element_type=F32)


def _in_proj(x2, g, w_main, w_gate, alog_pad, dtb_pad, *, tm, tn):
    m = x2.shape[0]
    return pl.pallas_call(
        _in_proj_kernel,
        grid=(m // tm, PROJ_COLS // tn),
        in_specs=[
            pl.BlockSpec((tm, D_MODEL), lambda i, j: (i, 0)),
            pl.BlockSpec((1, D_MODEL), lambda i, j: (0, 0)),
            pl.BlockSpec(w_main.shape, lambda i, j: (0, 0), pipeline_mode=pl.Buffered(1)),
            pl.BlockSpec((D_MODEL, LANES), lambda i, j: (0, 0)),
            pl.BlockSpec((1, LANES), lambda i, j: (0, 0)),
            pl.BlockSpec((1, LANES), lambda i, j: (0, 0)),
        ],
        out_specs=[
            pl.BlockSpec((tm, tn), lambda i, j: (i, j)),
            pl.BlockSpec((tm, LANES), lambda i, j: (i, 0)),
        ],
        out_shape=[
            jax.ShapeDtypeStruct((m, PROJ_COLS), F32),
            jax.ShapeDtypeStruct((m, LANES), F32),
        ],
        scratch_shapes=[pltpu.VMEM((tm, D_MODEL), BF16)],
        compiler_params=_params(("parallel", "arbitrary")),
        name="in_proj",
    )(x2, g, w_main, w_gate, alog_pad, dtb_pad)


def _each(fn, *lists):
    return [fn(*args) for args in zip(*lists)]


def _unit_lower_inverse(lows, row, col):
    c = lows[0].shape[0]
    eye = (row == col).astype(F32)
    base_shift = INV_BASE.bit_length() - 1
    base_mask = (row >> base_shift) == (col >> base_shift)
    l0 = [jnp.where(base_mask, low, 0.0) for low in lows]
    l2 = _each(_mm, l0, l0)
    l4 = _each(_mm, l2, l2)
    l3 = _each(_mm, l0, l2)
    p1 = _each(lambda a, b, d: eye - a + b - d, l0, l2, l3)
    inv = _each(lambda p, pl4: p + pl4, p1, _each(_mm, p1, l4))
    shift = base_shift
    while (1 << shift) < c:
        off_mask = (((row >> (shift + 1)) == (col >> (shift + 1)))
                    & ((row >> shift) != (col >> shift)))
        off_inv = _each(lambda low, t: _mm(jnp.where(off_mask, low, 0.0), t), lows, inv)
        inv = _each(lambda t, x: t - _mm(t, x), inv, off_inv)
        shift += 1
    return inv


def _deltanet_kernel(q_ref, k_ref, v_ref, z_ref, gates_ref, cw_ref, ng_ref, *refs,
                     ts, chunk, n_cast):
    cast_src = refs[:n_cast]
    out_ref = refs[n_cast]
    cast_dst = refs[n_cast + 1:2 * n_cast + 1]
    qbuf, kbuf, vbuf, state_ref, u_s, wq_s, qk_s, kdt_s = refs[2 * n_cast + 1:]

    halo = SUBLANES_F32
    first = pl.program_id(1) == 0

    @pl.when(first)
    def _():
        state_ref[...] = jnp.zeros_like(state_ref)
        for buf in (qbuf, kbuf, vbuf):
            buf[pl.ds(0, halo), :] = jnp.zeros((halo, DN_WIDTH), F32)

    @pl.when(jnp.logical_not(first))
    def _():
        for buf in (qbuf, kbuf, vbuf):
            buf[pl.ds(0, halo), :] = buf[pl.ds(ts, halo), :]

    for src, buf in ((q_ref, qbuf), (k_ref, kbuf), (v_ref, vbuf)):
        buf[pl.ds(halo, ts), :] = src[...]
    for src, dst in zip(cast_src, cast_dst):
        dst[...] = src[...].astype(dst.dtype)

    row = lax.broadcasted_iota(jnp.int32, (chunk, chunk), 0)
    col = lax.broadcasted_iota(jnp.int32, (chunk, chunk), 1)
    lower_incl = (row >= col).astype(F32)
    causal = row >= col
    strict = row > col
    scale = DN_HEAD_DIM ** -0.5
    heads = range(DN_HEADS)
    chunks = range(ts // chunk)
    head_cols = [slice(h * DN_HEAD_DIM, (h + 1) * DN_HEAD_DIM) for h in heads]
    pairs = [(c, h) for c in chunks for h in heads]
    slot = lambda c, h: c * DN_HEADS + h

    def l2_normalized(x, extra_scale):
        return x * (lax.rsqrt(jnp.sum(x * x, axis=-1, keepdims=True) + EPS) * extra_scale)

    def conv_silu(buf, c, which):
        w = [cw_ref[pl.ds(j, 1), pl.ds(which * DN_WIDTH, DN_WIDTH)] for j in range(DN_CONV_K)]
        cur = buf[pl.ds(c * chunk, chunk + halo), :]
        prev = pltpu.roll(cur, 1, axis=0)
        tail = pltpu.roll(w[1] * cur + w[0] * prev, 2, axis=0)
        return _silu((w[3] * cur + w[2] * prev + tail)[halo:, :])

    qa = [conv_silu(qbuf, c, 0) for c in chunks]
    ka = [conv_silu(kbuf, c, 1) for c in chunks]
    va = [conv_silu(vbuf, c, 2) for c in chunks]
    gts = [gates_ref[pl.ds(c * chunk, chunk), :] for c in chunks]
    gcum = [_mm_exact(lower_incl, g) for g in gts]
    gcum_t = [g.T for g in gcum]
    q = [l2_normalized(qa[c][:, head_cols[h]], scale) for c, h in pairs]
    k = [l2_normalized(ka[c][:, head_cols[h]], 1.0) for c, h in pairs]
    v = [va[c][:, head_cols[h]] for c, h in pairs]
    gc = [gcum[c][:, h:h + 1] for c, h in pairs]
    gr = [gcum_t[c][h:h + 1, :] for c, h in pairs]
    beta = [gts[c][:, DN_HEADS + h:DN_HEADS + h + 1] for c, h in pairs]
    g_last = [g[chunk - 1:chunk, :] for g in gc]
    decay = _each(lambda a, b: jnp.exp(jnp.where(causal, a - b, -1e30)), gc, gr)
    kk = _each(_mm_nt, k, k)
    qk = _each(lambda a, b, d: _mm_nt(a, b) * d, q, k, decay)
    low = _each(lambda m, d, b: jnp.where(strict, m * d * b, 0.0), kk, decay, beta)
    inv = _unit_lower_inverse(low, row, col)
    eg = _each(jnp.exp, gc)
    rhs = _each(lambda vv, kx, b, e: jnp.concatenate([vv * b, kx * (b * e)], axis=1),
                v, k, beta, eg)
    sol = _each(_mm, inv, rhs)
    for (c, h), s, qx, e, a, kx, gl, g in zip(pairs, sol, q, eg, qk, k, g_last, gc):
        i = slot(c, h)
        u_s[i] = s[:, :DN_HEAD_DIM]
        wq_s[i] = jnp.concatenate([s[:, DN_HEAD_DIM:], qx * e], axis=0).astype(BF16)
        qk_s[i] = a.astype(BF16)
        kdt_s[i] = (kx * jnp.exp(gl - g)).T.astype(BF16)
    carry_decay = _each(jnp.exp, g_last)

    state = [state_ref[h] for h in heads]
    for c in chunks:
        ids = [slot(c, h) for h in heads]
        ws_qs = [jnp.dot(wq_s[i], st.astype(BF16), preferred_element_type=F32)
                 for i, st in zip(ids, state)]
        v_new = [(u_s[i] - x[:chunk]).astype(BF16) for i, x in zip(ids, ws_qs)]
        state = [st * carry_decay[i] + jnp.dot(kdt_s[i], vn, preferred_element_type=F32)
                 for i, st, vn in zip(ids, state, v_new)]
        o = [x[chunk:] + jnp.dot(qk_s[i], vn, preferred_element_type=F32)
             for i, x, vn in zip(ids, ws_qs, v_new)]
        rows = pl.ds(c * chunk, chunk)
        for h, sl in enumerate(head_cols):
            y = o[h] * _rms_scale(o[h]) * ng_ref[...] * _silu(z_ref[rows, sl])
            out_ref[rows, sl] = y.astype(out_ref.dtype)
    for h in heads:
        state_ref[h] = state[h]


def _deltanet(proj, gates, conv_w, norm_g, cast_weights, *, seq, ts, chunk):
    m = proj.shape[0]
    tiles = seq // ts
    steps = (m // seq) * tiles
    pairs = (ts // chunk) * DN_HEADS
    col0 = 3 * CONV_WIDTH // DN_WIDTH
    row_map = lambda b, t: b * tiles + t
    qkvz = [pl.BlockSpec((ts, DN_WIDTH), functools.partial(lambda b, t, n: (b * tiles + t, col0 + n), n=n))
            for n in range(4)]
    cast_specs = []
    for w in cast_weights:
        share = 1 if w.shape[0] % (steps * SUBLANES_BF16) == 0 else 2
        assert w.shape[0] % (steps // share * SUBLANES_BF16) == 0, w.shape
        cast_specs.append(pl.BlockSpec(
            (w.shape[0] // (steps // share), w.shape[1]),
            functools.partial(lambda b, t, share: (row_map(b, t) // share, 0), share=share)))
    outs = pl.pallas_call(
        functools.partial(_deltanet_kernel, ts=ts, chunk=chunk, n_cast=len(cast_weights)),
        grid=(m // seq, tiles),
        in_specs=qkvz + [
            pl.BlockSpec((ts, LANES), lambda b, t: (row_map(b, t), 0)),
            pl.BlockSpec((DN_CONV_K, 3 * DN_WIDTH), lambda b, t: (0, 0)),
            pl.BlockSpec((1, DN_HEAD_DIM), lambda b, t: (0, 0)),
        ] + cast_specs,
        out_specs=[pl.BlockSpec((ts, DN_WIDTH), lambda b, t: (row_map(b, t), 0))] + cast_specs,
        out_shape=[jax.ShapeDtypeStruct((m, DN_WIDTH), BF16)]
        + [jax.ShapeDtypeStruct(w.shape, BF16) for w in cast_weights],
        scratch_shapes=[pltpu.VMEM((ts + SUBLANES_F32, DN_WIDTH), F32)] * 3 + [
            pltpu.VMEM((DN_HEADS, DN_HEAD_DIM, DN_HEAD_DIM), F32),
            pltpu.VMEM((pairs, chunk, DN_HEAD_DIM), F32),
            pltpu.VMEM((pairs, 2 * chunk, DN_HEAD_DIM), BF16),
            pltpu.VMEM((pairs, chunk, chunk), BF16),
            pltpu.VMEM((pairs, DN_HEAD_DIM, chunk), BF16),
        ],
        compiler_params=_params(("parallel", "arbitrary")),
        name="deltanet",
    )(proj, proj, proj, proj, gates, conv_w, norm_g, *cast_weights)
    return outs[0], outs[1:]


RING = 3


def _out_proj_kernel(proj_hbm, yb_hbm, x_hbm, hax_ref, hac_ref, cw_ref, w_ref, out_ref,
                     a_buf, yb_buf, x_buf, sems, pbuf, *, tm, seq, steps):
    halo = SUBLANES_F32
    step = pl.program_id(0)

    def tile_copies(t, slot):
        rows = pl.ds(t * tm, tm)
        copies = [pltpu.make_async_copy(proj_hbm.at[rows, pl.ds(n * CONV_WIDTH, CONV_WIDTH)],
                                        a_buf.at[slot, n], sems.at[n, slot]) for n in range(3)]
        copies.append(pltpu.make_async_copy(yb_hbm.at[rows, :], yb_buf.at[slot], sems.at[3, slot]))
        copies.append(pltpu.make_async_copy(x_hbm.at[rows, :], x_buf.at[slot], sems.at[4, slot]))
        return copies

    @pl.when(step == 0)
    def _():
        for t in range(RING - 1):
            for copy in tile_copies(t, t):
                copy.start()

    ahead = step + (RING - 1)

    @pl.when(ahead < steps)
    def _():
        for copy in tile_copies(ahead, ahead % RING):
            copy.start()

    slot = step % RING
    for copy in tile_copies(step, slot):
        copy.wait()

    acc_b = jnp.dot(yb_buf[slot], w_ref[pl.ds(CONV_WIDTH, DN_WIDTH), :], preferred_element_type=F32)
    seq_start = (step * tm) % seq == 0
    pbuf[pl.ds(0, halo), :] = jnp.where(seq_start, 0.0, hac_ref[...] * hax_ref[...])
    pbuf[pl.ds(halo, tm), :] = a_buf[slot, 2] * a_buf[slot, 0]
    acc = None
    for j in range(CONV_K):
        term = pbuf[pl.ds(halo - (CONV_K - 1) + j, tm), :] * cw_ref[pl.ds(j, 1), :]
        acc = term if acc is None else acc + term
    ya = (a_buf[slot, 1] * acc).astype(BF16)
    out_ref[...] = (x_buf[slot] + acc_b
                    + jnp.dot(ya, w_ref[pl.ds(0, CONV_WIDTH), :], preferred_element_type=F32))


def _out_proj(proj, yb, x2, conv_w, w_out, *, tm, seq):
    m = x2.shape[0]
    steps = m // tm
    assert steps >= RING
    halo = SUBLANES_F32
    halo_row = lambda i: jnp.maximum(i * (tm // halo) - 1, 0)
    return pl.pallas_call(
        functools.partial(_out_proj_kernel, tm=tm, seq=seq, steps=steps),
        grid=(steps,),
        in_specs=[
            pl.BlockSpec(memory_space=pl.ANY),
            pl.BlockSpec(memory_space=pl.ANY),
            pl.BlockSpec(memory_space=pl.ANY),
            pl.BlockSpec((halo, CONV_WIDTH), lambda i: (halo_row(i), 0)),
            pl.BlockSpec((halo, CONV_WIDTH), lambda i: (halo_row(i), 2)),
            pl.BlockSpec((CONV_K, CONV_WIDTH), lambda i: (0, 0)),
            pl.BlockSpec((D_MODEL, D_MODEL), lambda i: (0, 0), pipeline_mode=pl.Buffered(1)),
        ],
        out_specs=pl.BlockSpec((tm, D_MODEL), lambda i: (i, 0)),
        out_shape=jax.ShapeDtypeStruct((m, D_MODEL), F32),
        scratch_shapes=[
            pltpu.VMEM((RING, 3, tm, CONV_WIDTH), F32),
            pltpu.VMEM((RING, tm, DN_WIDTH), BF16),
            pltpu.VMEM((RING, tm, D_MODEL), F32),
            pltpu.SemaphoreType.DMA((5, RING)),
            pltpu.VMEM((tm + halo, CONV_WIDTH), F32),
        ],
        compiler_params=_params(("arbitrary",)),
        name="out_proj",
    )(proj, yb, x2, proj, proj, conv_w, w_out)


def _ffn_kernel(x_ref, g_ref, wg_ref, wv_ref, cg_ref, cv_ref, wd_ref, out_ref,
                hs_ref, ubuf, carry, *, tm, seq):
    halo = SUBLANES_F32
    i = pl.program_id(0)
    j = pl.program_id(1)
    seq_start = (i * tm) % seq == 0

    @pl.when(jnp.logical_and(i == 0, j == 0))
    def _():
        carry[...] = jnp.zeros_like(carry)

    @pl.when(j == 0)
    def _():
        x = x_ref[...]
        hs_ref[...] = (x * _rms_scale(x) * g_ref[...]).astype(BF16)
        out_ref[...] = x

    def up_conv(w_ref, cw_ref, which, cs):
        up = jnp.dot(hs_ref[...], w_ref[:, cs], preferred_element_type=F32)
        ubuf[which, pl.ds(0, halo), cs] = jnp.where(seq_start, 0.0, carry[which, j, :, cs])
        ubuf[which, pl.ds(halo, tm), cs] = up
        carry[which, j, :, cs] = up[tm - halo:, :]
        acc = None
        for t in range(FFN_CONV_K):
            term = (ubuf[which, pl.ds(halo - (FFN_CONV_K - 1) + t, tm), cs]
                    * cw_ref[pl.ds(t, 1), cs])
            acc = term if acc is None else acc + term
        return acc

    bn = wd_ref.shape[0]
    halves = [pl.ds(h * (bn // 2), bn // 2) for h in range(2)]
    gates = [_silu(up_conv(wg_ref, cg_ref, 0, cs)) for cs in halves]
    acts = [(g * up_conv(wv_ref, cv_ref, 1, cs)).astype(BF16) for g, cs in zip(gates, halves)]
    for cs, act in zip(halves, acts):
        out_ref[...] += jnp.dot(act, wd_ref[cs, :], preferred_element_type=F32)


def _ffn(x1, g, w_up, conv_w, w_down, *, tm, bn, seq):
    m = x1.shape[0]
    halo = SUBLANES_F32
    nj = D_FF // bn
    return pl.pallas_call(
        functools.partial(_ffn_kernel, tm=tm, seq=seq),
        grid=(m // tm, nj),
        in_specs=[
            pl.BlockSpec((tm, D_MODEL), lambda i, j: (i, 0)),
            pl.BlockSpec((1, D_MODEL), lambda i, j: (0, 0)),
            pl.BlockSpec((D_MODEL, bn), lambda i, j: (0, j)),
            pl.BlockSpec((D_MODEL, bn), lambda i, j: (0, j + nj)),
            pl.BlockSpec((FFN_CONV_K, bn), lambda i, j: (0, j)),
            pl.BlockSpec((FFN_CONV_K, bn), lambda i, j: (0, j + nj)),
            pl.BlockSpec((bn, D_MODEL), lambda i, j: (j, 0)),
        ],
        out_specs=pl.BlockSpec((tm, D_MODEL), lambda i, j: (i, 0)),
        out_shape=jax.ShapeDtypeStruct((m, D_MODEL), F32),
        scratch_shapes=[pltpu.VMEM((tm, D_MODEL), BF16),
                        pltpu.VMEM((2, tm + halo, bn), F32),
                        pltpu.VMEM((2, nj, halo, bn), F32)],
        compiler_params=_params(("arbitrary", "arbitrary")),
        name="conv_ffn",
    )(x1, g, w_up, w_up, conv_w, conv_w, w_down)


def _ple_kernel(x_ref, p_ref, g_ref, wpg_ref, wpp_ref, fg_ref, out_ref):
    x = x_ref[...]
    hb = (x * _rms_scale(x) * g_ref[...]).astype(BF16)
    gate = jax.nn.sigmoid(jnp.dot(hb, wpg_ref[...], preferred_element_type=F32))
    emb = jnp.dot(p_ref[...].astype(BF16), wpp_ref[...], preferred_element_type=F32)
    y = x + gate * emb
    out_ref[...] = y * _rms_scale(y) * fg_ref[...]


def _ple(x2, p2, g, w_pg, w_pp, final_g, *, tm):
    m = x2.shape[0]
    return pl.pallas_call(
        _ple_kernel,
        grid=(m // tm,),
        in_specs=[
            pl.BlockSpec((tm, D_MODEL), lambda i: (i, 0)),
            pl.BlockSpec((tm, PLE_DIM), lambda i: (i, 0)),
            pl.BlockSpec((1, D_MODEL), lambda i: (0, 0)),
            pl.BlockSpec((D_MODEL, D_MODEL), lambda i: (0, 0)),
            pl.BlockSpec((PLE_DIM, D_MODEL), lambda i: (0, 0)),
            pl.BlockSpec((1, D_MODEL), lambda i: (0, 0)),
        ],
        out_specs=pl.BlockSpec((tm, D_MODEL), lambda i: (i, 0)),
        out_shape=jax.ShapeDtypeStruct((m, D_MODEL), F32),
        compiler_params=_params(("parallel",)),
        name="ple",
    )(x2, p2, g, w_pg, w_pp, final_g)


def _layer(x2, p2, seq, norm_mix_g, w_in, conv_a_w, conv_qkv_w, a_log, dt_bias, dn_norm_g,
           w_out, norm_ffn_g, w_up, conv_ffn_w, w_down, norm_ple_g, w_ple_gate, w_ple_proj,
           out_norm_g):
    row = lambda v: v.reshape(1, -1).astype(F32)
    lane_pad = lambda v: jnp.pad(row(v), ((0, 0), (0, LANES - v.shape[-1])))
    w_main = w_in.astype(BF16)
    w_gate = jnp.pad(w_in[:, PROJ_COLS:], ((0, 0), (0, LANES - 2 * DN_HEADS))).astype(BF16)

    proj, gates = _in_proj(x2, row(norm_mix_g), w_main, w_gate, lane_pad(a_log), lane_pad(dt_bias),
                           tm=IN_TM, tn=IN_TN)
    yb, (w_out_b, w_up_b, w_down_b, w_pg_b) = _deltanet(
        proj, gates, conv_qkv_w.astype(F32), row(dn_norm_g), (w_out, w_up, w_down, w_ple_gate),
        seq=seq, ts=DN_TS, chunk=DN_CHUNK)
    x2 = _out_proj(proj, yb, x2, conv_a_w.astype(F32), w_out_b, tm=OUT_TM, seq=seq)
    x2 = _ffn(x2, row(norm_ffn_g), w_up_b, conv_ffn_w.astype(F32), w_down_b,
              tm=FFN_TM, bn=FFN_BN, seq=seq)
    return _ple(x2, p2, row(norm_ple_g), w_pg_b, w_ple_proj.astype(BF16),
                row(out_norm_g), tm=PLE_TM)


def kernel(x, p, norm_mix_g, w_in, conv_a_w, conv_qkv_w, a_log, dt_bias, dn_norm_g, w_out,
           norm_ffn_g, w_up, conv_ffn_w, w_down, norm_ple_g, w_ple_gate, w_ple_proj, final_norm_g):
    batch, seq, d_model = x.shape
    depth = p.shape[0]
    assert depth == 1 and d_model == D_MODEL
    x2 = x.reshape(batch * seq, d_model)
    p2 = p[0].reshape(batch * seq, PLE_DIM)
    out = _layer(x2, p2, seq, norm_mix_g[0], w_in[0], conv_a_w[0], conv_qkv_w[0], a_log[0],
                 dt_bias[0], dn_norm_g[0], w_out[0], norm_ffn_g[0], w_up[0], conv_ffn_w[0],
                 w_down[0], norm_ple_g[0], w_ple_gate[0], w_ple_proj[0], final_norm_g)
    return out.reshape(batch, seq, d_model)
```

```python
import functools

import jax
import jax.numpy as jnp
from jax import lax
from jax.experimental import pallas as pl
from jax.experimental.pallas import tpu as pltpu

D_MODEL = 2048
CONV_WIDTH = 1024
CONV_K = 3
DN_HEADS = 8
DN_HEAD_DIM = 128
DN_WIDTH = DN_HEADS * DN_HEAD_DIM
DN_CONV_K = 4
D_FF = 5632
FFN_CONV_K = 3
PLE_DIM = 256
EPS = 1e-6
PROJ_COLS = 3 * CONV_WIDTH + 4 * DN_WIDTH

LANES = 128
SUBLANES_F32 = 8
SUBLANES_BF16 = 16
V7X_VMEM_BYTES = 64 * 1024 * 1024
VMEM_LIMIT_BYTES = V7X_VMEM_BYTES - 2 * 1024 * 1024

IN_TM, IN_TN = 1024, 1024
DN_TS = 256
OUT_TM = 512
FFN_TM, FFN_BN = 1024, 512
PLE_TM = 512

DN_CHUNK = 128
INV_BASE = 8

F32 = jnp.float32
BF16 = jnp.bfloat16


def _mm(a, b):
    return jnp.dot(a.astype(BF16), b.astype(BF16), preferred_element_type=F32)


def _mm_nt(a, b):
    return lax.dot_general(a.astype(BF16), b.astype(BF16), (((1,), (1,)), ((), ())),
                           preferred_element_type=F32)


def _mm_exact(a, b):
    return jnp.dot(a, b, precision=lax.Precision.HIGHEST, preferred_element_type=F32)


def _rms_scale(x):
    return lax.rsqrt(jnp.mean(x * x, axis=-1, keepdims=True) + EPS)


def _silu(x):
    return x * jax.nn.sigmoid(x)


def _params(semantics):
    return pltpu.CompilerParams(dimension_semantics=semantics, vmem_limit_bytes=VMEM_LIMIT_BYTES)


def _in_proj_kernel(x_ref, g_ref, w_ref, wg_ref, alog_ref, dtb_ref, proj_ref, gates_ref, hs_ref):
    @pl.when(pl.program_id(1) == 0)
    def _():
        x = x_ref[...]
        hb = (x * _rms_scale(x) * g_ref[...]).astype(BF16)
        hs_ref[...] = hb
        raw = jnp.dot(hb, wg_ref[...], preferred_element_type=F32)
        a = raw + dtb_ref[...]
        softplus = jnp.maximum(a, 0.0) + jnp.log(1.0 + jnp.exp(-jnp.abs(a)))
        decay = -jnp.exp(alog_ref[...]) * softplus
        lane = lax.broadcasted_iota(jnp.int32, raw.shape, 1)
        gates_ref[...] = jnp.where(lane < DN_HEADS, decay, jax.nn.sigmoid(raw))

    tn = proj_ref.shape[1]
    for block in range(PROJ_COLS // tn):
        @pl.when(pl.program_id(1) == block)
        def _(block=block):
            proj_ref[...] = jnp.dot(hs_ref[...], w_ref[:, pl.ds(block * tn, tn)],
                                    preferred_element_type=F32)


def _in_proj(x2, g, w_main, w_gate, alog_pad, dtb_pad, *, tm, tn):
    m = x2.shape[0]
    return pl.pallas_call(
        _in_proj_kernel,
        grid=(m // tm, PROJ_COLS // tn),
        in_specs=[
            pl.BlockSpec((tm, D_MODEL), lambda i, j: (i, 0)),
            pl.BlockSpec((1, D_MODEL), lambda i, j: (0, 0)),
            pl.BlockSpec(w_main.shape, lambda i, j: (0, 0), pipeline_mode=pl.Buffered(1)),
            pl.BlockSpec((D_MODEL, LANES), lambda i, j: (0, 0)),
            pl.BlockSpec((1, LANES), lambda i, j: (0, 0)),
            pl.BlockSpec((1, LANES), lambda i, j: (0, 0)),
        ],
        out_specs=[
            pl.BlockSpec((tm, tn), lambda i, j: (i, j)),
            pl.BlockSpec((tm, LANES), lambda i, j: (i, 0)),
        ],
        out_shape=[
            jax.ShapeDtypeStruct((m, PROJ_COLS), F32),
            jax.ShapeDtypeStruct((m, LANES), F32),
        ],
        scratch_shapes=[pltpu.VMEM((tm, D_MODEL), BF16)],
        compiler_params=_params(("parallel", "arbitrary")),
        name="in_proj",
    )(x2, g, w_main, w_gate, alog_pad, dtb_pad)


def _each(fn, *lists):
    return [fn(*args) for args in zip(*lists)]


def _unit_lower_inverse(lows, row, col):
    c = lows[0].shape[0]
    eye = (row == col).astype(F32)
    base_shift = INV_BASE.bit_length() - 1
    base_mask = (row >> base_shift) == (col >> base_shift)
    l0 = [jnp.where(base_mask, low, 0.0) for low in lows]
    l2 = _each(_mm, l0, l0)
    l4 = _each(_mm, l2, l2)
    l3 = _each(_mm, l0, l2)
    p1 = _each(lambda a, b, d: eye - a + b - d, l0, l2, l3)
    inv = _each(lambda p, pl4: p + pl4, p1, _each(_mm, p1, l4))
    shift = base_shift
    while (1 << shift) < c:
        off_mask = (((row >> (shift + 1)) == (col >> (shift + 1)))
                    & ((row >> shift) != (col >> shift)))
        off_inv = _each(lambda low, t: _mm(jnp.where(off_mask, low, 0.0), t), lows, inv)
        inv = _each(lambda t, x: t - _mm(t, x), inv, off_inv)
        shift += 1
    return inv


def _deltanet_kernel(q_ref, k_ref, v_ref, z_ref, gates_ref, cw_ref, ng_ref, *refs,
                     ts, chunk, n_cast):
    cast_src = refs[:n_cast]
    out_ref = refs[n_cast]
    cast_dst = refs[n_cast + 1:2 * n_cast + 1]
    qbuf, kbuf, vbuf, state_ref, u_s, wq_s, qk_s, kdt_s = refs[2 * n_cast + 1:]

    halo = SUBLANES_F32
    first = pl.program_id(1) == 0

    @pl.when(first)
    def _():
        state_ref[...] = jnp.zeros_like(state_ref)
        for buf in (qbuf, kbuf, vbuf):
            buf[pl.ds(0, halo), :] = jnp.zeros((halo, DN_WIDTH), F32)

    @pl.when(jnp.logical_not(first))
    def _():
        for buf in (qbuf, kbuf, vbuf):
            buf[pl.ds(0, halo), :] = buf[pl.ds(ts, halo), :]

    for src, buf in ((q_ref, qbuf), (k_ref, kbuf), (v_ref, vbuf)):
        buf[pl.ds(halo, ts), :] = src[...]
    for src, dst in zip(cast_src, cast_dst):
        dst[...] = src[...].astype(dst.dtype)

    row = lax.broadcasted_iota(jnp.int32, (chunk, chunk), 0)
    col = lax.broadcasted_iota(jnp.int32, (chunk, chunk), 1)
    lower_incl = (row >= col).astype(F32)
    causal = row >= col
    strict = row > col
    scale = DN_HEAD_DIM ** -0.5
    heads = range(DN_HEADS)
    chunks = range(ts // chunk)
    head_cols = [slice(h * DN_HEAD_DIM, (h + 1) * DN_HEAD_DIM) for h in heads]
    pairs = [(c, h) for c in chunks for h in heads]
    slot = lambda c, h: c * DN_HEADS + h

    def l2_normalized(x, extra_scale):
        return x * (lax.rsqrt(jnp.sum(x * x, axis=-1, keepdims=True) + EPS) * extra_scale)

    def conv_silu(buf, c, which):
        w = [cw_ref[pl.ds(j, 1), pl.ds(which * DN_WIDTH, DN_WIDTH)] for j in range(DN_CONV_K)]
        cur = buf[pl.ds(c * chunk, chunk + halo), :]
        prev = pltpu.roll(cur, 1, axis=0)
        tail = pltpu.roll(w[1] * cur + w[0] * prev, 2, axis=0)
        return _silu((w[3] * cur + w[2] * prev + tail)[halo:, :])

    qa = [conv_silu(qbuf, c, 0) for c in chunks]
    ka = [conv_silu(kbuf, c, 1) for c in chunks]
    va = [conv_silu(vbuf, c, 2) for c in chunks]
    gts = [gates_ref[pl.ds(c * chunk, chunk), :] for c in chunks]
    gcum = [_mm_exact(lower_incl, g) for g in gts]
    gcum_t = [g.T for g in gcum]
    q = [l2_normalized(qa[c][:, head_cols[h]], scale) for c, h in pairs]
    k = [l2_normalized(ka[c][:, head_cols[h]], 1.0) for c, h in pairs]
    v = [va[c][:, head_cols[h]] for c, h in pairs]
    gc = [gcum[c][:, h:h + 1] for c, h in pairs]
    gr = [gcum_t[c][h:h + 1, :] for c, h in pairs]
    beta = [gts[c][:, DN_HEADS + h:DN_HEADS + h + 1] for c, h in pairs]
    g_last = [g[chunk - 1:chunk, :] for g in gc]
    decay = _each(lambda a, b: jnp.exp(jnp.where(causal, a - b, -1e30)), gc, gr)
    kk = _each(_mm_nt, k, k)
    qk = _each(lambda a, b, d: _mm_nt(a, b) * d, q, k, decay)
    low = _each(lambda m, d, b: jnp.where(strict, m * d * b, 0.0), kk, decay, beta)
    inv = _unit_lower_inverse(low, row, col)
    eg = _each(jnp.exp, gc)
    rhs = _each(lambda vv, kx, b, e: jnp.concatenate([vv * b, kx * (b * e)], axis=1),
                v, k, beta, eg)
    sol = _each(_mm, inv, rhs)
    for (c, h), s, qx, e, a, kx, gl, g in zip(pairs, sol, q, eg, qk, k, g_last, gc):
        i = slot(c, h)
        u_s[i] = s[:, :DN_HEAD_DIM]
        wq_s[i] = jnp.concatenate([s[:, DN_HEAD_DIM:], qx * e], axis=0).astype(BF16)
        qk_s[i] = a.astype(BF16)
        kdt_s[i] = (kx * jnp.exp(gl - g)).T.astype(BF16)
    carry_decay = _each(jnp.exp, g_last)

    state = [state_ref[h] for h in heads]
    for c in chunks:
        ids = [slot(c, h) for h in heads]
        ws_qs = [jnp.dot(wq_s[i], st.astype(BF16), preferred_element_type=F32)
                 for i, st in zip(ids, state)]
        v_new = [(u_s[i] - x[:chunk]).astype(BF16) for i, x in zip(ids, ws_qs)]
        state = [st * carry_decay[i] + jnp.dot(kdt_s[i], vn, preferred_element_type=F32)
                 for i, st, vn in zip(ids, state, v_new)]
        o = [x[chunk:] + jnp.dot(qk_s[i], vn, preferred_element_type=F32)
             for i, x, vn in zip(ids, ws_qs, v_new)]
        rows = pl.ds(c * chunk, chunk)
        for h, sl in enumerate(head_cols):
            y = o[h] * _rms_scale(o[h]) * ng_ref[...] * _silu(z_ref[rows, sl])
            out_ref[rows, sl] = y.astype(out_ref.dtype)
    for h in heads:
        state_ref[h] = state[h]


def _deltanet(proj, gates, conv_w, norm_g, cast_weights, *, seq, ts, chunk):
    m = proj.shape[0]
    tiles = seq // ts
    steps = (m // seq) * tiles
    pairs = (ts // chunk) * DN_HEADS
    col0 = 3 * CONV_WIDTH // DN_WIDTH
    row_map = lambda b, t: b * tiles + t
    qkvz = [pl.BlockSpec((ts, DN_WIDTH), functools.partial(lambda b, t, n: (b * tiles + t, col0 + n), n=n))
            for n in range(4)]
    cast_specs = []
    for w in cast_weights:
        share = 1 if w.shape[0] % (steps * SUBLANES_BF16) == 0 else 2
        assert w.shape[0] % (steps // share * SUBLANES_BF16) == 0, w.shape
        cast_specs.append(pl.BlockSpec(
            (w.shape[0] // (steps // share), w.shape[1]),
            functools.partial(lambda b, t, share: (row_map(b, t) // share, 0), share=share)))
    outs = pl.pallas_call(
        functools.partial(_deltanet_kernel, ts=ts, chunk=chunk, n_cast=len(cast_weights)),
        grid=(m // seq, tiles),
        in_specs=qkvz + [
            pl.BlockSpec((ts, LANES), lambda b, t: (row_map(b, t), 0)),
            pl.BlockSpec((DN_CONV_K, 3 * DN_WIDTH), lambda b, t: (0, 0)),
            pl.BlockSpec((1, DN_HEAD_DIM), lambda b, t: (0, 0)),
        ] + cast_specs,
        out_specs=[pl.BlockSpec((ts, DN_WIDTH), lambda b, t: (row_map(b, t), 0))] + cast_specs,
        out_shape=[jax.ShapeDtypeStruct((m, DN_WIDTH), BF16)]
        + [jax.ShapeDtypeStruct(w.shape, BF16) for w in cast_weights],
        scratch_shapes=[pltpu.VMEM((ts + SUBLANES_F32, DN_WIDTH), F32)] * 3 + [
            pltpu.VMEM((DN_HEADS, DN_HEAD_DIM, DN_HEAD_DIM), F32),
            pltpu.VMEM((pairs, chunk, DN_HEAD_DIM), F32),
            pltpu.VMEM((pairs, 2 * chunk, DN_HEAD_DIM), BF16),
            pltpu.VMEM((pairs, chunk, chunk), BF16),
            pltpu.VMEM((pairs, DN_HEAD_DIM, chunk), BF16),
        ],
        compiler_params=_params(("parallel", "arbitrary")),
        name="deltanet",
    )(proj, proj, proj, proj, gates, conv_w, norm_g, *cast_weights)
    return outs[0], outs[1:]


RING = 3


def _out_proj_kernel(proj_hbm, yb_hbm, x_hbm, hax_ref, hac_ref, cw_ref, w_ref, out_ref,
                     a_buf, yb_buf, x_buf, sems, pbuf, *, tm, seq, steps):
    halo = SUBLANES_F32
    step = pl.program_id(0)

    def tile_copies(t, slot):
        rows = pl.ds(t * tm, tm)
        copies = [pltpu.make_async_copy(proj_hbm.at[rows, pl.ds(n * CONV_WIDTH, CONV_WIDTH)],
                                        a_buf.at[slot, n], sems.at[n, slot]) for n in range(3)]
        copies.append(pltpu.make_async_copy(yb_hbm.at[rows, :], yb_buf.at[slot], sems.at[3, slot]))
        copies.append(pltpu.make_async_copy(x_hbm.at[rows, :], x_buf.at[slot], sems.at[4, slot]))
        return copies

    @pl.when(step == 0)
    def _():
        for t in range(RING - 1):
            for copy in tile_copies(t, t):
                copy.start()

    ahead = step + (RING - 1)

    @pl.when(ahead < steps)
    def _():
        for copy in tile_copies(ahead, ahead % RING):
            copy.start()

    slot = step % RING
    for copy in tile_copies(step, slot):
        copy.wait()

    acc_b = jnp.dot(yb_buf[slot], w_ref[pl.ds(CONV_WIDTH, DN_WIDTH), :], preferred_element_type=F32)
    seq_start = (step * tm) % seq == 0
    pbuf[pl.ds(0, halo), :] = jnp.where(seq_start, 0.0, hac_ref[...] * hax_ref[...])
    pbuf[pl.ds(halo, tm), :] = a_buf[slot, 2] * a_buf[slot, 0]
    acc = None
    for j in range(CONV_K):
        term = pbuf[pl.ds(halo - (CONV_K - 1) + j, tm), :] * cw_ref[pl.ds(j, 1), :]
        acc = term if acc is None else acc + term
    ya = (a_buf[slot, 1] * acc).astype(BF16)
    out_ref[...] = (x_buf[slot] + acc_b
                    + jnp.dot(ya, w_ref[pl.ds(0, CONV_WIDTH), :], preferred_element_type=F32))


def _out_proj(proj, yb, x2, conv_w, w_out, *, tm, seq):
    m = x2.shape[0]
    steps = m // tm
    assert steps >= RING
    halo = SUBLANES_F32
    halo_row = lambda i: jnp.maximum(i * (tm // halo) - 1, 0)
    return pl.pallas_call(
        functools.partial(_out_proj_kernel, tm=tm, seq=seq, steps=steps),
        grid=(steps,),
        in_specs=[
            pl.BlockSpec(memory_space=pl.ANY),
            pl.BlockSpec(memory_space=pl.ANY),
            pl.BlockSpec(memory_space=pl.ANY),
            pl.BlockSpec((halo, CONV_WIDTH), lambda i: (halo_row(i), 0)),
            pl.BlockSpec((halo, CONV_WIDTH), lambda i: (halo_row(i), 2)),
            pl.BlockSpec((CONV_K, CONV_WIDTH), lambda i: (0, 0)),
            pl.BlockSpec((D_MODEL, D_MODEL), lambda i: (0, 0), pipeline_mode=pl.Buffered(1)),
        ],
        out_specs=pl.BlockSpec((tm, D_MODEL), lambda i: (i, 0)),
        out_shape=jax.ShapeDtypeStruct((m, D_MODEL), F32),
        scratch_shapes=[
            pltpu.VMEM((RING, 3, tm, CONV_WIDTH), F32),
            pltpu.VMEM((RING, tm, DN_WIDTH), BF16),
            pltpu.VMEM((RING, tm, D_MODEL), F32),
            pltpu.SemaphoreType.DMA((5, RING)),
            pltpu.VMEM((tm + halo, CONV_WIDTH), F32),
        ],
        compiler_params=_params(("arbitrary",)),
        name="out_proj",
    )(proj, yb, x2, proj, proj, conv_w, w_out)


def _ffn_kernel(x_ref, g_ref, wg_ref, wv_ref, cg_ref, cv_ref, wd_ref, out_ref,
                hs_ref, ubuf, carry, *, tm, seq, nj):
    halo = SUBLANES_F32
    i = pl.program_id(0)
    j = pl.program_id(1)
    seq_start = (i * tm) % seq == 0
    cur = i % 2

    def normed_input():
        x = x_ref[...]
        return (x * _rms_scale(x) * g_ref[...]).astype(BF16)

    @pl.when(jnp.logical_and(i == 0, j == 0))
    def _():
        carry[...] = jnp.zeros_like(carry)
        hs_ref[0] = normed_input()

    @pl.when(j == 0)
    def _():
        out_ref[...] = x_ref[...]

    def up_conv(w_ref, cw_ref, which, cs):
        up = jnp.dot(hs_ref[cur], w_ref[:, cs], preferred_element_type=F32)
        ubuf[which, pl.ds(0, halo), cs] = jnp.where(seq_start, 0.0, carry[which, j, :, cs])
        ubuf[which, pl.ds(halo, tm), cs] = up
        carry[which, j, :, cs] = up[tm - halo:, :]
        acc = None
        for t in range(FFN_CONV_K):
            term = (ubuf[which, pl.ds(halo - (FFN_CONV_K - 1) + t, tm), cs]
                    * cw_ref[pl.ds(t, 1), cs])
            acc = term if acc is None else acc + term
        return acc

    def block():
        bn = wd_ref.shape[0]
        halves = [pl.ds(h * (bn // 2), bn // 2) for h in range(2)]
        gates = [_silu(up_conv(wg_ref, cg_ref, 0, cs)) for cs in halves]
        acts = [(g * up_conv(wv_ref, cv_ref, 1, cs)).astype(BF16) for g, cs in zip(gates, halves)]
        for cs, act in zip(halves, acts):
            out_ref[...] += jnp.dot(act, wd_ref[cs, :], preferred_element_type=F32)

    @pl.when(j < nj - 1)
    def _():
        block()

    @pl.when(j == nj - 1)
    def _():
        hs_ref[1 - cur] = normed_input()
        block()


def _ffn(x1, g, w_up, conv_w, w_down, *, tm, bn, seq):
    m = x1.shape[0]
    halo = SUBLANES_F32
    nj = D_FF // bn
    return pl.pallas_call(
        functools.partial(_ffn_kernel, tm=tm, seq=seq, nj=nj),
        grid=(m // tm, nj),
        in_specs=[
            pl.BlockSpec((tm, D_MODEL),
                         lambda i, j: (jnp.minimum(i + (j == nj - 1).astype(jnp.int32), m // tm - 1), 0)),
            pl.BlockSpec((1, D_MODEL), lambda i, j: (0, 0)),
            pl.BlockSpec((D_MODEL, bn), lambda i, j: (0, j)),
            pl.BlockSpec((D_MODEL, bn), lambda i, j: (0, j + nj)),
            pl.BlockSpec((FFN_CONV_K, bn), lambda i, j: (0, j)),
            pl.BlockSpec((FFN_CONV_K, bn), lambda i, j: (0, j + nj)),
            pl.BlockSpec((bn, D_MODEL), lambda i, j: (j, 0)),
        ],
        out_specs=pl.BlockSpec((tm, D_MODEL), lambda i, j: (i, 0)),
        out_shape=jax.ShapeDtypeStruct((m, D_MODEL), F32),
        scratch_shapes=[pltpu.VMEM((2, tm, D_MODEL), BF16),
                        pltpu.VMEM((2, tm + halo, bn), F32),
                        pltpu.VMEM((2, nj, halo, bn), F32)],
        compiler_params=_params(("arbitrary", "arbitrary")),
        name="conv_ffn",
    )(x1, g, w_up, w_up, conv_w, conv_w, w_down)


def _ple_kernel(x_ref, p_ref, g_ref, wpg_ref, wpp_ref, fg_ref, out_ref):
    x = x_ref[...]
    hb = (x * _rms_scale(x) * g_ref[...]).astype(BF16)
    gate = jax.nn.sigmoid(jnp.dot(hb, wpg_ref[...], preferred_element_type=F32))
    emb = jnp.dot(p_ref[...].astype(BF16), wpp_ref[...], preferred_element_type=F32)
    y = x + gate * emb
    out_ref[...] = y * _rms_scale(y) * fg_ref[...]


def _ple(x2, p2, g, w_pg, w_pp, final_g, *, tm):
    m = x2.shape[0]
    return pl.pallas_call(
        _ple_kernel,
        grid=(m // tm,),
        in_specs=[
            pl.BlockSpec((tm, D_MODEL), lambda i: (i, 0)),
            pl.BlockSpec((tm, PLE_DIM), lambda i: (i, 0)),
            pl.BlockSpec((1, D_MODEL), lambda i: (0, 0)),
            pl.BlockSpec((D_MODEL, D_MODEL), lambda i: (0, 0)),
            pl.BlockSpec((PLE_DIM, D_MODEL), lambda i: (0, 0)),
            pl.BlockSpec((1, D_MODEL), lambda i: (0, 0)),
        ],
        out_specs=pl.BlockSpec((tm, D_MODEL), lambda i: (i, 0)),
        out_shape=jax.ShapeDtypeStruct((m, D_MODEL), F32),
        compiler_params=_params(("parallel",)),
        name="ple",
    )(x2, p2, g, w_pg, w_pp, final_g)


def _layer(x2, p2, seq, norm_mix_g, w_in, conv_a_w, conv_qkv_w, a_log, dt_bias, dn_norm_g,
           w_out, norm_ffn_g, w_up, conv_ffn_w, w_down, norm_ple_g, w_ple_gate, w_ple_proj,
           out_norm_g):
    row = lambda v: v.reshape(1, -1).astype(F32)
    lane_pad = lambda v: jnp.pad(row(v), ((0, 0), (0, LANES - v.shape[-1])))
    w_main = w_in.astype(BF16)
    w_gate = jnp.pad(w_in[:, PROJ_COLS:], ((0, 0), (0, LANES - 2 * DN_HEADS))).astype(BF16)

    proj, gates = _in_proj(x2, row(norm_mix_g), w_main, w_gate, lane_pad(a_log), lane_pad(dt_bias),
                           tm=IN_TM, tn=IN_TN)
    yb, (w_out_b, w_up_b, w_down_b, w_pg_b) = _deltanet(
        proj, gates, conv_qkv_w.astype(F32), row(dn_norm_g), (w_out, w_up, w_down, w_ple_gate),
        seq=seq, ts=DN_TS, chunk=DN_CHUNK)
    x2 = _out_proj(proj, yb, x2, conv_a_w.astype(F32), w_out_b, tm=OUT_TM, seq=seq)
    x2 = _ffn(x2, row(norm_ffn_g), w_up_b, conv_ffn_w.astype(F32), w_down_b,
              tm=FFN_TM, bn=FFN_BN, seq=seq)
    return _ple(x2, p2, row(norm_ple_g), w_pg_b, w_ple_proj.astype(BF16),
                row(out_norm_g), tm=PLE_TM)


def kernel(x, p, norm_mix_g, w_in, conv_a_w, conv_qkv_w, a_log, dt_bias, dn_norm_g, w_out,
           norm_ffn_g, w_up, conv_ffn_w, w_down, norm_ple_g, w_ple_gate, w_ple_proj, final_norm_g):
    batch, seq, d_model = x.shape
    depth = p.shape[0]
    assert depth == 1 and d_model == D_MODEL
    x2 = x.reshape(batch * seq, d_model)
    p2 = p[0].reshape(batch * seq, PLE_DIM)
    out = _layer(x2, p2, seq, norm_mix_g[0], w_in[0], conv_a_w[0], conv_qkv_w[0], a_log[0],
                 dt_bias[0], dn_norm_g[0], w_out[0], norm_ffn_g[0], w_up[0], conv_ffn_w[0],
                 w_down[0], norm_ple_g[0], w_ple_gate[0], w_ple_proj[0], final_norm_g)
    return out.reshape(batch, seq, d_model)
```

```python
import functools

import jax
import jax.numpy as jnp
from jax import lax
from jax.experimental import pallas as pl
from jax.experimental.pallas import tpu as pltpu

D_MODEL = 2048
CONV_WIDTH = 1024
CONV_K = 3
DN_HEADS = 8
DN_HEAD_DIM = 128
DN_WIDTH = DN_HEADS * DN_HEAD_DIM
DN_CONV_K = 4
D_FF = 5632
FFN_CONV_K = 3
PLE_DIM = 256
EPS = 1e-6
PROJ_COLS = 3 * CONV_WIDTH + 4 * DN_WIDTH

LANES = 128
SUBLANES_F32 = 8
SUBLANES_BF16 = 16
V7X_VMEM_BYTES = 64 * 1024 * 1024
VMEM_LIMIT_BYTES = V7X_VMEM_BYTES - 2 * 1024 * 1024

IN_TM, IN_TN = 1024, 1024
DN_TS = 256
OUT_TM = 512
FFN_TM, FFN_BN = 1024, 512
PLE_TM = 512

DN_CHUNK = 128
INV_BASE = 8

F32 = jnp.float32
BF16 = jnp.bfloat16


def _mm(a, b):
    return jnp.dot(a.astype(BF16), b.astype(BF16), preferred_element_type=F32)


def _mm_nt(a, b):
    return lax.dot_general(a.astype(BF16), b.astype(BF16), (((1,), (1,)), ((), ())),
                           preferred_element_type=F32)


def _mm_exact(a, b):
    return jnp.dot(a, b, precision=lax.Precision.HIGHEST, preferred_element_type=F32)


def _rms_scale(x):
    return lax.rsqrt(jnp.mean(x * x, axis=-1, keepdims=True) + EPS)


def _silu(x):
    return x * jax.nn.sigmoid(x)


def _params(semantics):
    return pltpu.CompilerParams(dimension_semantics=semantics, vmem_limit_bytes=VMEM_LIMIT_BYTES)


def _in_proj_kernel(x_hbm, g_ref, w_ref, wg_ref, alog_ref, dtb_ref, proj_ref, gates_ref,
                    hs_ref, x_buf, x_sem, *, tm, row_tiles):
    i = pl.program_id(0)

    def x_copy(t):
        return pltpu.make_async_copy(x_hbm.at[pl.ds(t * tm, tm), :], x_buf.at[t % 2], x_sem.at[t % 2])

    @pl.when(jnp.logical_and(i == 0, pl.program_id(1) == 0))
    def _():
        x_copy(0).start()

    @pl.when(jnp.logical_and(i + 1 < row_tiles, pl.program_id(1) == 0))
    def _():
        x_copy(i + 1).start()

    @pl.when(pl.program_id(1) == 0)
    def _():
        x_copy(i).wait()
        x = x_buf[i % 2]
        hb = (x * _rms_scale(x) * g_ref[...]).astype(BF16)
        hs_ref[...] = hb
        raw = jnp.dot(hb, wg_ref[...], preferred_element_type=F32)
        a = raw + dtb_ref[...]
        softplus = jnp.maximum(a, 0.0) + jnp.log(1.0 + jnp.exp(-jnp.abs(a)))
        decay = -jnp.exp(alog_ref[...]) * softplus
        lane = lax.broadcasted_iota(jnp.int32, raw.shape, 1)
        gates_ref[...] = jnp.where(lane < DN_HEADS, decay, jax.nn.sigmoid(raw))

    tn = proj_ref.shape[1]
    for block in range(PROJ_COLS // tn):
        @pl.when(pl.program_id(1) == block)
        def _(block=block):
            proj_ref[...] = jnp.dot(hs_ref[...], w_ref[:, pl.ds(block * tn, tn)],
                                    preferred_element_type=F32)


def _in_proj(x2, g, w_main, w_gate, alog_pad, dtb_pad, *, tm, tn):
    m = x2.shape[0]
    return pl.pallas_call(
        functools.partial(_in_proj_kernel, tm=tm, row_tiles=m // tm),
        grid=(m // tm, PROJ_COLS // tn),
        in_specs=[
            pl.BlockSpec(memory_space=pl.ANY),
            pl.BlockSpec((1, D_MODEL), lambda i, j: (0, 0)),
            pl.BlockSpec(w_main.shape, lambda i, j: (0, 0), pipeline_mode=pl.Buffered(1)),
            pl.BlockSpec((D_MODEL, LANES), lambda i, j: (0, 0)),
            pl.BlockSpec((1, LANES), lambda i, j: (0, 0)),
            pl.BlockSpec((1, LANES), lambda i, j: (0, 0)),
        ],
        out_specs=[
            pl.BlockSpec((tm, tn), lambda i, j: (i, j)),
            pl.BlockSpec((tm, LANES), lambda i, j: (i, 0)),
        ],
        out_shape=[
            jax.ShapeDtypeStruct((m, PROJ_COLS), F32),
            jax.ShapeDtypeStruct((m, LANES), F32),
        ],
        scratch_shapes=[pltpu.VMEM((tm, D_MODEL), BF16),
                        pltpu.VMEM((2, tm, D_MODEL), F32),
                        pltpu.SemaphoreType.DMA((2,))],
        compiler_params=_params(("arbitrary", "arbitrary")),
        name="in_proj",
    )(x2, g, w_main, w_gate, alog_pad, dtb_pad)


def _each(fn, *lists):
    return [fn(*args) for args in zip(*lists)]


def _unit_lower_inverse(lows, row, col):
    c = lows[0].shape[0]
    eye = (row == col).astype(F32)
    base_shift = INV_BASE.bit_length() - 1
    base_mask = (row >> base_shift) == (col >> base_shift)
    l0 = [jnp.where(base_mask, low, 0.0) for low in lows]
    l2 = _each(_mm, l0, l0)
    l4 = _each(_mm, l2, l2)
    l3 = _each(_mm, l0, l2)
    p1 = _each(lambda a, b, d: eye - a + b - d, l0, l2, l3)
    inv = _each(lambda p, pl4: p + pl4, p1, _each(_mm, p1, l4))
    shift = base_shift
    while (1 << shift) < c:
        off_mask = (((row >> (shift + 1)) == (col >> (shift + 1)))
                    & ((row >> shift) != (col >> shift)))
        off_inv = _each(lambda low, t: _mm(jnp.where(off_mask, low, 0.0), t), lows, inv)
        inv = _each(lambda t, x: t - _mm(t, x), inv, off_inv)
        shift += 1
    return inv


def _deltanet_kernel(q_ref, k_ref, v_ref, z_ref, gates_ref, cw_ref, ng_ref, *refs,
                     ts, chunk, n_cast):
    cast_src = refs[:n_cast]
    out_ref = refs[n_cast]
    cast_dst = refs[n_cast + 1:2 * n_cast + 1]
    qbuf, kbuf, vbuf, state_ref, u_s, wq_s, qk_s, kdt_s = refs[2 * n_cast + 1:]

    halo = SUBLANES_F32
    first = pl.program_id(1) == 0

    @pl.when(first)
    def _():
        state_ref[...] = jnp.zeros_like(state_ref)
        for buf in (qbuf, kbuf, vbuf):
            buf[pl.ds(0, halo), :] = jnp.zeros((halo, DN_WIDTH), F32)

    @pl.when(jnp.logical_not(first))
    def _():
        for buf in (qbuf, kbuf, vbuf):
            buf[pl.ds(0, halo), :] = buf[pl.ds(ts, halo), :]

    for src, buf in ((q_ref, qbuf), (k_ref, kbuf), (v_ref, vbuf)):
        buf[pl.ds(halo, ts), :] = src[...]
    for src, dst in zip(cast_src, cast_dst):
        dst[...] = src[...].astype(dst.dtype)

    row = lax.broadcasted_iota(jnp.int32, (chunk, chunk), 0)
    col = lax.broadcasted_iota(jnp.int32, (chunk, chunk), 1)
    lower_incl = (row >= col).astype(F32)
    causal = row >= col
    strict = row > col
    scale = DN_HEAD_DIM ** -0.5
    heads = range(DN_HEADS)
    chunks = range(ts // chunk)
    head_cols = [slice(h * DN_HEAD_DIM, (h + 1) * DN_HEAD_DIM) for h in heads]
    pairs = [(c, h) for c in chunks for h in heads]
    slot = lambda c, h: c * DN_HEADS + h

    def l2_normalized(x, extra_scale):
        return x * (lax.rsqrt(jnp.sum(x * x, axis=-1, keepdims=True) + EPS) * extra_scale)

    def conv_silu(buf, c, which):
        w = [cw_ref[pl.ds(j, 1), pl.ds(which * DN_WIDTH, DN_WIDTH)] for j in range(DN_CONV_K)]
        cur = buf[pl.ds(c * chunk, chunk + halo), :]
        prev = pltpu.roll(cur, 1, axis=0)
        tail = pltpu.roll(w[1] * cur + w[0] * prev, 2, axis=0)
        return _silu((w[3] * cur + w[2] * prev + tail)[halo:, :])

    qa = [conv_silu(qbuf, c, 0) for c in chunks]
    ka = [conv_silu(kbuf, c, 1) for c in chunks]
    va = [conv_silu(vbuf, c, 2) for c in chunks]
    gts = [gates_ref[pl.ds(c * chunk, chunk), :] for c in chunks]
    gcum = [_mm_exact(lower_incl, g) for g in gts]
    gcum_t = [g.T for g in gcum]
    q = [l2_normalized(qa[c][:, head_cols[h]], scale) for c, h in pairs]
    k = [l2_normalized(ka[c][:, head_cols[h]], 1.0) for c, h in pairs]
    v = [va[c][:, head_cols[h]] for c, h in pairs]
    gc = [gcum[c][:, h:h + 1] for c, h in pairs]
    gr = [gcum_t[c][h:h + 1, :] for c, h in pairs]
    beta = [gts[c][:, DN_HEADS + h:DN_HEADS + h + 1] for c, h in pairs]
    g_last = [g[chunk - 1:chunk, :] for g in gc]
    decay = _each(lambda a, b: jnp.exp(jnp.where(causal, a - b, -1e30)), gc, gr)
    kk = _each(_mm_nt, k, k)
    qk = _each(lambda a, b, d: _mm_nt(a, b) * d, q, k, decay)
    low = _each(lambda m, d, b: jnp.where(strict, m * d * b, 0.0), kk, decay, beta)
    inv = _unit_lower_inverse(low, row, col)
    eg = _each(jnp.exp, gc)
    rhs = _each(lambda vv, kx, b, e: jnp.concatenate([vv * b, kx * (b * e)], axis=1),
                v, k, beta, eg)
    sol = _each(_mm, inv, rhs)
    for (c, h), s, qx, e, a, kx, gl, g in zip(pairs, sol, q, eg, qk, k, g_last, gc):
        i = slot(c, h)
        u_s[i] = s[:, :DN_HEAD_DIM]
        wq_s[i] = jnp.concatenate([s[:, DN_HEAD_DIM:], qx * e], axis=0).astype(BF16)
        qk_s[i] = a.astype(BF16)
        kdt_s[i] = (kx * jnp.exp(gl - g)).T.astype(BF16)
    carry_decay = _each(jnp.exp, g_last)

    state = [state_ref[h] for h in heads]
    for c in chunks:
        ids = [slot(c, h) for h in heads]
        ws_qs = [jnp.dot(wq_s[i], st.astype(BF16), preferred_element_type=F32)
                 for i, st in zip(ids, state)]
        v_new = [(u_s[i] - x[:chunk]).astype(BF16) for i, x in zip(ids, ws_qs)]
        state = [st * carry_decay[i] + jnp.dot(kdt_s[i], vn, preferred_element_type=F32)
                 for i, st, vn in zip(ids, state, v_new)]
        o = [x[chunk:] + jnp.dot(qk_s[i], vn, preferred_element_type=F32)
             for i, x, vn in zip(ids, ws_qs, v_new)]
        rows = pl.ds(c * chunk, chunk)
        for h, sl in enumerate(head_cols):
            y = o[h] * _rms_scale(o[h]) * ng_ref[...] * _silu(z_ref[rows, sl])
            out_ref[rows, sl] = y.astype(out_ref.dtype)
    for h in heads:
        state_ref[h] = state[h]


def _deltanet(proj, gates, conv_w, norm_g, cast_weights, *, seq, ts, chunk):
    m = proj.shape[0]
    tiles = seq // ts
    steps = (m // seq) * tiles
    pairs = (ts // chunk) * DN_HEADS
    col0 = 3 * CONV_WIDTH // DN_WIDTH
    row_map = lambda b, t: b * tiles + t
    qkvz = [pl.BlockSpec((ts, DN_WIDTH), functools.partial(lambda b, t, n: (b * tiles + t, col0 + n), n=n))
            for n in range(4)]
    cast_specs = []
    for w in cast_weights:
        share = 1 if w.shape[0] % (steps * SUBLANES_BF16) == 0 else 2
        assert w.shape[0] % (steps // share * SUBLANES_BF16) == 0, w.shape
        cast_specs.append(pl.BlockSpec(
            (w.shape[0] // (steps // share), w.shape[1]),
            functools.partial(lambda b, t, share: (row_map(b, t) // share, 0), share=share)))
    outs = pl.pallas_call(
        functools.partial(_deltanet_kernel, ts=ts, chunk=chunk, n_cast=len(cast_weights)),
        grid=(m // seq, tiles),
        in_specs=qkvz + [
            pl.BlockSpec((ts, LANES), lambda b, t: (row_map(b, t), 0)),
            pl.BlockSpec((DN_CONV_K, 3 * DN_WIDTH), lambda b, t: (0, 0)),
            pl.BlockSpec((1, DN_HEAD_DIM), lambda b, t: (0, 0)),
        ] + cast_specs,
        out_specs=[pl.BlockSpec((ts, DN_WIDTH), lambda b, t: (row_map(b, t), 0))] + cast_specs,
        out_shape=[jax.ShapeDtypeStruct((m, DN_WIDTH), BF16)]
        + [jax.ShapeDtypeStruct(w.shape, BF16) for w in cast_weights],
        scratch_shapes=[pltpu.VMEM((ts + SUBLANES_F32, DN_WIDTH), F32)] * 3 + [
            pltpu.VMEM((DN_HEADS, DN_HEAD_DIM, DN_HEAD_DIM), F32),
            pltpu.VMEM((pairs, chunk, DN_HEAD_DIM), F32),
            pltpu.VMEM((pairs, 2 * chunk, DN_HEAD_DIM), BF16),
            pltpu.VMEM((pairs, chunk, chunk), BF16),
            pltpu.VMEM((pairs, DN_HEAD_DIM, chunk), BF16),
        ],
        compiler_params=_params(("parallel", "arbitrary")),
        name="deltanet",
    )(proj, proj, proj, proj, gates, conv_w, norm_g, *cast_weights)
    return outs[0], outs[1:]


RING = 3


def _out_proj_kernel(proj_hbm, yb_hbm, x_hbm, hax_ref, hac_ref, cw_ref, w_ref, out_ref,
                     a_buf, yb_buf, x_buf, sems, pbuf, *, tm, seq, steps):
    halo = SUBLANES_F32
    step = pl.program_id(0)

    def tile_copies(t, slot):
        rows = pl.ds(t * tm, tm)
        copies = [pltpu.make_async_copy(proj_hbm.at[rows, pl.ds(n * CONV_WIDTH, CONV_WIDTH)],
                                        a_buf.at[slot, n], sems.at[n, slot]) for n in range(3)]
        copies.append(pltpu.make_async_copy(yb_hbm.at[rows, :], yb_buf.at[slot], sems.at[3, slot]))
        copies.append(pltpu.make_async_copy(x_hbm.at[rows, :], x_buf.at[slot], sems.at[4, slot]))
        return copies

    @pl.when(step == 0)
    def _():
        for t in range(RING - 1):
            for copy in tile_copies(t, t):
                copy.start()

    ahead = step + (RING - 1)

    @pl.when(ahead < steps)
    def _():
        for copy in tile_copies(ahead, ahead % RING):
            copy.start()

    slot = step % RING
    for copy in tile_copies(step, slot):
        copy.wait()

    acc_b = jnp.dot(yb_buf[slot], w_ref[pl.ds(CONV_WIDTH, DN_WIDTH), :], preferred_element_type=F32)
    seq_start = (step * tm) % seq == 0
    pbuf[pl.ds(0, halo), :] = jnp.where(seq_start, 0.0, hac_ref[...] * hax_ref[...])
    pbuf[pl.ds(halo, tm), :] = a_buf[slot, 2] * a_buf[slot, 0]
    acc = None
    for j in range(CONV_K):
        term = pbuf[pl.ds(halo - (CONV_K - 1) + j, tm), :] * cw_ref[pl.ds(j, 1), :]
        acc = term if acc is None else acc + term
    ya = (a_buf[slot, 1] * acc).astype(BF16)
    out_ref[...] = (x_buf[slot] + acc_b
                    + jnp.dot(ya, w_ref[pl.ds(0, CONV_WIDTH), :], preferred_element_type=F32))


def _out_proj(proj, yb, x2, conv_w, w_out, *, tm, seq):
    m = x2.shape[0]
    steps = m // tm
    assert steps >= RING
    halo = SUBLANES_F32
    halo_row = lambda i: jnp.maximum(i * (tm // halo) - 1, 0)
    return pl.pallas_call(
        functools.partial(_out_proj_kernel, tm=tm, seq=seq, steps=steps),
        grid=(steps,),
        in_specs=[
            pl.BlockSpec(memory_space=pl.ANY),
            pl.BlockSpec(memory_space=pl.ANY),
            pl.BlockSpec(memory_space=pl.ANY),
            pl.BlockSpec((halo, CONV_WIDTH), lambda i: (halo_row(i), 0)),
            pl.BlockSpec((halo, CONV_WIDTH), lambda i: (halo_row(i), 2)),
            pl.BlockSpec((CONV_K, CONV_WIDTH), lambda i: (0, 0)),
            pl.BlockSpec((D_MODEL, D_MODEL), lambda i: (0, 0), pipeline_mode=pl.Buffered(1)),
        ],
        out_specs=pl.BlockSpec((tm, D_MODEL), lambda i: (i, 0)),
        out_shape=jax.ShapeDtypeStruct((m, D_MODEL), F32),
        scratch_shapes=[
            pltpu.VMEM((RING, 3, tm, CONV_WIDTH), F32),
            pltpu.VMEM((RING, tm, DN_WIDTH), BF16),
            pltpu.VMEM((RING, tm, D_MODEL), F32),
            pltpu.SemaphoreType.DMA((5, RING)),
            pltpu.VMEM((tm + halo, CONV_WIDTH), F32),
        ],
        compiler_params=_params(("arbitrary",)),
        name="out_proj",
    )(proj, yb, x2, proj, proj, conv_w, w_out)


def _ffn_kernel(x_ref, g_ref, wg_ref, wv_ref, cg_ref, cv_ref, wd_ref, out_ref,
                hs_ref, ubuf, carry, *, tm, seq):
    halo = SUBLANES_F32
    i = pl.program_id(0)
    j = pl.program_id(1)
    seq_start = (i * tm) % seq == 0

    @pl.when(jnp.logical_and(i == 0, j == 0))
    def _():
        carry[...] = jnp.zeros_like(carry)

    @pl.when(j == 0)
    def _():
        x = x_ref[...]
        hs_ref[...] = (x * _rms_scale(x) * g_ref[...]).astype(BF16)
        out_ref[...] = x

    def up_conv(w_ref, cw_ref, which, cs):
        up = jnp.dot(hs_ref[...], w_ref[:, cs], preferred_element_type=F32)
        ubuf[which, pl.ds(0, halo), cs] = jnp.where(seq_start, 0.0, carry[which, j, :, cs])
        ubuf[which, pl.ds(halo, tm), cs] = up
        carry[which, j, :, cs] = up[tm - halo:, :]
        acc = None
        for t in range(FFN_CONV_K):
            term = (ubuf[which, pl.ds(halo - (FFN_CONV_K - 1) + t, tm), cs]
                    * cw_ref[pl.ds(t, 1), cs])
            acc = term if acc is None else acc + term
        return acc

    bn = wd_ref.shape[0]
    halves = [pl.ds(h * (bn // 2), bn // 2) for h in range(2)]
    gates = [_silu(up_conv(wg_ref, cg_ref, 0, cs)) for cs in halves]
    acts = [(g * up_conv(wv_ref, cv_ref, 1, cs)).astype(BF16) for g, cs in zip(gates, halves)]
    for cs, act in zip(halves, acts):
        out_ref[...] += jnp.dot(act, wd_ref[cs, :], preferred_element_type=F32)


def _ffn(x1, g, w_up, conv_w, w_down, *, tm, bn, seq):
    m = x1.shape[0]
    halo = SUBLANES_F32
    nj = D_FF // bn
    return pl.pallas_call(
        functools.partial(_ffn_kernel, tm=tm, seq=seq),
        grid=(m // tm, nj),
        in_specs=[
            pl.BlockSpec((tm, D_MODEL), lambda i, j: (i, 0)),
            pl.BlockSpec((1, D_MODEL), lambda i, j: (0, 0)),
            pl.BlockSpec((D_MODEL, bn), lambda i, j: (0, j)),
            pl.BlockSpec((D_MODEL, bn), lambda i, j: (0, j + nj)),
            pl.BlockSpec((FFN_CONV_K, bn), lambda i, j: (0, j)),
            pl.BlockSpec((FFN_CONV_K, bn), lambda i, j: (0, j + nj)),
            pl.BlockSpec((bn, D_MODEL), lambda i, j: (j, 0)),
        ],
        out_specs=pl.BlockSpec((tm, D_MODEL), lambda i, j: (i, 0)),
        out_shape=jax.ShapeDtypeStruct((m, D_MODEL), F32),
        scratch_shapes=[pltpu.VMEM((tm, D_MODEL), BF16),
                        pltpu.VMEM((2, tm + halo, bn), F32),
                        pltpu.VMEM((2, nj, halo, bn), F32)],
        compiler_params=_params(("arbitrary", "arbitrary")),
        name="conv_ffn",
    )(x1, g, w_up, w_up, conv_w, conv_w, w_down)


def _ple_kernel(x_ref, p_ref, g_ref, wpg_ref, wpp_ref, fg_ref, out_ref):
    x = x_ref[...]
    hb = (x * _rms_scale(x) * g_ref[...]).astype(BF16)
    gate = jax.nn.sigmoid(jnp.dot(hb, wpg_ref[...], preferred_element_type=F32))
    emb = jnp.dot(p_ref[...].astype(BF16), wpp_ref[...], preferred_element_type=F32)
    y = x + gate * emb
    out_ref[...] = y * _rms_scale(y) * fg_ref[...]


def _ple(x2, p2, g, w_pg, w_pp, final_g, *, tm):
    m = x2.shape[0]
    return pl.pallas_call(
        _ple_kernel,
        grid=(m // tm,),
        in_specs=[
            pl.BlockSpec((tm, D_MODEL), lambda i: (i, 0)),
            pl.BlockSpec((tm, PLE_DIM), lambda i: (i, 0)),
            pl.BlockSpec((1, D_MODEL), lambda i: (0, 0)),
            pl.BlockSpec((D_MODEL, D_MODEL), lambda i: (0, 0)),
            pl.BlockSpec((PLE_DIM, D_MODEL), lambda i: (0, 0)),
            pl.BlockSpec((1, D_MODEL), lambda i: (0, 0)),
        ],
        out_specs=pl.BlockSpec((tm, D_MODEL), lambda i: (i, 0)),
        out_shape=jax.ShapeDtypeStruct((m, D_MODEL), F32),
        compiler_params=_params(("parallel",)),
        name="ple",
    )(x2, p2, g, w_pg, w_pp, final_g)


def _layer(x2, p2, seq, norm_mix_g, w_in, conv_a_w, conv_qkv_w, a_log, dt_bias, dn_norm_g,
           w_out, norm_ffn_g, w_up, conv_ffn_w, w_down, norm_ple_g, w_ple_gate, w_ple_proj,
           out_norm_g):
    row = lambda v: v.reshape(1, -1).astype(F32)
    lane_pad = lambda v: jnp.pad(row(v), ((0, 0), (0, LANES - v.shape[-1])))
    w_main = w_in.astype(BF16)
    w_gate = jnp.pad(w_in[:, PROJ_COLS:], ((0, 0), (0, LANES - 2 * DN_HEADS))).astype(BF16)

    proj, gates = _in_proj(x2, row(norm_mix_g), w_main, w_gate, lane_pad(a_log), lane_pad(dt_bias),
                           tm=IN_TM, tn=IN_TN)
    yb, (w_out_b, w_up_b, w_down_b, w_pg_b) = _deltanet(
        proj, gates, conv_qkv_w.astype(F32), row(dn_norm_g), (w_out, w_up, w_down, w_ple_gate),
        seq=seq, ts=DN_TS, chunk=DN_CHUNK)
    x2 = _out_proj(proj, yb, x2, conv_a_w.astype(F32), w_out_b, tm=OUT_TM, seq=seq)
    x2 = _ffn(x2, row(norm_ffn_g), w_up_b, conv_ffn_w.astype(F32), w_down_b,
              tm=FFN_TM, bn=FFN_BN, seq=seq)
    return _ple(x2, p2, row(norm_ple_g), w_pg_b, w_ple_proj.astype(BF16),
                row(out_norm_g), tm=PLE_TM)


def kernel(x, p, norm_mix_g, w_in, conv_a_w, conv_qkv_w, a_log, dt_bias, dn_norm_g, w_out,
           norm_ffn_g, w_up, conv_ffn_w, w_down, norm_ple_g, w_ple_gate, w_ple_proj, final_norm_g):
    batch, seq, d_model = x.shape
    depth = p.shape[0]
    assert depth == 1 and d_model == D_MODEL
    x2 = x.reshape(batch * seq, d_model)
    p2 = p[0].reshape(batch * seq, PLE_DIM)
    out = _layer(x2, p2, seq, norm_mix_g[0], w_in[0], conv_a_w[0], conv_qkv_w[0], a_log[0],
                 dt_bias[0], dn_norm_g[0], w_out[0], norm_ffn_g[0], w_up[0], conv_ffn_w[0],
                 w_down[0], norm_ple_g[0], w_ple_gate[0], w_ple_proj[0], final_norm_g)
    return out.reshape(batch, seq, d_model)
```

```python
import functools

import jax
import jax.numpy as jnp
from jax import lax
from jax.experimental import pallas as pl
from jax.experimental.pallas import tpu as pltpu

D_MODEL = 2048
CONV_WIDTH = 1024
CONV_K = 3
DN_HEADS = 8
DN_HEAD_DIM = 128
DN_WIDTH = DN_HEADS * DN_HEAD_DIM
DN_CONV_K = 4
D_FF = 5632
FFN_CONV_K = 3
PLE_DIM = 256
EPS = 1e-6
PROJ_COLS = 3 * CONV_WIDTH + 4 * DN_WIDTH

LANES = 128
SUBLANES_F32 = 8
SUBLANES_BF16 = 16
V7X_VMEM_BYTES = 64 * 1024 * 1024
VMEM_LIMIT_BYTES = V7X_VMEM_BYTES - 2 * 1024 * 1024

IN_TM, IN_TN = 1024, 1024
DN_TS = 256
OUT_TM = 512
FFN_TM, FFN_BN = 1024, 512
PLE_TM = 512

DN_CHUNK = 128
INV_BASE = 8

F32 = jnp.float32
BF16 = jnp.bfloat16


def _mm(a, b):
    return jnp.dot(a.astype(BF16), b.astype(BF16), preferred_element_type=F32)


def _mm_nt(a, b):
    return lax.dot_general(a.astype(BF16), b.astype(BF16), (((1,), (1,)), ((), ())),
                           preferred_element_type=F32)


def _mm_exact(a, b):
    return jnp.dot(a, b, precision=lax.Precision.HIGHEST, preferred_element_type=F32)


def _rms_scale(x):
    return lax.rsqrt(jnp.mean(x * x, axis=-1, keepdims=True) + EPS)


def _silu(x):
    return x * jax.nn.sigmoid(x)


def _params(semantics):
    return pltpu.CompilerParams(dimension_semantics=semantics, vmem_limit_bytes=VMEM_LIMIT_BYTES)


def _in_proj_kernel(x_ref, g_ref, w_ref, wg_ref, alog_ref, dtb_ref, proj_ref, gates_ref, hs_ref):
    @pl.when(pl.program_id(1) == 0)
    def _():
        x = x_ref[...]
        hb = (x * _rms_scale(x) * g_ref[...]).astype(BF16)
        hs_ref[...] = hb
        raw = jnp.dot(hb, wg_ref[...], preferred_element_type=F32)
        a = raw + dtb_ref[...]
        softplus = jnp.maximum(a, 0.0) + jnp.log(1.0 + jnp.exp(-jnp.abs(a)))
        decay = -jnp.exp(alog_ref[...]) * softplus
        lane = lax.broadcasted_iota(jnp.int32, raw.shape, 1)
        gates_ref[...] = jnp.where(lane < DN_HEADS, decay, jax.nn.sigmoid(raw))

    tn = proj_ref.shape[1]
    for block in range(PROJ_COLS // tn):
        @pl.when(pl.program_id(1) == block)
        def _(block=block):
            proj_ref[...] = jnp.dot(hs_ref[...], w_ref[:, pl.ds(block * tn, tn)],
                                    preferred_element_type=F32)


def _in_proj(x2, g, w_main, w_gate, alog_pad, dtb_pad, *, tm, tn):
    m = x2.shape[0]
    return pl.pallas_call(
        _in_proj_kernel,
        grid=(m // tm, PROJ_COLS // tn),
        in_specs=[
            pl.BlockSpec((tm, D_MODEL), lambda i, j: (i, 0)),
            pl.BlockSpec((1, D_MODEL), lambda i, j: (0, 0)),
            pl.BlockSpec(w_main.shape, lambda i, j: (0, 0), pipeline_mode=pl.Buffered(1)),
            pl.BlockSpec((D_MODEL, LANES), lambda i, j: (0, 0)),
            pl.BlockSpec((1, LANES), lambda i, j: (0, 0)),
            pl.BlockSpec((1, LANES), lambda i, j: (0, 0)),
        ],
        out_specs=[
            pl.BlockSpec((tm, tn), lambda i, j: (i, j)),
            pl.BlockSpec((tm, LANES), lambda i, j: (i, 0)),
        ],
        out_shape=[
            jax.ShapeDtypeStruct((m, PROJ_COLS), F32),
            jax.ShapeDtypeStruct((m, LANES), F32),
        ],
        scratch_shapes=[pltpu.VMEM((tm, D_MODEL), BF16)],
        compiler_params=_params(("parallel", "arbitrary")),
        name="in_proj",
    )(x2, g, w_main, w_gate, alog_pad, dtb_pad)


def _each(fn, *lists):
    return [fn(*args) for args in zip(*lists)]


def _unit_lower_inverse(lows, row, col):
    c = lows[0].shape[0]
    eye = (row == col).astype(F32)
    base_shift = INV_BASE.bit_length() - 1
    base_mask = (row >> base_shift) == (col >> base_shift)
    l0 = [jnp.where(base_mask, low, 0.0) for low in lows]
    l2 = _each(_mm, l0, l0)
    l4 = _each(_mm, l2, l2)
    l3 = _each(_mm, l0, l2)
    p1 = _each(lambda a, b, d: eye - a + b - d, l0, l2, l3)
    inv = _each(lambda p, pl4: p + pl4, p1, _each(_mm, p1, l4))
    shift = base_shift
    while (1 << shift) < c:
        off_mask = (((row >> (shift + 1)) == (col >> (shift + 1)))
                    & ((row >> shift) != (col >> shift)))
        off_inv = _each(lambda low, t: _mm(jnp.where(off_mask, low, 0.0), t), lows, inv)
        inv = _each(lambda t, x: t - _mm(t, x), inv, off_inv)
        shift += 1
    return inv


def _deltanet_kernel(q_ref, k_ref, v_ref, z_ref, gates_ref, cw_ref, ng_ref, *refs,
                     ts, chunk, n_cast):
    cast_src = refs[:n_cast]
    out_ref = refs[n_cast]
    cast_dst = refs[n_cast + 1:2 * n_cast + 1]
    qbuf, kbuf, vbuf, state_ref, u_s, wq_s, qk_s, kdt_s = refs[2 * n_cast + 1:]

    halo = SUBLANES_F32
    first = pl.program_id(1) == 0

    @pl.when(first)
    def _():
        state_ref[...] = jnp.zeros_like(state_ref)
        for buf in (qbuf, kbuf, vbuf):
            buf[pl.ds(0, halo), :] = jnp.zeros((halo, DN_WIDTH), F32)

    @pl.when(jnp.logical_not(first))
    def _():
        for buf in (qbuf, kbuf, vbuf):
            buf[pl.ds(0, halo), :] = buf[pl.ds(ts, halo), :]

    for src, buf in ((q_ref, qbuf), (k_ref, kbuf), (v_ref, vbuf)):
        buf[pl.ds(halo, ts), :] = src[...]
    for src, dst in zip(cast_src, cast_dst):
        dst[...] = src[...].astype(dst.dtype)

    row = lax.broadcasted_iota(jnp.int32, (chunk, chunk), 0)
    col = lax.broadcasted_iota(jnp.int32, (chunk, chunk), 1)
    lower_incl = (row >= col).astype(F32)
    causal = row >= col
    strict = row > col
    scale = DN_HEAD_DIM ** -0.5
    heads = range(DN_HEADS)
    chunks = range(ts // chunk)
    head_cols = [slice(h * DN_HEAD_DIM, (h + 1) * DN_HEAD_DIM) for h in heads]
    pairs = [(c, h) for c in chunks for h in heads]
    slot = lambda c, h: c * DN_HEADS + h

    def l2_normalized(x, extra_scale):
        return x * (lax.rsqrt(jnp.sum(x * x, axis=-1, keepdims=True) + EPS) * extra_scale)

    def conv_silu(buf, c, which):
        w = [cw_ref[pl.ds(j, 1), pl.ds(which * DN_WIDTH, DN_WIDTH)] for j in range(DN_CONV_K)]
        cur = buf[pl.ds(c * chunk, chunk + halo), :]
        prev = pltpu.roll(cur, 1, axis=0)
        tail = pltpu.roll(w[1] * cur + w[0] * prev, 2, axis=0)
        return _silu((w[3] * cur + w[2] * prev + tail)[halo:, :])

    qa = [conv_silu(qbuf, c, 0) for c in chunks]
    ka = [conv_silu(kbuf, c, 1) for c in chunks]
    va = [conv_silu(vbuf, c, 2) for c in chunks]
    gts = [gates_ref[pl.ds(c * chunk, chunk), :] for c in chunks]
    gcum = [_mm_exact(lower_incl, g) for g in gts]
    gcum_t = [g.T for g in gcum]
    q = [l2_normalized(qa[c][:, head_cols[h]], scale) for c, h in pairs]
    k = [l2_normalized(ka[c][:, head_cols[h]], 1.0) for c, h in pairs]
    v = [va[c][:, head_cols[h]] for c, h in pairs]
    gc = [gcum[c][:, h:h + 1] for c, h in pairs]
    gr = [gcum_t[c][h:h + 1, :] for c, h in pairs]
    beta = [gts[c][:, DN_HEADS + h:DN_HEADS + h + 1] for c, h in pairs]
    g_last = [g[chunk - 1:chunk, :] for g in gc]
    decay = _each(lambda a, b: jnp.exp(jnp.where(causal, a - b, -1e30)), gc, gr)
    kk = _each(_mm_nt, k, k)
    qk = _each(lambda a, b, d: _mm_nt(a, b) * d, q, k, decay)
    low = _each(lambda m, d, b: jnp.where(strict, m * d * b, 0.0), kk, decay, beta)
    inv = _unit_lower_inverse(low, row, col)
    eg = _each(jnp.exp, gc)
    rhs = _each(lambda vv, kx, b, e: jnp.concatenate([vv * b, kx * (b * e)], axis=1),
                v, k, beta, eg)
    sol = _each(_mm, inv, rhs)
    for (c, h), s, qx, e, a, kx, gl, g in zip(pairs, sol, q, eg, qk, k, g_last, gc):
        i = slot(c, h)
        u_s[i] = s[:, :DN_HEAD_DIM]
        wq_s[i] = jnp.concatenate([s[:, DN_HEAD_DIM:], qx * e], axis=0).astype(BF16)
        qk_s[i] = a.astype(BF16)
        kdt_s[i] = (kx * jnp.exp(gl - g)).T.astype(BF16)
    carry_decay = _each(jnp.exp, g_last)

    state = [state_ref[h] for h in heads]
    for c in chunks:
        ids = [slot(c, h) for h in heads]
        ws_qs = [jnp.dot(wq_s[i], st.astype(BF16), preferred_element_type=F32)
                 for i, st in zip(ids, state)]
        v_new = [(u_s[i] - x[:chunk]).astype(BF16) for i, x in zip(ids, ws_qs)]
        state = [st * carry_decay[i] + jnp.dot(kdt_s[i], vn, preferred_element_type=F32)
                 for i, st, vn in zip(ids, state, v_new)]
        o = [x[chunk:] + jnp.dot(qk_s[i], vn, preferred_element_type=F32)
             for i, x, vn in zip(ids, ws_qs, v_new)]
        rows = pl.ds(c * chunk, chunk)
        for h, sl in enumerate(head_cols):
            y = o[h] * _rms_scale(o[h]) * ng_ref[...] * _silu(z_ref[rows, sl])
            out_ref[rows, sl] = y.astype(out_ref.dtype)
    for h in heads:
        state_ref[h] = state[h]


def _deltanet(proj, gates, conv_w, norm_g, cast_weights, *, seq, ts, chunk):
    m = proj.shape[0]
    tiles = seq // ts
    steps = (m // seq) * tiles
    pairs = (ts // chunk) * DN_HEADS
    col0 = 3 * CONV_WIDTH // DN_WIDTH
    row_map = lambda b, t: b * tiles + t
    qkvz = [pl.BlockSpec((ts, DN_WIDTH), functools.partial(lambda b, t, n: (b * tiles + t, col0 + n), n=n))
            for n in range(4)]
    cast_specs = []
    for w in cast_weights:
        share = 1 if w.shape[0] % (steps * SUBLANES_BF16) == 0 else 2
        assert w.shape[0] % (steps // share * SUBLANES_BF16) == 0, w.shape
        cast_specs.append(pl.BlockSpec(
            (w.shape[0] // (steps // share), w.shape[1]),
            functools.partial(lambda b, t, share: (row_map(b, t) // share, 0), share=share)))
    outs = pl.pallas_call(
        functools.partial(_deltanet_kernel, ts=ts, chunk=chunk, n_cast=len(cast_weights)),
        grid=(m // seq, tiles),
        in_specs=qkvz + [
            pl.BlockSpec((ts, LANES), lambda b, t: (row_map(b, t), 0)),
            pl.BlockSpec((DN_CONV_K, 3 * DN_WIDTH), lambda b, t: (0, 0)),
            pl.BlockSpec((1, DN_HEAD_DIM), lambda b, t: (0, 0)),
        ] + cast_specs,
        out_specs=[pl.BlockSpec((ts, DN_WIDTH), lambda b, t: (row_map(b, t), 0))] + cast_specs,
        out_shape=[jax.ShapeDtypeStruct((m, DN_WIDTH), BF16)]
        + [jax.ShapeDtypeStruct(w.shape, BF16) for w in cast_weights],
        scratch_shapes=[pltpu.VMEM((ts + SUBLANES_F32, DN_WIDTH), F32)] * 3 + [
            pltpu.VMEM((DN_HEADS, DN_HEAD_DIM, DN_HEAD_DIM), F32),
            pltpu.VMEM((pairs, chunk, DN_HEAD_DIM), F32),
            pltpu.VMEM((pairs, 2 * chunk, DN_HEAD_DIM), BF16),
            pltpu.VMEM((pairs, chunk, chunk), BF16),
            pltpu.VMEM((pairs, DN_HEAD_DIM, chunk), BF16),
        ],
        compiler_params=_params(("parallel", "arbitrary")),
        name="deltanet",
    )(proj, proj, proj, proj, gates, conv_w, norm_g, *cast_weights)
    return outs[0], outs[1:]


RING = 3


def _out_proj_kernel(proj_hbm, yb_hbm, x_hbm, hax_ref, hac_ref, cw_ref, w_ref, out_ref,
                     a_buf, yb_buf, x_buf, sems, pbuf, *, tm, seq, steps):
    halo = SUBLANES_F32
    step = pl.program_id(0)

    def tile_copies(t, slot):
        rows = pl.ds(t * tm, tm)
        copies = [pltpu.make_async_copy(proj_hbm.at[rows, pl.ds(n * CONV_WIDTH, CONV_WIDTH)],
                                        a_buf.at[slot, n], sems.at[n, slot]) for n in range(3)]
        copies.append(pltpu.make_async_copy(yb_hbm.at[rows, :], yb_buf.at[slot], sems.at[3, slot]))
        copies.append(pltpu.make_async_copy(x_hbm.at[rows, :], x_buf.at[slot], sems.at[4, slot]))
        return copies

    @pl.when(step == 0)
    def _():
        for t in range(RING - 1):
            for copy in tile_copies(t, t):
                copy.start()

    ahead = step + (RING - 1)

    @pl.when(ahead < steps)
    def _():
        for copy in tile_copies(ahead, ahead % RING):
            copy.start()

    slot = step % RING
    for copy in tile_copies(step, slot):
        copy.wait()

    acc_b = jnp.dot(yb_buf[slot], w_ref[pl.ds(CONV_WIDTH, DN_WIDTH), :], preferred_element_type=F32)
    seq_start = (step * tm) % seq == 0
    pbuf[pl.ds(0, halo), :] = jnp.where(seq_start, 0.0, hac_ref[...] * hax_ref[...])
    pbuf[pl.ds(halo, tm), :] = a_buf[slot, 2] * a_buf[slot, 0]
    acc = None
    for j in range(CONV_K):
        term = pbuf[pl.ds(halo - (CONV_K - 1) + j, tm), :] * cw_ref[pl.ds(j, 1), :]
        acc = term if acc is None else acc + term
    ya = (a_buf[slot, 1] * acc).astype(BF16)
    out_ref[...] = (x_buf[slot] + acc_b
                    + jnp.dot(ya, w_ref[pl.ds(0, CONV_WIDTH), :], preferred_element_type=F32))


def _out_proj(proj, yb, x2, conv_w, w_out, *, tm, seq):
    m = x2.shape[0]
    steps = m // tm
    assert steps >= RING
    halo = SUBLANES_F32
    halo_row = lambda i: jnp.maximum(i * (tm // halo) - 1, 0)
    return pl.pallas_call(
        functools.partial(_out_proj_kernel, tm=tm, seq=seq, steps=steps),
        grid=(steps,),
        in_specs=[
            pl.BlockSpec(memory_space=pl.ANY),
            pl.BlockSpec(memory_space=pl.ANY),
            pl.BlockSpec(memory_space=pl.ANY),
            pl.BlockSpec((halo, CONV_WIDTH), lambda i: (halo_row(i), 0)),
            pl.BlockSpec((halo, CONV_WIDTH), lambda i: (halo_row(i), 2)),
            pl.BlockSpec((CONV_K, CONV_WIDTH), lambda i: (0, 0)),
            pl.BlockSpec((D_MODEL, D_MODEL), lambda i: (0, 0), pipeline_mode=pl.Buffered(1)),
        ],
        out_specs=pl.BlockSpec((tm, D_MODEL), lambda i: (i, 0)),
        out_shape=jax.ShapeDtypeStruct((m, D_MODEL), F32),
        scratch_shapes=[
            pltpu.VMEM((RING, 3, tm, CONV_WIDTH), F32),
            pltpu.VMEM((RING, tm, DN_WIDTH), BF16),
            pltpu.VMEM((RING, tm, D_MODEL), F32),
            pltpu.SemaphoreType.DMA((5, RING)),
            pltpu.VMEM((tm + halo, CONV_WIDTH), F32),
        ],
        compiler_params=_params(("arbitrary",)),
        name="out_proj",
    )(proj, yb, x2, proj, proj, conv_w, w_out)


def _ffn_kernel(x_ref, g_ref, wg_ref, wv_ref, cg_ref, cv_ref, wd_ref, out_ref,
                hs_ref, ubuf, carry, *, tm, seq):
    halo = SUBLANES_F32
    i = pl.program_id(0)
    j = pl.program_id(1)
    seq_start = (i * tm) % seq == 0

    @pl.when(jnp.logical_and(i == 0, j == 0))
    def _():
        carry[...] = jnp.zeros_like(carry)

    @pl.when(j == 0)
    def _():
        x = x_ref[...]
        hs_ref[...] = (x * _rms_scale(x) * g_ref[...]).astype(BF16)
        out_ref[...] = x

    def up_conv(w_ref, cw_ref, which, cs):
        up = jnp.dot(hs_ref[...], w_ref[:, cs], preferred_element_type=F32)
        ubuf[which, pl.ds(0, halo), cs] = jnp.where(seq_start, 0.0, carry[which, j, :, cs])
        ubuf[which, pl.ds(halo, tm), cs] = up
        carry[which, j, :, cs] = up[tm - halo:, :]
        acc = None
        for t in range(FFN_CONV_K):
            term = (ubuf[which, pl.ds(halo - (FFN_CONV_K - 1) + t, tm), cs]
                    * cw_ref[pl.ds(t, 1), cs])
            acc = term if acc is None else acc + term
        return acc

    bn = wd_ref.shape[0]
    halves = [pl.ds(h * (bn // 2), bn // 2) for h in range(2)]
    gates = [_silu(up_conv(wg_ref, cg_ref, 0, cs)) for cs in halves]
    acts = [(g * up_conv(wv_ref, cv_ref, 1, cs)).astype(BF16) for g, cs in zip(gates, halves)]
    for cs, act in zip(halves, acts):
        out_ref[...] += jnp.dot(act, wd_ref[cs, :], preferred_element_type=F32)


def _ffn(x1, g, w_up, conv_w, w_down, *, tm, bn, seq):
    m = x1.shape[0]
    halo = SUBLANES_F32
    nj = D_FF // bn
    return pl.pallas_call(
        functools.partial(_ffn_kernel, tm=tm, seq=seq),
        grid=(m // tm, nj),
        in_specs=[
            pl.BlockSpec((tm, D_MODEL), lambda i, j: (i, 0)),
            pl.BlockSpec((1, D_MODEL), lambda i, j: (0, 0)),
            pl.BlockSpec((D_MODEL, bn), lambda i, j: (0, j)),
            pl.BlockSpec((D_MODEL, bn), lambda i, j: (0, j + nj)),
            pl.BlockSpec((FFN_CONV_K, bn), lambda i, j: (0, j)),
            pl.BlockSpec((FFN_CONV_K, bn), lambda i, j: (0, j + nj)),
            pl.BlockSpec((bn, D_MODEL), lambda i, j: (j, 0)),
        ],
        out_specs=pl.BlockSpec((tm, D_MODEL), lambda i, j: (i, 0)),
        out_shape=jax.ShapeDtypeStruct((m, D_MODEL), F32),
        scratch_shapes=[pltpu.VMEM((tm, D_MODEL), BF16),
                        pltpu.VMEM((2, tm + halo, bn), F32),
                        pltpu.VMEM((2, nj, halo, bn), F32)],
        compiler_params=_params(("arbitrary", "arbitrary")),
        name="conv_ffn",
    )(x1, g, w_up, w_up, conv_w, conv_w, w_down)


def _ple_kernel(x_hbm, p_hbm, g_ref, wpg_ref, wpp_ref, fg_ref, out_ref,
                x_buf, p_buf, sems, *, tm, steps):
    step = pl.program_id(0)

    def tile_copies(t, slot):
        rows = pl.ds(t * tm, tm)
        return [pltpu.make_async_copy(x_hbm.at[rows, :], x_buf.at[slot], sems.at[0, slot]),
                pltpu.make_async_copy(p_hbm.at[rows, :], p_buf.at[slot], sems.at[1, slot])]

    @pl.when(step == 0)
    def _():
        for t in range(RING - 1):
            for copy in tile_copies(t, t):
                copy.start()

    ahead = step + (RING - 1)

    @pl.when(ahead < steps)
    def _():
        for copy in tile_copies(ahead, ahead % RING):
            copy.start()

    slot = step % RING
    for copy in tile_copies(step, slot):
        copy.wait()

    x = x_buf[slot]
    hb = (x * _rms_scale(x) * g_ref[...]).astype(BF16)
    gate = jax.nn.sigmoid(jnp.dot(hb, wpg_ref[...], preferred_element_type=F32))
    emb = jnp.dot(p_buf[slot].astype(BF16), wpp_ref[...], preferred_element_type=F32)
    y = x + gate * emb
    out_ref[...] = y * _rms_scale(y) * fg_ref[...]


def _ple(x2, p2, g, w_pg, w_pp, final_g, *, tm):
    m = x2.shape[0]
    steps = m // tm
    assert steps >= RING
    return pl.pallas_call(
        functools.partial(_ple_kernel, tm=tm, steps=steps),
        grid=(steps,),
        in_specs=[
            pl.BlockSpec(memory_space=pl.ANY),
            pl.BlockSpec(memory_space=pl.ANY),
            pl.BlockSpec((1, D_MODEL), lambda i: (0, 0)),
            pl.BlockSpec((D_MODEL, D_MODEL), lambda i: (0, 0)),
            pl.BlockSpec((PLE_DIM, D_MODEL), lambda i: (0, 0)),
            pl.BlockSpec((1, D_MODEL), lambda i: (0, 0)),
        ],
        out_specs=pl.BlockSpec((tm, D_MODEL), lambda i: (i, 0)),
        out_shape=jax.ShapeDtypeStruct((m, D_MODEL), F32),
        scratch_shapes=[
            pltpu.VMEM((RING, tm, D_MODEL), F32),
            pltpu.VMEM((RING, tm, PLE_DIM), F32),
            pltpu.SemaphoreType.DMA((2, RING)),
        ],
        compiler_params=_params(("arbitrary",)),
        name="ple",
    )(x2, p2, g, w_pg, w_pp, final_g)


def _layer(x2, p2, seq, norm_mix_g, w_in, conv_a_w, conv_qkv_w, a_log, dt_bias, dn_norm_g,
           w_out, norm_ffn_g, w_up, conv_ffn_w, w_down, norm_ple_g, w_ple_gate, w_ple_proj,
           out_norm_g):
    row = lambda v: v.reshape(1, -1).astype(F32)
    lane_pad = lambda v: jnp.pad(row(v), ((0, 0), (0, LANES - v.shape[-1])))
    w_main = w_in.astype(BF16)
    w_gate = jnp.pad(w_in[:, PROJ_COLS:], ((0, 0), (0, LANES - 2 * DN_HEADS))).astype(BF16)

    proj, gates = _in_proj(x2, row(norm_mix_g), w_main, w_gate, lane_pad(a_log), lane_pad(dt_bias),
                           tm=IN_TM, tn=IN_TN)
    yb, (w_out_b, w_up_b, w_down_b, w_pg_b) = _deltanet(
        proj, gates, conv_qkv_w.astype(F32), row(dn_norm_g), (w_out, w_up, w_down, w_ple_gate),
        seq=seq, ts=DN_TS, chunk=DN_CHUNK)
    x2 = _out_proj(proj, yb, x2, conv_a_w.astype(F32), w_out_b, tm=OUT_TM, seq=seq)
    x2 = _ffn(x2, row(norm_ffn_g), w_up_b, conv_ffn_w.astype(F32), w_down_b,
              tm=FFN_TM, bn=FFN_BN, seq=seq)
    return _ple(x2, p2, row(norm_ple_g), w_pg_b, w_ple_proj.astype(BF16),
                row(out_norm_g), tm=PLE_TM)


def kernel(x, p, norm_mix_g, w_in, conv_a_w, conv_qkv_w, a_log, dt_bias, dn_norm_g, w_out,
           norm_ffn_g, w_up, conv_ffn_w, w_down, norm_ple_g, w_ple_gate, w_ple_proj, final_norm_g):
    batch, seq, d_model = x.shape
    depth = p.shape[0]
    assert depth == 1 and d_model == D_MODEL
    x2 = x.reshape(batch * seq, d_model)
    p2 = p[0].reshape(batch * seq, PLE_DIM)
    out = _layer(x2, p2, seq, norm_mix_g[0], w_in[0], conv_a_w[0], conv_qkv_w[0], a_log[0],
                 dt_bias[0], dn_norm_g[0], w_out[0], norm_ffn_g[0], w_up[0], conv_ffn_w[0],
                 w_down[0], norm_ple_g[0], w_ple_gate[0], w_ple_proj[0], final_norm_g)
    return out.reshape(batch, seq, d_model)
```
